```python
import math, functools
import jax, jax.numpy as jnp
from jax import lax
import numpy as np

D_MODEL = 1024
BATCH = 2
SEQ = 8192
DEPTH = 4
DEC_BATCH = 32
DEC_SEQ = 8
PAST_LEN = 8192
PAGE_SIZE = 128

HEAD_DIM = 64
A_WIDTH = D_MODEL // 4
C_WIDTH = 3 * D_MODEL // 8
B_WIDTH = D_MODEL - A_WIDTH - C_WIDTH
MIX_WIDTH = A_WIDTH + B_WIDTH + C_WIDTH
A_HEADS = A_WIDTH // HEAD_DIM
C_HEADS = C_WIDTH // HEAD_DIM
CHUNK = 128
SSM_GROUP = 16
B_GROUPS = B_WIDTH // SSM_GROUP
SSM_STATE = 64
DILATIONS = ((128, 1), (512, 4), (2048, 16))
MAX_WINDOW = 2048
Q_BLOCK = 128
ROPE_THETA = 500000.0
ROT_DIM = HEAD_DIM // 4
D_FF = ((8 * D_MODEL // 3 + 127) // 128) * 128
CONV_W = 3
EPS = 1e-6
IN_WIDTH = 2 * A_WIDTH + B_WIDTH + 3 * C_WIDTH
NEG_INF = -1e30

kernel_name = 'hymba_sgu_s5_dilated_convffn_step'


def rmsnorm(x, g):
    xf = x.astype(jnp.float32)
    y = xf * lax.rsqrt(jnp.mean(xf * xf, -1, keepdims=True) + EPS)
    return (y * g.astype(jnp.float32)).astype(x.dtype)


def rope_partial(x, pos):
    half = ROT_DIM // 2
    inv = ROPE_THETA ** (-(jnp.arange(half, dtype=jnp.float32) * 2.0 / ROT_DIM))
    ang = pos.astype(jnp.float32)[:, None] * inv[None, :]
    cos = jnp.cos(ang)[None, :, None, :]
    sin = jnp.sin(ang)[None, :, None, :]
    xr = x[..., :ROT_DIM].astype(jnp.float32)
    x1, x2 = xr[..., :half], xr[..., half:]
    rot = jnp.concatenate([x1 * cos - x2 * sin, x2 * cos + x1 * sin], -1)
    return jnp.concatenate([rot.astype(x.dtype), x[..., ROT_DIM:]], -1)


def sgu_mixer(uv, w_s, b_s):
    bsz, s, _ = uv.shape
    h = jax.nn.gelu(uv)
    u, v = h[..., :A_WIDTH], h[..., A_WIDTH:]
    vf = v.astype(jnp.float32)
    mu = jnp.mean(vf, -1, keepdims=True)
    var = jnp.mean(jnp.square(vf - mu), -1, keepdims=True)
    vn = ((vf - mu) * lax.rsqrt(var + EPS)).astype(uv.dtype)
    L = min(s, CHUNK)
    nc = s // L
    mask = jnp.tril(jnp.ones((L, L), dtype=bool))
    w = jnp.where(mask[None], w_s[:, :L, :L], 0.0)
    vh = vn.reshape(bsz, nc, L, A_HEADS, HEAD_DIM)
    mixed = jnp.einsum('hij,bcjhd->bcihd', w, vh) + jnp.transpose(b_s[:, :L])[None, None, :, :, None]
    return u * mixed.reshape(bsz, s, A_WIDTH), vn


def s5_mixer(u, h0_re, h0_im, lam_re, lam_im, log_dt, b_re, b_im, c_re, c_im, d_skip, w_glu, b_glu):
    bsz, s, _ = u.shape
    f32 = jnp.float32
    dt = jnp.exp(log_dt.astype(f32))[:, None]
    lr, li = lam_re.astype(f32), lam_im.astype(f32)
    mag = jnp.exp(lr * dt)
    ar, ai = mag * jnp.cos(li * dt), mag * jnp.sin(li * dt)
    nr, ni = ar - 1.0, ai
    den = lr * lr + li * li
    fr, fi = (nr * lr + ni * li) / den, (ni * lr - nr * li) / den
    br, bi = b_re.astype(f32), b_im.astype(f32)
    bbr = fr[..., None] * br - fi[..., None] * bi
    bbi = fr[..., None] * bi + fi[..., None] * br
    ug = u.astype(f32).reshape(bsz, s, B_GROUPS, SSM_GROUP)
    xr = jnp.einsum('bsgc,gpc->bsgp', ug, bbr)
    xi = jnp.einsum('bsgc,gpc->bsgp', ug, bbi)
    h0r, h0i = h0_re.astype(f32), h0_im.astype(f32)
    xr = xr.at[:, 0].add(ar * h0r - ai * h0i)
    xi = xi.at[:, 0].add(ar * h0i + ai * h0r)
    a_r = jnp.broadcast_to(ar[None, None], (1, s, B_GROUPS, SSM_STATE))
    a_i = jnp.broadcast_to(ai[None, None], (1, s, B_GROUPS, SSM_STATE))

    def combine(e1, e2):
        a1r, a1i, b1r, b1i = e1
        a2r, a2i, b2r, b2i = e2
        return (a1r * a2r - a1i * a2i, a1r * a2i + a1i * a2r,
                a2r * b1r - a2i * b1i + b2r, a2r * b1i + a2i * b1r + b2i)

    _, _, hr, hi = lax.associative_scan(combine, (a_r, a_i, xr, xi), axis=1)
    y = (jnp.einsum('bsgp,gcp->bsgc', hr, c_re.astype(f32))
         - jnp.einsum('bsgp,gcp->bsgc', hi, c_im.astype(f32)))
    y = y.reshape(bsz, s, B_WIDTH) + d_skip.astype(f32) * u.astype(f32)
    g = jax.nn.gelu(y).astype(u.dtype)
    out = g * jax.nn.sigmoid(g @ w_glu + b_glu)
    return out, hr[:, -1], hi[:, -1]


def dilated_attention(q, k_src, v_src, q_pos, src_base):
    qf = q.astype(jnp.float32) * (HEAD_DIM ** -0.5)
    lses, outs = [], []
    for window, dil in DILATIONS:
        n_keys = window // dil + 1
        key_pos = q_pos[:, None] - dil * jnp.arange(n_keys)[None, :]
        valid = key_pos >= 0
        idx = jnp.clip(key_pos - src_base, 0, k_src.shape[1] - 1)
        kg = jnp.take(k_src, idx, axis=1).astype(jnp.float32)
        vg = jnp.take(v_src, idx, axis=1).astype(jnp.float32)
        sc = jnp.einsum('bqhd,bqjhd->bqhj', qf, kg)
        sc = jnp.where(valid[None, :, None, :], sc, NEG_INF)
        m = jnp.max(sc, -1, keepdims=True)
        p = jnp.exp(sc - m)
        l = jnp.sum(p, -1, keepdims=True)
        outs.append(jnp.einsum('bqhj,bqjhd->bqhd', p, vg) / l)
        lses.append((m + jnp.log(l))[..., 0])
    alpha = jax.nn.softmax(jnp.stack(lses, 0), axis=0)
    return jnp.sum(alpha[..., None] * jnp.stack(outs, 0), 0)


def attend_prompt(q, k, v):
    bsz, s = q.shape[:2]
    nb = s // Q_BLOCK
    qb = jnp.transpose(q.reshape(bsz, nb, Q_BLOCK, C_HEADS, HEAD_DIM), (1, 0, 2, 3, 4))

    def one(args):
        blk, qblk = args
        pos = blk * Q_BLOCK + jnp.arange(Q_BLOCK)
        return dilated_attention(qblk, k, v, pos, 0).astype(q.dtype)

    out = lax.map(one, (jnp.arange(nb), qb))
    return jnp.transpose(out, (1, 0, 2, 3, 4)).reshape(bsz, s, C_WIDTH)


def attend_sample(k_cache, v_cache, q, k, v):
    bsz, s = q.shape[:2]
    cw = k_cache.shape[1]
    k_src = jnp.concatenate([k_cache, k], 1)
    v_src = jnp.concatenate([v_cache, v], 1)
    pos = PAST_LEN + jnp.arange(s)
    out = dilated_attention(q, k_src, v_src, pos, PAST_LEN - cw)
    return out.astype(q.dtype).reshape(bsz, s, C_WIDTH)


def conv_ffn(x, buf, w_up, conv_w, conv_b, w_down):
    s = x.shape[1]
    h = x @ w_up
    hp = jnp.concatenate([buf.astype(h.dtype), h], 1)
    hc = conv_b + sum(hp[:, j:j + s] * conv_w[j] for j in range(CONV_W))
    g, u = hc[..., :D_FF], hc[..., D_FF:]
    return (jax.nn.gelu(g) * u) @ w_down, hp[:, -(CONV_W - 1):]


def trunk_layer(x, pos, attend, h0_re, h0_im, conv_buf, n1, w_in, w_s, b_s, lam_re, lam_im, log_dt,
                b_re, b_im, c_re, c_im, d_skip, w_glu, b_glu, w_out, n2, w_up, conv_w, conv_b, w_down):
    bsz, s, _ = x.shape
    proj = rmsnorm(x, n1) @ w_in
    o0 = 2 * A_WIDTH
    o1 = o0 + B_WIDTH
    o2 = o1 + C_WIDTH
    o3 = o2 + C_WIDTH
    out_a, v_rows = sgu_mixer(proj[..., :o0], w_s, b_s)
    out_b, h_re, h_im = s5_mixer(proj[..., o0:o1], h0_re, h0_im, lam_re, lam_im, log_dt,
                                 b_re, b_im, c_re, c_im, d_skip, w_glu, b_glu)
    q = rope_partial(proj[..., o1:o2].reshape(bsz, s, C_HEADS, HEAD_DIM), pos)
    k = rope_partial(proj[..., o2:o3].reshape(bsz, s, C_HEADS, HEAD_DIM), pos)
    v = proj[..., o3:].reshape(bsz, s, C_HEADS, HEAD_DIM)
    out_c = attend(q, k, v)
    x = x + jnp.concatenate([out_a, out_b, out_c], -1) @ w_out
    f, conv_new = conv_ffn(rmsnorm(x, n2), conv_buf, w_up, conv_w, conv_b, w_down)
    return x + f, k, v, v_rows, h_re, h_im, conv_new


def setup_inputs(seed: int = 0) -> dict:
    key = jax.random.key(seed)
    ks = jax.random.split(key, 32)
    f32 = jnp.float32

    def nrm(k, shape, scale):
        return scale * jax.random.normal(k, shape, f32)

    cw = min(MAX_WINDOW, PAST_LEN)
    G, P = B_GROUPS, SSM_STATE
    return {
        'x_prompt': nrm(ks[0], (BATCH, SEQ, D_MODEL), 1.0),
        'x_sample': nrm(ks[1], (DEC_BATCH, DEC_SEQ, D_MODEL), 1.0),
        'cache_c_k': nrm(ks[2], (DEPTH, DEC_BATCH, cw, C_HEADS, HEAD_DIM), 1.0),
        'cache_c_v': nrm(ks[3], (DEPTH, DEC_BATCH, cw, C_HEADS, HEAD_DIM), 1.0),
        'state_ssm_re': nrm(ks[4], (DEPTH, DEC_BATCH, G, P), 0.1),
        'state_ssm_im': nrm(ks[5], (DEPTH, DEC_BATCH, G, P), 0.1),
        'state_ffn_conv': nrm(ks[6], (DEPTH, DEC_BATCH, CONV_W - 1, 2 * D_FF), 1.0),
        'norm1_g': 1.0 + nrm(ks[7], (DEPTH, D_MODEL), 0.02),
        'w_in': nrm(ks[8], (DEPTH, D_MODEL, IN_WIDTH), D_MODEL ** -0.5),
        'w_s': nrm(ks[9], (DEPTH, A_HEADS, CHUNK, CHUNK), CHUNK ** -0.5),
        'b_s': 1.0 + nrm(ks[10], (DEPTH, A_HEADS, CHUNK), 0.1),
        'ssm_lam_re': -0.5 + nrm(ks[11], (DEPTH, G, P), 0.01),
        'ssm_lam_im': jnp.pi * jnp.arange(P, dtype=f32) + nrm(ks[12], (DEPTH, G, P), 0.01),
        'ssm_log_dt': jax.random.uniform(ks[13], (DEPTH, G), f32, math.log(1e-3), math.log(1e-1)),
        'ssm_b_re': nrm(ks[14], (DEPTH, G, P, SSM_GROUP), (2 * SSM_GROUP) ** -0.5),
        'ssm_b_im': nrm(ks[15], (DEPTH, G, P, SSM_GROUP), (2 * SSM_GROUP) ** -0.5),
        'ssm_c_re': nrm(ks[16], (DEPTH, G, SSM_GROUP, P), SSM_STATE ** -0.5),
        'ssm_c_im': nrm(ks[17], (DEPTH, G, SSM_GROUP, P), SSM_STATE ** -0.5),
        'ssm_d': nrm(ks[18], (DEPTH, B_WIDTH), 1.0),
        'w_glu': nrm(ks[19], (DEPTH, B_WIDTH, B_WIDTH), B_WIDTH ** -0.5),
        'b_glu': nrm(ks[20], (DEPTH, B_WIDTH), 0.01),
        'w_out': nrm(ks[21], (DEPTH, MIX_WIDTH, D_MODEL), MIX_WIDTH ** -0.5),
        'norm2_g': 1.0 + nrm(ks[22], (DEPTH, D_MODEL), 0.02),
        'w_up': nrm(ks[23], (DEPTH, D_MODEL, 2 * D_FF), D_MODEL ** -0.5),
        'conv_w': nrm(ks[24], (DEPTH, CONV_W, 2 * D_FF), CONV_W ** -0.5),
        'conv_b': nrm(ks[25], (DEPTH, 2 * D_FF), 0.01),
        'w_down': nrm(ks[26], (DEPTH, D_FF, D_MODEL), D_FF ** -0.5),
        'final_g': 1.0 + nrm(ks[27], (D_MODEL,), 0.02),
    }


def reference(x_prompt, x_sample, cache_c_k, cache_c_v, state_ssm_re, state_ssm_im, state_ffn_conv,
              norm1_g, w_in, w_s, b_s, ssm_lam_re, ssm_lam_im, ssm_log_dt, ssm_b_re, ssm_b_im,
              ssm_c_re, ssm_c_im, ssm_d, w_glu, b_glu, w_out, norm2_g, w_up, conv_w, conv_b, w_down,
              final_g):
    bp, sp = x_prompt.shape[:2]
    pos_p = jnp.arange(sp)
    pos_s = PAST_LEN + jnp.arange(x_sample.shape[1])
    cw_p = min(MAX_WINDOW, sp)
    hp, hs = x_prompt, x_sample
    kp_l, vp_l, ks_l, vs_l = [], [], [], []
    srp_l, sip_l, srs_l, sis_l = [], [], [], []
    cvp_l, cvs_l, vsg_l = [], [], []
    for l in range(DEPTH):
        lw = (norm1_g[l], w_in[l], w_s[l], b_s[l], ssm_lam_re[l], ssm_lam_im[l], ssm_log_dt[l],
              ssm_b_re[l], ssm_b_im[l], ssm_c_re[l], ssm_c_im[l], ssm_d[l], w_glu[l], b_glu[l],
              w_out[l], norm2_g[l], w_up[l], conv_w[l], conv_b[l], w_down[l])
        z = jnp.zeros((bp, B_GROUPS, SSM_STATE), jnp.float32)
        zc = jnp.zeros((bp, CONV_W - 1, 2 * D_FF), hp.dtype)
        hp, kp, vp, _, hrp, hip, cnp = trunk_layer(hp, pos_p, attend_prompt, z, z, zc, *lw)
        kp_l.append(kp[:, sp - cw_p:])
        vp_l.append(vp[:, sp - cw_p:])
        srp_l.append(hrp)
        sip_l.append(hip)
        cvp_l.append(cnp)
        att_s = functools.partial(attend_sample, cache_c_k[l], cache_c_v[l])
        hs, kn, vn, vsg, hrs, his, cns = trunk_layer(hs, pos_s, att_s, state_ssm_re[l], state_ssm_im[l],
                                                      state_ffn_conv[l], *lw)
        ks_l.append(kn)
        vs_l.append(vn)
        srs_l.append(hrs)
        sis_l.append(his)
        cvs_l.append(cns)
        vsg_l.append(vsg)
    y_prompt = rmsnorm(hp, final_g)
    y_sample = rmsnorm(hs, final_g)
    return (y_prompt, y_sample,
            jnp.stack(kp_l), jnp.stack(vp_l), jnp.stack(ks_l), jnp.stack(vs_l),
            jnp.stack(srp_l), jnp.stack(sip_l), jnp.stack(srs_l), jnp.stack(sis_l),
            jnp.stack(cvp_l), jnp.stack(cvs_l), jnp.stack(vsg_l))
```

```python
import functools
import math

import jax
import jax.numpy as jnp
from jax import lax
from jax.experimental import pallas as pl
from jax.experimental.pallas import tpu as pltpu

F32 = jnp.float32
BF16 = jnp.bfloat16

D_MODEL = 1024
DEPTH = 4
PAST_LEN = 8192
HEAD_DIM = 64
A_WIDTH = 256
B_WIDTH = 384
C_WIDTH = 384
A_HEADS = 4
C_HEADS = 6
CHUNK = 128
SSM_GROUP = 16
B_GROUPS = 24
SSM_STATE = 64
DILATIONS = ((128, 1), (512, 4), (2048, 16))
MAX_WINDOW = 2048
ROPE_THETA = 500000.0
ROT_DIM = 16
D_FF = 2816
EPS = 1e-6
NEG_INF = -1e30

O_A, O_B, O_Q, O_K, O_V = 0, 2 * A_WIDTH, 896, 1280, 1664

LANES = 128
SUBLANES = 8
S5_BLOCKS = B_WIDTH // LANES
S5_HALF = 512
FF_CHUNK = 256
N_FF = D_FF // FF_CHUNK
VMEM_LIMIT = 56 * 1024 * 1024


def _dot(a, b):
    return jnp.dot(a.astype(BF16), b.astype(BF16), preferred_element_type=F32)


def _rms(x, g):
    return x * lax.rsqrt(jnp.mean(x * x, -1, keepdims=True) + EPS) * g


def _params(*sem):
    return pltpu.CompilerParams(dimension_semantics=sem, vmem_limit_bytes=VMEM_LIMIT)


def _rope_tables_kernel(c_ref, a_ref, b_ref, *, pos0):
    n = c_ref.shape[0]
    pos = (lax.broadcasted_iota(jnp.int32, (n, LANES), 0) + (pos0 + pl.program_id(0) * n)).astype(F32)
    d = lax.broadcasted_iota(jnp.int32, (n, LANES), 1) & (HEAD_DIM - 1)
    k = (d & (ROT_DIM // 2 - 1)).astype(F32)
    inv = jnp.exp(k * (-2.0 / ROT_DIM * math.log(ROPE_THETA)))
    ang = pos * inv
    cos, sin = jnp.cos(ang), jnp.sin(ang)
    c_ref[...] = jnp.where(d < ROT_DIM, cos, 1.0)
    a_ref[...] = jnp.where(d < ROT_DIM // 2, -sin, 0.0)
    b_ref[...] = jnp.where((d >= ROT_DIM // 2) & (d < ROT_DIM), sin, 0.0)


def rope_tables(n, pos0):
    sds = jax.ShapeDtypeStruct((n, LANES), F32)
    tr = min(n, 1024)
    spec = pl.BlockSpec((tr, LANES), lambda i: (i, 0))
    return pl.pallas_call(functools.partial(_rope_tables_kernel, pos0=pos0), grid=(n // tr,),
                          out_specs=(spec, spec, spec), out_shape=(sds, sds, sds),
                          compiler_params=_params("arbitrary"), name="rope_tables")()


def _s5_params_kernel(lr_ref, li_ref, ldt_ref, br_ref, bi_ref, cim_ref,
                      bbr_ref, bbi_ref, pr_ref, pi_ref, ncim_ref, *, n_pow):
    lr, li = lr_ref[...], li_ref[...]
    dt = jnp.exp(ldt_ref[...])
    mag = jnp.exp(lr * dt)
    ar, ai = mag * jnp.cos(li * dt), mag * jnp.sin(li * dt)
    nr, ni = ar - 1.0, ai
    den = lr * lr + li * li
    fr, fi = (nr * lr + ni * li) / den, (ni * lr - nr * li) / den
    br, bi = br_ref[...], bi_ref[...]
    bbr_ref[...] = fr[None] * br - fi[None] * bi
    bbi_ref[...] = fr[None] * bi + fi[None] * br
    k = (lax.broadcasted_iota(jnp.int32, (n_pow,) + lr.shape, 0) + 1).astype(F32)
    magk = jnp.exp(k * (lr * dt)[None])
    angk = k * (li * dt)[None]
    pr_ref[...] = magk * jnp.cos(angk)
    pi_ref[...] = magk * jnp.sin(angk)
    ncim_ref[...] = -cim_ref[...]


def s5_params(lam_re, lam_im, log_dt, b_re, b_im, c_im, n_pow):
    depth, g, p = lam_re.shape
    c = b_re.shape[-1]
    ldt = jnp.broadcast_to(log_dt[:, :, None], (depth, g, p))
    brt = jnp.transpose(b_re, (0, 3, 1, 2))
    bit = jnp.transpose(b_im, (0, 3, 1, 2))
    gp = pl.BlockSpec((None, g, p), lambda l: (l, 0, 0))
    cgp = pl.BlockSpec((None, c, g, p), lambda l: (l, 0, 0, 0))
    gcp = pl.BlockSpec((None, g, c, p), lambda l: (l, 0, 0, 0))
    kgp = pl.BlockSpec((None, n_pow, g, p), lambda l: (l, 0, 0, 0))
    return pl.pallas_call(
        functools.partial(_s5_params_kernel, n_pow=n_pow),
        grid=(depth,),
        in_specs=[gp, gp, gp, cgp, cgp, gcp],
        out_specs=(cgp, cgp, kgp, kgp, gcp),
        out_shape=(jax.ShapeDtypeStruct((depth, c, g, p), F32), jax.ShapeDtypeStruct((depth, c, g, p), F32),
                   jax.ShapeDtypeStruct((depth, n_pow, g, p), F32), jax.ShapeDtypeStruct((depth, n_pow, g, p), F32),
                   jax.ShapeDtypeStruct((depth, g, c, p), F32)),
        compiler_params=_params("arbitrary"),
        name="s5_params",
    )(lam_re, lam_im, ldt, brt, bit, c_im)


def _s5_lane_vec(t):
    return t.reshape(t.shape[:-2] + (S5_BLOCKS, 1, S5_HALF))


def s5_layouts(bbr, bbi, c_re, ncim, pr, pi):
    eye = jnp.eye(SUBLANES, dtype=F32)
    bb = jnp.stack([bbr, bbi], 0).reshape(2, SSM_GROUP, S5_BLOCKS, 8, SSM_STATE)
    wb = jnp.einsum('ecjgp,gh->jgcehp', bb, eye).reshape(S5_BLOCKS, LANES, 2 * S5_HALF)
    cc = jnp.stack([c_re, ncim], 0).reshape(2, S5_BLOCKS, 8, SSM_GROUP, SSM_STATE)
    wc = jnp.einsum('ejgcp,gh->jeghpc', cc, eye)
    wc = jnp.transpose(wc, (0, 1, 2, 4, 3, 5)).reshape(S5_BLOCKS, 2 * S5_HALF, LANES)
    n_pow = pr.shape[0]
    prl = jnp.transpose(pr.reshape(n_pow, S5_BLOCKS, S5_HALF), (1, 0, 2))
    pil = jnp.transpose(pi.reshape(n_pow, S5_BLOCKS, S5_HALF), (1, 0, 2))
    return wb.astype(BF16), wc.astype(BF16), prl, pil


def _rope_apply(x, c, a, b):
    return x * c + pltpu.roll(x, LANES - ROT_DIM // 2, 1) * a + pltpu.roll(x, ROT_DIM // 2, 1) * b


def _inproj_kernel(x_ref, g_ref, w_ref, ws_ref, bs_ref, rc_ref, ra_ref, rb_ref,
                   oa_ref, vn_ref, ub_ref, q_ref, k_ref, v_ref, kf_ref, vf_ref, *, seq_rows):
    tm = x_ref.shape[0]
    xn = _rms(x_ref[...], g_ref[...]).astype(BF16)

    h = jax.nn.gelu(jnp.dot(xn, w_ref[:, O_A:O_B], preferred_element_type=F32))
    u, v = h[:, :A_WIDTH], h[:, A_WIDTH:]
    mu = jnp.mean(v, -1, keepdims=True)
    var = jnp.mean(jnp.square(v - mu), -1, keepdims=True)
    vn = (v - mu) * lax.rsqrt(var + EPS)
    vn_ref[...] = vn
    ri = lax.broadcasted_iota(jnp.int32, (CHUNK, CHUNK), 0)
    ci = lax.broadcasted_iota(jnp.int32, (CHUNK, CHUNK), 1)
    keep = (ci <= ri) & ((ri // seq_rows) == (ci // seq_rows))
    lane = lax.broadcasted_iota(jnp.int32, (CHUNK, LANES), 1)
    wm = [jnp.where(keep, ws_ref[hh], 0.0).astype(BF16) for hh in range(A_HEADS)]
    vnb = vn.astype(BF16)
    for c in range(tm // CHUNK):
        rows = slice(c * CHUNK, (c + 1) * CHUNK)
        for p in range(A_WIDTH // LANES):
            cols = slice(p * LANES, (p + 1) * LANES)
            vp = vnb[rows, cols]
            m0 = jnp.dot(wm[2 * p], vp, preferred_element_type=F32)
            m1 = jnp.dot(wm[2 * p + 1], vp, preferred_element_type=F32)
            mixed = jnp.where(lane < HEAD_DIM, m0, m1) + bs_ref[:, cols]
            oa_ref[rows, cols] = u[rows, cols] * mixed

    ub_ref[...] = jnp.dot(xn, w_ref[:, O_B:O_Q], preferred_element_type=F32)

    rc, ra, rb = rc_ref[...], ra_ref[...], rb_ref[...]
    for p in range(C_WIDTH // LANES):
        cols = slice(p * LANES, (p + 1) * LANES)
        q = jnp.dot(xn, w_ref[:, O_Q + p * LANES:O_Q + (p + 1) * LANES], preferred_element_type=F32)
        q_ref[:, cols] = (_rope_apply(q, rc, ra, rb) * (HEAD_DIM ** -0.5)).astype(BF16)
        k = _rope_apply(jnp.dot(xn, w_ref[:, O_K + p * LANES:O_K + (p + 1) * LANES], preferred_element_type=F32),
                        rc, ra, rb)
        k_ref[:, cols] = k.astype(BF16)
        kf_ref[:, cols] = k
        vv = jnp.dot(xn, w_ref[:, O_V + p * LANES:O_V + (p + 1) * LANES], preferred_element_type=F32)
        v_ref[:, cols] = vv.astype(BF16)
        vf_ref[:, cols] = vv


def in_projection(x, g1, w_in, w_s, bs_rows, rope, *, tm, seq_rows, tail_rows):
    bsz, s, _ = x.shape
    nt = s // tm
    tail_t = tail_rows // tm
    row = lambda w: pl.BlockSpec((None, tm, w), lambda b, t: (b, t, 0))
    tail = pl.BlockSpec((None, tm, C_WIDTH), lambda b, t: (b, jnp.maximum(t - (nt - tail_t), 0), 0))
    full = lambda shp: pl.BlockSpec(shp, lambda b, t: (0,) * len(shp))
    ropespec = pl.BlockSpec((tm, LANES), lambda b, t: (t, 0))
    sd = lambda w, dt: jax.ShapeDtypeStruct((bsz, s, w), dt)
    return pl.pallas_call(
        functools.partial(_inproj_kernel, seq_rows=seq_rows),
        grid=(bsz, nt),
        in_specs=[row(D_MODEL), full((1, D_MODEL)), full(w_in.shape), full(w_s.shape), full(bs_rows.shape),
                  ropespec, ropespec, ropespec],
        out_specs=(row(A_WIDTH), row(A_WIDTH), row(B_WIDTH), row(C_WIDTH), row(C_WIDTH), row(C_WIDTH), tail, tail),
        out_shape=(sd(A_WIDTH, F32), sd(A_WIDTH, F32), sd(B_WIDTH, F32), sd(C_WIDTH, BF16), sd(C_WIDTH, BF16),
                   sd(C_WIDTH, BF16), jax.ShapeDtypeStruct((bsz, tail_rows, C_WIDTH), F32),
                   jax.ShapeDtypeStruct((bsz, tail_rows, C_WIDTH), F32)),
        compiler_params=_params("arbitrary", "arbitrary"),
        name="in_projection",
    )(x, g1, w_in, w_s, bs_rows, *rope)


def _s5_readout(hb, uperm, gs, wc_ref, d_ref, wg_ref, bg_ref):
    for j in range(S5_BLOCKS):
        cols = slice(j * LANES, (j + 1) * LANES)
        y = jnp.dot(hb[j], wc_ref[j], preferred_element_type=F32) + d_ref[:, cols] * uperm[:, cols]
        gs[:, cols] = jax.nn.gelu(y)
    g = gs[...]
    gs[...] = g * jax.nn.sigmoid(_dot(g, wg_ref[...]) + bg_ref[...])


def _s5_prompt_kernel(u_ref, wb_ref, wc_ref, pr_ref, pi_ref, d_ref, wg_ref, bg_ref,
                      o_ref, hr_ref, hi_ref,
                      upad, uperm, xs, hb, cs, carry, gs, *, seglen):
    pitch = seglen + SUBLANES

    @pl.when(pl.program_id(1) == 0)
    def _():
        carry[...] = jnp.zeros_like(carry)

    for s in range(SUBLANES):
        for p in range(S5_BLOCKS):
            upad[p, s * pitch:s * pitch + seglen, :] = u_ref[s * seglen:(s + 1) * seglen, p * LANES:(p + 1) * LANES]

    def perm(i, _):
        r = pl.multiple_of(i * SUBLANES, SUBLANES)
        for p in range(S5_BLOCKS):
            uperm[pl.ds(r, SUBLANES), p * LANES:(p + 1) * LANES] = upad[p, pl.ds(i, SUBLANES, stride=pitch), :]
        return 0
    lax.fori_loop(0, seglen, perm, 0)

    for j in range(S5_BLOCKS):
        xs[j] = _dot(uperm[:, j * LANES:(j + 1) * LANES], wb_ref[j])

    for j in range(S5_BLOCKS):
        ar = jnp.broadcast_to(pr_ref[j, 0:1, :], (SUBLANES, S5_HALF))
        ai = jnp.broadcast_to(pi_ref[j, 0:1, :], (SUBLANES, S5_HALF))

        def scan(i, hc, j=j, ar=ar, ai=ai):
            hr, hi = hc
            r = pl.multiple_of(i * SUBLANES, SUBLANES)
            nhr = ar * hr - ai * hi + xs[j, pl.ds(r, SUBLANES), 0:S5_HALF]
            nhi = ar * hi + ai * hr + xs[j, pl.ds(r, SUBLANES), S5_HALF:2 * S5_HALF]
            xs[j, pl.ds(r, SUBLANES), 0:S5_HALF] = nhr
            xs[j, pl.ds(r, SUBLANES), S5_HALF:2 * S5_HALF] = nhi
            return nhr, nhi
        z = jnp.zeros((SUBLANES, S5_HALF), F32)
        er, ei = lax.fori_loop(0, seglen, scan, (z, z))

        a64r, a64i = pr_ref[j, seglen - 1:seglen, :], pi_ref[j, seglen - 1:seglen, :]
        cr, ci = carry[j, 0, 0:1, :], carry[j, 1, 0:1, :]
        for s in range(SUBLANES):
            cs[j, 0, s:s + 1, :] = cr
            cs[j, 1, s:s + 1, :] = ci
            cr, ci = (a64r * cr - a64i * ci + er[s:s + 1, :], a64r * ci + a64i * cr + ei[s:s + 1, :])
        carry[j, 0, 0:1, :] = cr
        carry[j, 1, 0:1, :] = ci
        hr_ref[j] = cr
        hi_ref[j] = ci

        cr8, ci8 = cs[j, 0], cs[j, 1]

        def fix(i2, _, j=j, cr8=cr8, ci8=ci8):
            r = pl.multiple_of(i2 * 2 * SUBLANES, 2 * SUBLANES)
            hrs, his = [], []
            for t in range(2):
                i = i2 * 2 + t
                pr, pi = pr_ref[j, pl.ds(i, 1), :], pi_ref[j, pl.ds(i, 1), :]
                rows = pl.ds(r + t * SUBLANES, SUBLANES)
                hrs.append(xs[j, rows, 0:S5_HALF] + pr * cr8 - pi * ci8)
                his.append(xs[j, rows, S5_HALF:2 * S5_HALF] + pr * ci8 + pi * cr8)
            hb[j, pl.ds(r, 2 * SUBLANES), 0:S5_HALF] = jnp.concatenate(hrs, 0).astype(BF16)
            hb[j, pl.ds(r, 2 * SUBLANES), S5_HALF:2 * S5_HALF] = jnp.concatenate(his, 0).astype(BF16)
            return 0
        lax.fori_loop(0, seglen // 2, fix, 0)

    _s5_readout(hb, uperm, gs, wc_ref, d_ref, wg_ref, bg_ref)

    def unperm(i, _):
        r = pl.multiple_of(i * SUBLANES, SUBLANES)
        for p in range(S5_BLOCKS):
            upad[p, pl.ds(i, SUBLANES, stride=pitch), :] = gs[pl.ds(r, SUBLANES), p * LANES:(p + 1) * LANES]
        return 0
    lax.fori_loop(0, seglen, unperm, 0)
    for s in range(SUBLANES):
        for p in range(S5_BLOCKS):
            o_ref[s * seglen:(s + 1) * seglen, p * LANES:(p + 1) * LANES] = upad[p, s * pitch:s * pitch + seglen, :]


def s5_prompt(u, wb, wc, prl, pil, d_skip, w_glu, b_glu, *, seglen):
    bsz, s, _ = u.shape
    t = SUBLANES * seglen
    pitch = seglen + SUBLANES
    full = lambda shp: pl.BlockSpec(shp, lambda b, c: (0,) * len(shp))
    st = pl.BlockSpec((None, S5_BLOCKS, 1, S5_HALF), lambda b, c: (b, 0, 0, 0))
    rows = pl.BlockSpec((None, t, B_WIDTH), lambda b, c: (b, c, 0))
    return pl.pallas_call(
        functools.partial(_s5_prompt_kernel, seglen=seglen),
        grid=(bsz, s // t),
        in_specs=[rows, full(wb.shape), full(wc.shape), full(prl.shape), full(pil.shape),
                  full((1, B_WIDTH)), full(w_glu.shape), full((1, B_WIDTH))],
        out_specs=(rows, st, st),
        out_shape=(jax.ShapeDtypeStruct((bsz, s, B_WIDTH), F32),
                   jax.ShapeDtypeStruct((bsz, S5_BLOCKS, 1, S5_HALF), F32),
                   jax.ShapeDtypeStruct((bsz, S5_BLOCKS, 1, S5_HALF), F32)),
        scratch_shapes=[pltpu.VMEM((S5_BLOCKS, SUBLANES * pitch, LANES), F32),
                        pltpu.VMEM((t, B_WIDTH), F32),
                        pltpu.VMEM((S5_BLOCKS, t, 2 * S5_HALF), F32),
                        pltpu.VMEM((S5_BLOCKS, t, 2 * S5_HALF), BF16),
                        pltpu.VMEM((S5_BLOCKS, 2, SUBLANES, S5_HALF), F32),
                        pltpu.VMEM((S5_BLOCKS, 2, SUBLANES, S5_HALF), F32),
                        pltpu.VMEM((t, B_WIDTH), F32)],
        compiler_params=_params("arbitrary", "arbitrary"),
        name="s5_prompt",
    )(u, wb, wc, prl, pil, d_skip, w_glu, b_glu)


def _s5_sample_kernel(u_ref, h0r_ref, h0i_ref, wb_ref, wc_ref, pr_ref, pi_ref, d_ref, wg_ref, bg_ref,
                      o_ref, hr_ref, hi_ref,
                      upad, uperm, xs, hb, gs, *, nseq, steps):
    nblk = nseq // SUBLANES
    for p in range(S5_BLOCKS):
        upad[p] = u_ref[:, p * LANES:(p + 1) * LANES]
    for st in range(steps):
        for bb in range(nblk):
            r = (st * nblk + bb) * SUBLANES
            for p in range(S5_BLOCKS):
                uperm[r:r + SUBLANES, p * LANES:(p + 1) * LANES] = (
                    upad[p, pl.ds(bb * SUBLANES * steps + st, SUBLANES, stride=steps), :])
    for j in range(S5_BLOCKS):
        xs[j] = _dot(uperm[:, j * LANES:(j + 1) * LANES], wb_ref[j])
    for j in range(S5_BLOCKS):
        ar = jnp.broadcast_to(pr_ref[j, 0:1, :], (SUBLANES, S5_HALF))
        ai = jnp.broadcast_to(pi_ref[j, 0:1, :], (SUBLANES, S5_HALF))
        for bb in range(nblk):
            seqs = slice(bb * SUBLANES, (bb + 1) * SUBLANES)
            hr, hi = h0r_ref[j, seqs, :], h0i_ref[j, seqs, :]
            for st in range(steps):
                r = (st * nblk + bb) * SUBLANES
                hr, hi = (ar * hr - ai * hi + xs[j, r:r + SUBLANES, 0:S5_HALF],
                          ar * hi + ai * hr + xs[j, r:r + SUBLANES, S5_HALF:2 * S5_HALF])
                xs[j, r:r + SUBLANES, 0:S5_HALF] = hr
                xs[j, r:r + SUBLANES, S5_HALF:2 * S5_HALF] = hi
            hr_ref[j, seqs, :] = hr
            hi_ref[j, seqs, :] = hi
        hb[j] = xs[j].astype(BF16)
    _s5_readout(hb, uperm, gs, wc_ref, d_ref, wg_ref, bg_ref)
    for st in range(steps):
        for bb in range(nblk):
            r = (st * nblk + bb) * SUBLANES
            for p in range(S5_BLOCKS):
                upad[p, pl.ds(bb * SUBLANES * steps + st, SUBLANES, stride=steps), :] = (
                    gs[r:r + SUBLANES, p * LANES:(p + 1) * LANES])
    for p in range(S5_BLOCKS):
        o_ref[:, p * LANES:(p + 1) * LANES] = upad[p]


def s5_sample(u, h0r, h0i, wb, wc, prl, pil, d_skip, w_glu, b_glu, *, nseq, steps):
    n = nseq * steps
    st_sd = jax.ShapeDtypeStruct((S5_BLOCKS, nseq, S5_HALF), F32)
    return pl.pallas_call(
        functools.partial(_s5_sample_kernel, nseq=nseq, steps=steps),
        out_shape=(jax.ShapeDtypeStruct((n, B_WIDTH), F32), st_sd, st_sd),
        scratch_shapes=[pltpu.VMEM((S5_BLOCKS, n, LANES), F32),
                        pltpu.VMEM((n, B_WIDTH), F32),
                        pltpu.VMEM((S5_BLOCKS, n, 2 * S5_HALF), F32),
                        pltpu.VMEM((S5_BLOCKS, n, 2 * S5_HALF), BF16),
                        pltpu.VMEM((n, B_WIDTH), F32)],
        compiler_params=pltpu.CompilerParams(vmem_limit_bytes=VMEM_LIMIT),
        name="s5_sample",
    )(u, h0r, h0i, wb, wc, prl, pil, d_skip, w_glu, b_glu)


def _attn_prompt_kernel(q_ref, kp_ref, kc_ref, vp_ref, vc_ref, o_ref, st_ref, kx, vx):
    tq = q_ref.shape[0]
    t = pl.program_id(2)
    kx[0:CHUNK, :] = kp_ref[...]
    kx[CHUNK:CHUNK + tq, :] = kc_ref[...]
    vx[0:CHUNK, :] = vp_ref[...]
    vx[CHUNK:CHUNK + tq, :] = vc_ref[...]
    r = lax.broadcasted_iota(jnp.int32, (CHUNK, 2 * CHUNK), 0)
    c = lax.broadcasted_iota(jnp.int32, (CHUNK, 2 * CHUNK), 1)
    band = (c >= r) & (c <= r + CHUNK)
    lane = lax.broadcasted_iota(jnp.int32, (CHUNK, LANES), 1)
    lo_half = lane < HEAD_DIM
    for u in range(tq // CHUNK):
        rows = slice(u * CHUNK, (u + 1) * CHUNK)
        keys = slice(u * CHUNK, (u + 2) * CHUNK)
        if u == 0:
            valid = band & (c >= jnp.where(t == 0, CHUNK, 0))
        else:
            valid = band
        bias = jnp.where(valid, 0.0, NEG_INF)
        st = jnp.zeros((CHUNK, LANES), F32)
        for p in range(C_WIDTH // LANES):
            cols = slice(p * LANES, (p + 1) * LANES)
            qp = q_ref[rows, cols]
            kpair = kx[keys, cols]
            vpair = vx[keys, cols]
            outs = []
            for hh in range(2):
                qm = jnp.where(lo_half if hh == 0 else ~lo_half, qp, jnp.zeros_like(qp))
                sc = lax.dot_general(qm, kpair, (((1,), (1,)), ((), ())), preferred_element_type=F32) + bias
                m = jnp.max(sc, -1, keepdims=True)
                pe = jnp.exp(sc - m)
                l = jnp.sum(pe, -1, keepdims=True)
                outs.append(jnp.dot(pe.astype(BF16), vpair, preferred_element_type=F32) / l)
                st = jnp.where(lane == 2 * p + hh, m + jnp.log(l), st)
            o_ref[rows, cols] = jnp.where(lo_half, outs[0], outs[1])
        st_ref[rows, :] = st


def attn_prompt(q, k, v, dil, *, tq):
    bsz, s, _ = q.shape
    ln = s // dil
    tq = min(tq, ln)
    view = lambda a: a.reshape(bsz, ln, dil * C_WIDTH)
    cur = pl.BlockSpec((None, tq, C_WIDTH), lambda b, r, t: (b, t, r))
    prev = pl.BlockSpec((None, CHUNK, C_WIDTH), lambda b, r, t: (b, jnp.maximum(t * (tq // CHUNK) - 1, 0), r))
    o, st = pl.pallas_call(
        _attn_prompt_kernel,
        grid=(bsz, dil, ln // tq),
        in_specs=[cur, prev, cur, prev, cur],
        out_specs=(cur, pl.BlockSpec((None, tq, LANES), lambda b, r, t: (b, t, r))),
        out_shape=(jax.ShapeDtypeStruct((bsz, ln, dil * C_WIDTH), F32),
                   jax.ShapeDtypeStruct((bsz, ln, dil * LANES), F32)),
        scratch_shapes=[pltpu.VMEM((CHUNK + tq, C_WIDTH), BF16), pltpu.VMEM((CHUNK + tq, C_WIDTH), BF16)],
        compiler_params=_params("arbitrary", "arbitrary", "arbitrary"),
        name=f"attn_prompt_d{dil}",
    )(view(q), view(k), view(k), view(v), view(v))
    return o.reshape(bsz, s, C_WIDTH), st.reshape(bsz, s, LANES)


def _attn_sample_kernel(q_ref, kn_ref, vn_ref, ck_ref, cv_ref, o_ref, kx, vx, *, steps):
    cw = ck_ref.shape[0]
    nrow = C_HEADS * steps
    zpad = jnp.zeros((CHUNK - steps, C_WIDTH), F32)
    kx[0:cw, :] = ck_ref[...].astype(BF16)
    kx[cw:cw + CHUNK, :] = jnp.concatenate([kn_ref[...], zpad], 0).astype(BF16)
    vx[0:cw, :] = cv_ref[...].astype(BF16)
    vx[cw:cw + CHUNK, :] = jnp.concatenate([vn_ref[...], zpad], 0).astype(BF16)

    rr = lax.broadcasted_iota(jnp.int32, (nrow, C_WIDTH), 0)
    ll = lax.broadcasted_iota(jnp.int32, (nrow, C_WIDTH), 1)
    qx = jnp.concatenate([q_ref[...]] * C_HEADS, 0)
    qx = jnp.where(rr // steps == ll // HEAD_DIM, qx, 0.0).astype(BF16)
    sc = lax.dot_general(qx, kx[...], (((1,), (1,)), ((), ())), preferred_element_type=F32)

    row = lax.broadcasted_iota(jnp.int32, sc.shape, 0)
    idx = lax.broadcasted_iota(jnp.int32, sc.shape, 1)
    delta = cw + row % steps - idx
    cnt = jnp.zeros(sc.shape, F32)
    for window, dil in DILATIONS:
        cnt = cnt + jnp.where((delta >= 0) & (delta <= window) & (delta % dil == 0), 1.0, 0.0)
    live = cnt > 0
    m = jnp.max(jnp.where(live, sc, NEG_INF), -1, keepdims=True)
    e = cnt * jnp.exp(jnp.where(live, sc - m, NEG_INF))
    l = jnp.sum(e, -1, keepdims=True)
    acc = jnp.dot(e.astype(BF16), vx[...], preferred_element_type=F32) / l
    lane = lax.broadcasted_iota(jnp.int32, (steps, C_WIDTH), 1)
    out = jnp.zeros((steps, C_WIDTH), F32)
    for hh in range(C_HEADS):
        out = jnp.where(lane // HEAD_DIM == hh, acc[hh * steps:(hh + 1) * steps, :], out)
    o_ref[...] = out


def attn_sample(q, k_new, v_new, cache_k, cache_v, *, steps):
    nseq, cw, _ = cache_k.shape
    rows = pl.BlockSpec((steps, C_WIDTH), lambda b: (b, 0))
    cache = pl.BlockSpec((None, cw, C_WIDTH), lambda b: (b, 0, 0))
    return pl.pallas_call(
        functools.partial(_attn_sample_kernel, steps=steps),
        grid=(nseq,),
        in_specs=[rows, rows, rows, cache, cache],
        out_specs=rows,
        out_shape=jax.ShapeDtypeStruct((nseq * steps, C_WIDTH), F32),
        scratch_shapes=[pltpu.VMEM((cw + CHUNK, C_WIDTH), BF16), pltpu.VMEM((cw + CHUNK, C_WIDTH), BF16)],
        compiler_params=_params("arbitrary"),
        name="attn_sample",
    )(q, k_new, v_new, cache_k, cache_v)


def _outproj_kernel(*refs, nseq, seq_len, tiles_per_seq, n_branch, has_buf, final_norm):
    refs = list(refs)
    x_ref, oa_ref, ob_ref = refs[:3]
    o_refs = refs[3:3 + n_branch]
    st_refs = refs[3 + n_branch:3 + 2 * n_branch] if n_branch > 1 else []
    pos = 3 + (2 * n_branch if n_branch > 1 else 1)
    (e_ref, wo_ref, g2_ref, wug_ref, wuu_ref, cwg_ref, cwu_ref, cbg_ref, cbu_ref, wd_ref, fg_ref) = refs[pos:pos + 11]
    pos += 11
    if has_buf:
        bufg_ref, bufu_ref = refs[pos:pos + 2]
        pos += 2
    y_ref, cg_ref, cu_ref, xn_scr, hs_scr, carry_scr = refs[pos:pos + 6]

    i = pl.program_id(0)
    j = pl.program_id(1)
    tm = x_ref.shape[0]

    @pl.when(j == 0)
    def _():
        if n_branch > 1:
            sts = [r[...] for r in st_refs]
            mx = functools.reduce(jnp.maximum, sts)
            ws = [jnp.exp(s - mx) for s in sts]
            tot = functools.reduce(lambda a, b: a + b, ws)
            oc = jnp.zeros((tm, C_WIDTH), F32)
            for w, o_ref in zip(ws, o_refs):
                alpha = w / tot
                hi = alpha.astype(BF16)
                lo = (alpha - hi.astype(F32)).astype(BF16)
                wide = (jnp.dot(hi, e_ref[...], preferred_element_type=F32)
                        + jnp.dot(lo, e_ref[...], preferred_element_type=F32))
                oc = oc + wide * o_ref[...]
        else:
            oc = o_refs[0][...]
        mix = (_dot(oa_ref[...], wo_ref[0:A_WIDTH, :]) + _dot(ob_ref[...], wo_ref[A_WIDTH:A_WIDTH + B_WIDTH, :])
               + _dot(oc, wo_ref[A_WIDTH + B_WIDTH:, :]))
        x1 = x_ref[...] + mix
        y_ref[...] = x1
        xn_scr[...] = _rms(x1, g2_ref[...]).astype(BF16)

    xn = xn_scr[...]
    hg = jnp.dot(xn, wug_ref[...], preferred_element_type=F32)
    hu = jnp.dot(xn, wuu_ref[...], preferred_element_type=F32)
    lo, hi_ = SUBLANES, SUBLANES + seq_len
    hs_scr[:, lo:hi_, 0:FF_CHUNK] = hg.reshape(nseq, seq_len, FF_CHUNK)
    hs_scr[:, lo:hi_, FF_CHUNK:] = hu.reshape(nseq, seq_len, FF_CHUNK)
    if has_buf:
        hs_scr[:, lo - 2:lo, 0:FF_CHUNK] = bufg_ref[...]
        hs_scr[:, lo - 2:lo, FF_CHUNK:] = bufu_ref[...]
    else:
        first = (i % tiles_per_seq) == 0

        @pl.when(first)
        def _():
            hs_scr[0, 0:lo, :] = jnp.zeros((lo, 2 * FF_CHUNK), F32)

        @pl.when(jnp.logical_not(first))
        def _():
            hs_scr[0, 0:lo, :] = carry_scr[j]

        carry_scr[j] = hs_scr[0, hi_ - SUBLANES:hi_, :]
    cg_ref[...] = hs_scr[:, hi_ - SUBLANES:hi_, 0:FF_CHUNK]
    cu_ref[...] = hs_scr[:, hi_ - SUBLANES:hi_, FF_CHUNK:]

    cw = jnp.concatenate([cwg_ref[...], cwu_ref[...]], axis=1)
    cb = jnp.concatenate([cbg_ref[...], cbu_ref[...]], axis=1)
    hc = cb + (hs_scr[:, lo - 2:hi_ - 2, :] * cw[0:1] + hs_scr[:, lo - 1:hi_ - 1, :] * cw[1:2]
               + hs_scr[:, lo:hi_, :] * cw[2:3])
    hc = hc.reshape(tm, 2 * FF_CHUNK)
    act = jax.nn.gelu(hc[:, :FF_CHUNK]) * hc[:, FF_CHUNK:]
    y_ref[...] += _dot(act, wd_ref[...])

    if final_norm:
        @pl.when(j == pl.num_programs(1) - 1)
        def _():
            y_ref[...] = _rms(y_ref[...], fg_ref[...])


def out_projection(x, oa, ob, o_list, st_list, expand, w_out, g2, w_up, conv_w, conv_b, w_down, fg, bufs,
                   *, tm, nseq, seq_len, tiles_per_seq, final_norm):
    n = x.shape[0]
    nt = n // tm
    n_branch = len(o_list)
    has_buf = bufs is not None
    row = lambda w: pl.BlockSpec((tm, w), lambda i, j: (i, 0))
    full = lambda shp: pl.BlockSpec(shp, lambda i, j: (0,) * len(shp))
    in_specs = [row(D_MODEL), row(A_WIDTH), row(B_WIDTH)] + [row(C_WIDTH)] * n_branch
    args = [x, oa, ob] + list(o_list)
    if n_branch > 1:
        in_specs += [row(LANES)] * n_branch
        args += list(st_list)
    in_specs += [full(expand.shape), full(w_out.shape), full((1, D_MODEL)),
                 pl.BlockSpec((D_MODEL, FF_CHUNK), lambda i, j: (0, j)),
                 pl.BlockSpec((D_MODEL, FF_CHUNK), lambda i, j: (0, N_FF + j)),
                 pl.BlockSpec((3, FF_CHUNK), lambda i, j: (0, j)),
                 pl.BlockSpec((3, FF_CHUNK), lambda i, j: (0, N_FF + j)),
                 pl.BlockSpec((1, FF_CHUNK), lambda i, j: (0, j)),
                 pl.BlockSpec((1, FF_CHUNK), lambda i, j: (0, N_FF + j)),
                 pl.BlockSpec((FF_CHUNK, D_MODEL), lambda i, j: (j, 0)),
                 full((1, D_MODEL))]
    args += [expand, w_out, g2, w_up, w_up, conv_w, conv_w, conv_b, conv_b, w_down, fg]
    if has_buf:
        in_specs += [pl.BlockSpec((nseq, 2, FF_CHUNK), lambda i, j: (0, 0, j)),
                     pl.BlockSpec((nseq, 2, FF_CHUNK), lambda i, j: (0, 0, N_FF + j))]
        args += [bufs, bufs]
    nconv = nseq * nt
    conv_spec = pl.BlockSpec((nseq, SUBLANES, FF_CHUNK), lambda i, j: (i, 0, j))
    return pl.pallas_call(
        functools.partial(_outproj_kernel, nseq=nseq, seq_len=seq_len, tiles_per_seq=tiles_per_seq,
                          n_branch=n_branch, has_buf=has_buf, final_norm=final_norm),
        grid=(nt, N_FF),
        in_specs=in_specs,
        out_specs=(row(D_MODEL), conv_spec, conv_spec),
        out_shape=(jax.ShapeDtypeStruct((n, D_MODEL), F32),
                   jax.ShapeDtypeStruct((nconv, SUBLANES, D_FF), F32),
                   jax.ShapeDtypeStruct((nconv, SUBLANES, D_FF), F32)),
        scratch_shapes=[pltpu.VMEM((tm, D_MODEL), BF16),
                        pltpu.VMEM((nseq, SUBLANES + seq_len, 2 * FF_CHUNK), F32),
                        pltpu.VMEM((N_FF, SUBLANES, 2 * FF_CHUNK), F32)],
        compiler_params=_params("arbitrary", "arbitrary"),
        name="out_projection",
    )(*args)


PROMPT_TM = 512
S5_SEGLEN = 64
ATTN_TQ = 512


def kernel(x_prompt, x_sample, cache_c_k, cache_c_v, state_ssm_re, state_ssm_im, state_ffn_conv, norm1_g, w_in, w_s, b_s, ssm_lam_re, ssm_lam_im, ssm_log_dt, ssm_b_re, ssm_b_im, ssm_c_re, ssm_c_im, ssm_d, w_glu, b_glu, w_out, norm2_g, w_up, conv_w, conv_b, w_down, final_g):
    bp, sp, _ = x_prompt.shape
    nseq, steps, _ = x_sample.shape
    ns = nseq * steps
    cw_p = min(MAX_WINDOW, sp)
    cw_s = cache_c_k.shape[2]

    rope_p = rope_tables(sp, 0)
    rope_s = tuple(jnp.tile(t, (nseq, 1)) for t in rope_tables(steps, PAST_LEN))
    bbr, bbi, pr, pi, ncim = s5_params(ssm_lam_re, ssm_lam_im, ssm_log_dt, ssm_b_re, ssm_b_im, ssm_c_im, S5_SEGLEN)

    w_in_b, w_out_b, w_up_b, w_down_b, w_glu_b = (w.astype(BF16) for w in (w_in, w_out, w_up, w_down, w_glu))
    expand = (jnp.arange(LANES)[:, None] == jnp.arange(C_WIDTH)[None, :] // HEAD_DIM).astype(BF16)
    bs_p = jnp.repeat(jnp.transpose(b_s, (0, 2, 1)), HEAD_DIM, axis=2)
    bs_s = jnp.tile(bs_p[:, :steps], (1, CHUNK // steps, 1))
    ws_s = jnp.tile(w_s[:, :, :steps, :steps], (1, 1, CHUNK // steps, CHUNK // steps))
    row1 = lambda a: a.reshape(1, -1)

    hp = x_prompt
    hs = x_sample.reshape(1, ns, D_MODEL)
    outs = [[] for _ in range(11)]
    for l in range(DEPTH):
        last = l == DEPTH - 1
        wb, wc, prl, pil = s5_layouts(bbr[l], bbi[l], ssm_c_re[l], ncim[l], pr[l], pi[l])
        s5w = (wb, wc, prl, pil, row1(ssm_d[l]), w_glu_b[l], row1(b_glu[l]))
        ffw = (w_out_b[l], row1(norm2_g[l]), w_up_b[l], conv_w[l], row1(conv_b[l]), w_down_b[l], row1(final_g))

        oa, _, ub, q, k, v, kf, vf = in_projection(hp, row1(norm1_g[l]), w_in_b[l], w_s[l], bs_p[l], rope_p,
                                                   tm=PROMPT_TM, seq_rows=CHUNK, tail_rows=cw_p)
        ob, hr, hi = s5_prompt(ub, *s5w, seglen=S5_SEGLEN)
        o_list, st_list = zip(*[attn_prompt(q, k, v, dil, tq=ATTN_TQ) for _, dil in DILATIONS])
        flat = lambda a: a.reshape(bp * sp, a.shape[-1])
        y, cg, cu = out_projection(flat(hp), flat(oa), flat(ob), [flat(o) for o in o_list],
                                   [flat(s) for s in st_list], expand, *ffw, None,
                                   tm=PROMPT_TM, nseq=1, seq_len=PROMPT_TM, tiles_per_seq=sp // PROMPT_TM,
                                   final_norm=last)
        hp = y.reshape(bp, sp, D_MODEL)
        outs[0].append(kf.reshape(bp, cw_p, C_HEADS, HEAD_DIM))
        outs[1].append(vf.reshape(bp, cw_p, C_HEADS, HEAD_DIM))
        outs[4].append(hr.reshape(bp, B_GROUPS, SSM_STATE))
        outs[5].append(hi.reshape(bp, B_GROUPS, SSM_STATE))
        seq_end = lambda a: a.reshape(bp, sp // PROMPT_TM, SUBLANES, D_FF)[:, -1, -2:]
        outs[8].append(jnp.concatenate([seq_end(cg), seq_end(cu)], -1))

        oa, vn, ub, q, k, v, kf, vf = in_projection(hs, row1(norm1_g[l]), w_in_b[l], ws_s[l], bs_s[l], rope_s,
                                                    tm=ns, seq_rows=steps, tail_rows=ns)
        h0r = jnp.transpose(state_ssm_re[l].reshape(nseq, S5_BLOCKS, S5_HALF), (1, 0, 2))
        h0i = jnp.transpose(state_ssm_im[l].reshape(nseq, S5_BLOCKS, S5_HALF), (1, 0, 2))
        ob, hr, hi = s5_sample(ub[0], h0r, h0i, *s5w, nseq=nseq, steps=steps)
        oc = attn_sample(q[0].astype(F32), k[0].astype(F32), v[0].astype(F32), cache_c_k[l].reshape(nseq, cw_s, C_WIDTH),
                         cache_c_v[l].reshape(nseq, cw_s, C_WIDTH), steps=steps)
        y, cg, cu = out_projection(hs[0], oa[0], ob, [oc], [], expand, *ffw, state_ffn_conv[l],
                                   tm=ns, nseq=nseq, seq_len=steps, tiles_per_seq=1, final_norm=last)
        hs = y.reshape(1, ns, D_MODEL)
        outs[2].append(kf.reshape(nseq, steps, C_HEADS, HEAD_DIM))
        outs[3].append(vf.reshape(nseq, steps, C_HEADS, HEAD_DIM))
        outs[6].append(jnp.transpose(hr, (1, 0, 2)).reshape(nseq, B_GROUPS, SSM_STATE))
        outs[7].append(jnp.transpose(hi, (1, 0, 2)).reshape(nseq, B_GROUPS, SSM_STATE))
        outs[9].append(jnp.concatenate([cg[:, -2:], cu[:, -2:]], -1))
        outs[10].append(vn.reshape(nseq, steps, A_WIDTH))

    return (hp, hs.reshape(nseq, steps, D_MODEL)) + tuple(jnp.stack(o) for o in outs)
```

```python
import functools
import math

import jax
import jax.numpy as jnp
from jax import lax
from jax.experimental import pallas as pl
from jax.experimental.pallas import tpu as pltpu

F32 = jnp.float32
BF16 = jnp.bfloat16

D_MODEL = 1024
DEPTH = 4
PAST_LEN = 8192
HEAD_DIM = 64
A_WIDTH = 256
B_WIDTH = 384
C_WIDTH = 384
A_HEADS = 4
C_HEADS = 6
CHUNK = 128
SSM_GROUP = 16
B_GROUPS = 24
SSM_STATE = 64
DILATIONS = ((128, 1), (512, 4), (2048, 16))
MAX_WINDOW = 2048
ROPE_THETA = 500000.0
ROT_DIM = 16
D_FF = 2816
EPS = 1e-6
NEG_INF = -1e30

O_A, O_B, O_Q, O_K, O_V = 0, 2 * A_WIDTH, 896, 1280, 1664

LANES = 128
SUBLANES = 8
S5_BLOCKS = B_WIDTH // LANES
S5_HALF = 512
FF_CHUNK = 256
N_FF = D_FF // FF_CHUNK
VMEM_LIMIT = 56 * 1024 * 1024


def _dot(a, b):
    return jnp.dot(a.astype(BF16), b.astype(BF16), preferred_element_type=F32)


def _rms(x, g):
    return x * lax.rsqrt(jnp.mean(x * x, -1, keepdims=True) + EPS) * g


def _params(*sem):
    return pltpu.CompilerParams(dimension_semantics=sem, vmem_limit_bytes=VMEM_LIMIT)


def _rope_tables_kernel(c_ref, a_ref, b_ref, *, pos0):
    n = c_ref.shape[0]
    pos = (lax.broadcasted_iota(jnp.int32, (n, LANES), 0) + (pos0 + pl.program_id(0) * n)).astype(F32)
    d = lax.broadcasted_iota(jnp.int32, (n, LANES), 1) & (HEAD_DIM - 1)
    k = (d & (ROT_DIM // 2 - 1)).astype(F32)
    inv = jnp.exp(k * (-2.0 / ROT_DIM * math.log(ROPE_THETA)))
    ang = pos * inv
    cos, sin = jnp.cos(ang), jnp.sin(ang)
    c_ref[...] = jnp.where(d < ROT_DIM, cos, 1.0)
    a_ref[...] = jnp.where(d < ROT_DIM // 2, -sin, 0.0)
    b_ref[...] = jnp.where((d >= ROT_DIM // 2) & (d < ROT_DIM), sin, 0.0)


def rope_tables(n, pos0):
    sds = jax.ShapeDtypeStruct((n, LANES), F32)
    tr = min(n, 1024)
    spec = pl.BlockSpec((tr, LANES), lambda i: (i, 0))
    return pl.pallas_call(functools.partial(_rope_tables_kernel, pos0=pos0), grid=(n // tr,),
                          out_specs=(spec, spec, spec), out_shape=(sds, sds, sds),
                          compiler_params=_params("arbitrary"), name="rope_tables")()


def _s5_params_kernel(lr_ref, li_ref, ldt_ref, br_ref, bi_ref, cim_ref,
                      bbr_ref, bbi_ref, pr_ref, pi_ref, ncim_ref, *, n_pow):
    lr, li = lr_ref[...], li_ref[...]
    dt = jnp.exp(ldt_ref[...])
    mag = jnp.exp(lr * dt)
    ar, ai = mag * jnp.cos(li * dt), mag * jnp.sin(li * dt)
    nr, ni = ar - 1.0, ai
    den = lr * lr + li * li
    fr, fi = (nr * lr + ni * li) / den, (ni * lr - nr * li) / den
    br, bi = br_ref[...], bi_ref[...]
    bbr_ref[...] = fr[None] * br - fi[None] * bi
    bbi_ref[...] = fr[None] * bi + fi[None] * br
    k = (lax.broadcasted_iota(jnp.int32, (n_pow,) + lr.shape, 0) + 1).astype(F32)
    magk = jnp.exp(k * (lr * dt)[None])
    angk = k * (li * dt)[None]
    pr_ref[...] = magk * jnp.cos(angk)
    pi_ref[...] = magk * jnp.sin(angk)
    ncim_ref[...] = -cim_ref[...]


def s5_params(lam_re, lam_im, log_dt, b_re, b_im, c_im, n_pow):
    depth, g, p = lam_re.shape
    c = b_re.shape[-1]
    ldt = jnp.broadcast_to(log_dt[:, :, None], (depth, g, p))
    brt = jnp.transpose(b_re, (0, 3, 1, 2))
    bit = jnp.transpose(b_im, (0, 3, 1, 2))
    gp = pl.BlockSpec((None, g, p), lambda l: (l, 0, 0))
    cgp = pl.BlockSpec((None, c, g, p), lambda l: (l, 0, 0, 0))
    gcp = pl.BlockSpec((None, g, c, p), lambda l: (l, 0, 0, 0))
    kgp = pl.BlockSpec((None, n_pow, g, p), lambda l: (l, 0, 0, 0))
    return pl.pallas_call(
        functools.partial(_s5_params_kernel, n_pow=n_pow),
        grid=(depth,),
        in_specs=[gp, gp, gp, cgp, cgp, gcp],
        out_specs=(cgp, cgp, kgp, kgp, gcp),
        out_shape=(jax.ShapeDtypeStruct((depth, c, g, p), F32), jax.ShapeDtypeStruct((depth, c, g, p), F32),
                   jax.ShapeDtypeStruct((depth, n_pow, g, p), F32), jax.ShapeDtypeStruct((depth, n_pow, g, p), F32),
                   jax.ShapeDtypeStruct((depth, g, c, p), F32)),
        compiler_params=_params("arbitrary"),
        name="s5_params",
    )(lam_re, lam_im, ldt, brt, bit, c_im)


def _s5_lane_vec(t):
    return t.reshape(t.shape[:-2] + (S5_BLOCKS, 1, S5_HALF))


def s5_layouts(bbr, bbi, c_re, ncim, pr, pi):
    eye = jnp.eye(SUBLANES, dtype=F32)
    bb = jnp.stack([bbr, bbi], 0).reshape(2, SSM_GROUP, S5_BLOCKS, 8, SSM_STATE)
    wb = jnp.einsum('ecjgp,gh->jgcehp', bb, eye).reshape(S5_BLOCKS, LANES, 2 * S5_HALF)
    cc = jnp.stack([c_re, ncim], 0).reshape(2, S5_BLOCKS, 8, SSM_GROUP, SSM_STATE)
    wc = jnp.einsum('ejgcp,gh->jeghpc', cc, eye)
    wc = jnp.transpose(wc, (0, 1, 2, 4, 3, 5)).reshape(S5_BLOCKS, 2 * S5_HALF, LANES)
    n_pow = pr.shape[0]
    prl = jnp.transpose(pr.reshape(n_pow, S5_BLOCKS, S5_HALF), (1, 0, 2))
    pil = jnp.transpose(pi.reshape(n_pow, S5_BLOCKS, S5_HALF), (1, 0, 2))
    return wb.astype(BF16), wc.astype(BF16), prl, pil


def _rope_apply(x, c, a, b):
    return x * c + pltpu.roll(x, LANES - ROT_DIM // 2, 1) * a + pltpu.roll(x, ROT_DIM // 2, 1) * b


def _inproj_kernel(x_ref, g_ref, w_ref, ws_ref, bs_ref, rc_ref, ra_ref, rb_ref,
                   oa_ref, vn_ref, ub_ref, kf_ref, vf_ref, *rest, seq_rows, dils):
    qkv_refs, perm_scr = rest[:-1], rest[-1]
    tm = x_ref.shape[0]
    xn = _rms(x_ref[...], g_ref[...]).astype(BF16)

    h = jax.nn.gelu(jnp.dot(xn, w_ref[:, O_A:O_B], preferred_element_type=F32))
    u, v = h[:, :A_WIDTH], h[:, A_WIDTH:]
    mu = jnp.mean(v, -1, keepdims=True)
    var = jnp.mean(jnp.square(v - mu), -1, keepdims=True)
    vn = (v - mu) * lax.rsqrt(var + EPS)
    vn_ref[...] = vn
    ri = lax.broadcasted_iota(jnp.int32, (CHUNK, CHUNK), 0)
    ci = lax.broadcasted_iota(jnp.int32, (CHUNK, CHUNK), 1)
    keep = (ci <= ri) & ((ri // seq_rows) == (ci // seq_rows))
    lane = lax.broadcasted_iota(jnp.int32, (CHUNK, LANES), 1)
    wm = [jnp.where(keep, ws_ref[hh], 0.0).astype(BF16) for hh in range(A_HEADS)]
    vnb = vn.astype(BF16)
    for c in range(tm // CHUNK):
        rows = slice(c * CHUNK, (c + 1) * CHUNK)
        for p in range(A_WIDTH // LANES):
            cols = slice(p * LANES, (p + 1) * LANES)
            vp = vnb[rows, cols]
            m0 = jnp.dot(wm[2 * p], vp, preferred_element_type=F32)
            m1 = jnp.dot(wm[2 * p + 1], vp, preferred_element_type=F32)
            mixed = jnp.where(lane < HEAD_DIM, m0, m1) + bs_ref[:, cols]
            oa_ref[rows, cols] = u[rows, cols] * mixed

    ub_ref[...] = jnp.dot(xn, w_ref[:, O_B:O_Q], preferred_element_type=F32)

    rc, ra, rb = rc_ref[...], ra_ref[...], rb_ref[...]
    def emit(val, outs, slot, p):
        if len(dils) > 1:
            perm_scr[slot] = val
        for dil, ref in zip(dils, outs):
            if dil == 1:
                ref[:, p * LANES:(p + 1) * LANES] = val.astype(BF16)
                continue
            for r in range(dil):
                c0 = r * C_WIDTH + p * LANES
                ref[:, c0:c0 + LANES] = perm_scr[slot, pl.ds(r, tm // dil, stride=dil), :].astype(BF16)

    nd = len(dils)
    q_outs, k_outs, v_outs = qkv_refs[0:nd], qkv_refs[nd:2 * nd], qkv_refs[2 * nd:3 * nd]
    for p in range(C_WIDTH // LANES):
        cols = slice(p * LANES, (p + 1) * LANES)
        q = jnp.dot(xn, w_ref[:, O_Q + p * LANES:O_Q + (p + 1) * LANES], preferred_element_type=F32)
        emit(_rope_apply(q, rc, ra, rb) * (HEAD_DIM ** -0.5), q_outs, 3 * p, p)
        k = _rope_apply(jnp.dot(xn, w_ref[:, O_K + p * LANES:O_K + (p + 1) * LANES], preferred_element_type=F32),
                        rc, ra, rb)
        emit(k, k_outs, 3 * p + 1, p)
        kf_ref[:, cols] = k
        vv = jnp.dot(xn, w_ref[:, O_V + p * LANES:O_V + (p + 1) * LANES], preferred_element_type=F32)
        emit(vv, v_outs, 3 * p + 2, p)
        vf_ref[:, cols] = vv


def in_projection(x, g1, w_in, w_s, bs_rows, rope, *, tm, seq_rows, tail_rows, dils):
    bsz, s, _ = x.shape
    nt = s // tm
    tail_t = tail_rows // tm
    row = lambda w: pl.BlockSpec((None, tm, w), lambda b, t: (b, t, 0))
    tail = pl.BlockSpec((None, tm, C_WIDTH), lambda b, t: (b, jnp.maximum(t - (nt - tail_t), 0), 0))
    full = lambda shp: pl.BlockSpec(shp, lambda b, t: (0,) * len(shp))
    ropespec = pl.BlockSpec((tm, LANES), lambda b, t: (t, 0))
    sd = lambda w, dt: jax.ShapeDtypeStruct((bsz, s, w), dt)
    tail_sd = jax.ShapeDtypeStruct((bsz, tail_rows, C_WIDTH), F32)
    qkv_specs = [pl.BlockSpec((None, tm // d, d * C_WIDTH), lambda b, t: (b, t, 0)) for d in dils] * 3
    qkv_sds = [jax.ShapeDtypeStruct((bsz, s // d, d * C_WIDTH), BF16) for d in dils] * 3
    return pl.pallas_call(
        functools.partial(_inproj_kernel, seq_rows=seq_rows, dils=tuple(dils)),
        grid=(bsz, nt),
        in_specs=[row(D_MODEL), full((1, D_MODEL)), full(w_in.shape), full(w_s.shape), full(bs_rows.shape),
                  ropespec, ropespec, ropespec],
        out_specs=(row(A_WIDTH), row(A_WIDTH), row(B_WIDTH), tail, tail, *qkv_specs),
        out_shape=(sd(A_WIDTH, F32), sd(A_WIDTH, F32), sd(B_WIDTH, F32), tail_sd, tail_sd, *qkv_sds),
        scratch_shapes=[pltpu.VMEM((3 * (C_WIDTH // LANES), tm, LANES), F32)],
        compiler_params=_params("arbitrary", "arbitrary"),
        name="in_projection",
    )(x, g1, w_in, w_s, bs_rows, *rope)


def _s5_readout(hb, uperm, gs, wc_ref, d_ref, wg_ref, bg_ref):
    for j in range(S5_BLOCKS):
        cols = slice(j * LANES, (j + 1) * LANES)
        y = jnp.dot(hb[j], wc_ref[j], preferred_element_type=F32) + d_ref[:, cols] * uperm[:, cols]
        gs[:, cols] = jax.nn.gelu(y)
    g = gs[...]
    gs[...] = g * jax.nn.sigmoid(_dot(g, wg_ref[...]) + bg_ref[...])


def _s5_prompt_kernel(u_ref, wb_ref, wc_ref, pr_ref, pi_ref, d_ref, wg_ref, bg_ref,
                      o_ref, hr_ref, hi_ref,
                      upad, uperm, xs, hb, cs, carry, gs, *, seglen):
    pitch = seglen + SUBLANES

    @pl.when(pl.program_id(1) == 0)
    def _():
        carry[...] = jnp.zeros_like(carry)

    for s in range(SUBLANES):
        for p in range(S5_BLOCKS):
            upad[p, s * pitch:s * pitch + seglen, :] = u_ref[s * seglen:(s + 1) * seglen, p * LANES:(p + 1) * LANES]

    def perm(i, _):
        r = pl.multiple_of(i * SUBLANES, SUBLANES)
        for p in range(S5_BLOCKS):
            uperm[pl.ds(r, SUBLANES), p * LANES:(p + 1) * LANES] = upad[p, pl.ds(i, SUBLANES, stride=pitch), :]
        return 0
    lax.fori_loop(0, seglen, perm, 0)

    for j in range(S5_BLOCKS):
        xs[j] = _dot(uperm[:, j * LANES:(j + 1) * LANES], wb_ref[j])

    for j in range(S5_BLOCKS):
        ar = jnp.broadcast_to(pr_ref[j, 0:1, :], (SUBLANES, S5_HALF))
        ai = jnp.broadcast_to(pi_ref[j, 0:1, :], (SUBLANES, S5_HALF))

        def scan(i, hc, j=j, ar=ar, ai=ai):
            hr, hi = hc
            r = pl.multiple_of(i * SUBLANES, SUBLANES)
            nhr = ar * hr - ai * hi + xs[j, pl.ds(r, SUBLANES), 0:S5_HALF]
            nhi = ar * hi + ai * hr + xs[j, pl.ds(r, SUBLANES), S5_HALF:2 * S5_HALF]
            xs[j, pl.ds(r, SUBLANES), 0:S5_HALF] = nhr
            xs[j, pl.ds(r, SUBLANES), S5_HALF:2 * S5_HALF] = nhi
            return nhr, nhi
        z = jnp.zeros((SUBLANES, S5_HALF), F32)
        er, ei = lax.fori_loop(0, seglen, scan, (z, z))

        a64r, a64i = pr_ref[j, seglen - 1:seglen, :], pi_ref[j, seglen - 1:seglen, :]
        cr, ci = carry[j, 0, 0:1, :], carry[j, 1, 0:1, :]
        for s in range(SUBLANES):
            cs[j, 0, s:s + 1, :] = cr
            cs[j, 1, s:s + 1, :] = ci
            cr, ci = (a64r * cr - a64i * ci + er[s:s + 1, :], a64r * ci + a64i * cr + ei[s:s + 1, :])
        carry[j, 0, 0:1, :] = cr
        carry[j, 1, 0:1, :] = ci
        hr_ref[j] = cr
        hi_ref[j] = ci

        cr8, ci8 = cs[j, 0], cs[j, 1]

        def fix(i2, _, j=j, cr8=cr8, ci8=ci8):
            r = pl.multiple_of(i2 * 2 * SUBLANES, 2 * SUBLANES)
            hrs, his = [], []
            for t in range(2):
                i = i2 * 2 + t
                pr, pi = pr_ref[j, pl.ds(i, 1), :], pi_ref[j, pl.ds(i, 1), :]
                rows = pl.ds(r + t * SUBLANES, SUBLANES)
                hrs.append(xs[j, rows, 0:S5_HALF] + pr * cr8 - pi * ci8)
                his.append(xs[j, rows, S5_HALF:2 * S5_HALF] + pr * ci8 + pi * cr8)
            hb[j, pl.ds(r, 2 * SUBLANES), 0:S5_HALF] = jnp.concatenate(hrs, 0).astype(BF16)
            hb[j, pl.ds(r, 2 * SUBLANES), S5_HALF:2 * S5_HALF] = jnp.concatenate(his, 0).astype(BF16)
            return 0
        lax.fori_loop(0, seglen // 2, fix, 0)

    _s5_readout(hb, uperm, gs, wc_ref, d_ref, wg_ref, bg_ref)

    def unperm(i, _):
        r = pl.multiple_of(i * SUBLANES, SUBLANES)
        for p in range(S5_BLOCKS):
            upad[p, pl.ds(i, SUBLANES, stride=pitch), :] = gs[pl.ds(r, SUBLANES), p * LANES:(p + 1) * LANES]
        return 0
    lax.fori_loop(0, seglen, unperm, 0)
    for s in range(SUBLANES):
        for p in range(S5_BLOCKS):
            o_ref[s * seglen:(s + 1) * seglen, p * LANES:(p + 1) * LANES] = upad[p, s * pitch:s * pitch + seglen, :]


def s5_prompt(u, wb, wc, prl, pil, d_skip, w_glu, b_glu, *, seglen):
    bsz, s, _ = u.shape
    t = SUBLANES * seglen
    pitch = seglen + SUBLANES
    full = lambda shp: pl.BlockSpec(shp, lambda b, c: (0,) * len(shp))
    st = pl.BlockSpec((None, S5_BLOCKS, 1, S5_HALF), lambda b, c: (b, 0, 0, 0))
    rows = pl.BlockSpec((None, t, B_WIDTH), lambda b, c: (b, c, 0))
    return pl.pallas_call(
        functools.partial(_s5_prompt_kernel, seglen=seglen),
        grid=(bsz, s // t),
        in_specs=[rows, full(wb.shape), full(wc.shape), full(prl.shape), full(pil.shape),
                  full((1, B_WIDTH)), full(w_glu.shape), full((1, B_WIDTH))],
        out_specs=(rows, st, st),
        out_shape=(jax.ShapeDtypeStruct((bsz, s, B_WIDTH), F32),
                   jax.ShapeDtypeStruct((bsz, S5_BLOCKS, 1, S5_HALF), F32),
                   jax.ShapeDtypeStruct((bsz, S5_BLOCKS, 1, S5_HALF), F32)),
        scratch_shapes=[pltpu.VMEM((S5_BLOCKS, SUBLANES * pitch, LANES), F32),
                        pltpu.VMEM((t, B_WIDTH), F32),
                        pltpu.VMEM((S5_BLOCKS, t, 2 * S5_HALF), F32),
                        pltpu.VMEM((S5_BLOCKS, t, 2 * S5_HALF), BF16),
                        pltpu.VMEM((S5_BLOCKS, 2, SUBLANES, S5_HALF), F32),
                        pltpu.VMEM((S5_BLOCKS, 2, SUBLANES, S5_HALF), F32),
                        pltpu.VMEM((t, B_WIDTH), F32)],
        compiler_params=_params("arbitrary", "arbitrary"),
        name="s5_prompt",
    )(u, wb, wc, prl, pil, d_skip, w_glu, b_glu)


def _s5_sample_kernel(u_ref, h0r_ref, h0i_ref, wb_ref, wc_ref, pr_ref, pi_ref, d_ref, wg_ref, bg_ref,
                      o_ref, hr_ref, hi_ref,
                      upad, uperm, xs, hb, gs, *, nseq, steps):
    nblk = nseq // SUBLANES
    for p in range(S5_BLOCKS):
        upad[p] = u_ref[:, p * LANES:(p + 1) * LANES]
    for st in range(steps):
        for bb in range(nblk):
            r = (st * nblk + bb) * SUBLANES
            for p in range(S5_BLOCKS):
                uperm[r:r + SUBLANES, p * LANES:(p + 1) * LANES] = (
                    upad[p, pl.ds(bb * SUBLANES * steps + st, SUBLANES, stride=steps), :])
    for j in range(S5_BLOCKS):
        xs[j] = _dot(uperm[:, j * LANES:(j + 1) * LANES], wb_ref[j])
    for j in range(S5_BLOCKS):
        ar = jnp.broadcast_to(pr_ref[j, 0:1, :], (SUBLANES, S5_HALF))
        ai = jnp.broadcast_to(pi_ref[j, 0:1, :], (SUBLANES, S5_HALF))
        for bb in range(nblk):
            seqs = slice(bb * SUBLANES, (bb + 1) * SUBLANES)
            hr, hi = h0r_ref[j, seqs, :], h0i_ref[j, seqs, :]
            for st in range(steps):
                r = (st * nblk + bb) * SUBLANES
                hr, hi = (ar * hr - ai * hi + xs[j, r:r + SUBLANES, 0:S5_HALF],
                          ar * hi + ai * hr + xs[j, r:r + SUBLANES, S5_HALF:2 * S5_HALF])
                xs[j, r:r + SUBLANES, 0:S5_HALF] = hr
                xs[j, r:r + SUBLANES, S5_HALF:2 * S5_HALF] = hi
            hr_ref[j, seqs, :] = hr
            hi_ref[j, seqs, :] = hi
        hb[j] = xs[j].astype(BF16)
    _s5_readout(hb, uperm, gs, wc_ref, d_ref, wg_ref, bg_ref)
    for st in range(steps):
        for bb in range(nblk):
            r = (st * nblk + bb) * SUBLANES
            for p in range(S5_BLOCKS):
                upad[p, pl.ds(bb * SUBLANES * steps + st, SUBLANES, stride=steps), :] = (
                    gs[r:r + SUBLANES, p * LANES:(p + 1) * LANES])
    for p in range(S5_BLOCKS):
        o_ref[:, p * LANES:(p + 1) * LANES] = upad[p]


def s5_sample(u, h0r, h0i, wb, wc, prl, pil, d_skip, w_glu, b_glu, *, nseq, steps):
    n = nseq * steps
    st_sd = jax.ShapeDtypeStruct((S5_BLOCKS, nseq, S5_HALF), F32)
    return pl.pallas_call(
        functools.partial(_s5_sample_kernel, nseq=nseq, steps=steps),
        out_shape=(jax.ShapeDtypeStruct((n, B_WIDTH), F32), st_sd, st_sd),
        scratch_shapes=[pltpu.VMEM((S5_BLOCKS, n, LANES), F32),
                        pltpu.VMEM((n, B_WIDTH), F32),
                        pltpu.VMEM((S5_BLOCKS, n, 2 * S5_HALF), F32),
                        pltpu.VMEM((S5_BLOCKS, n, 2 * S5_HALF), BF16),
                        pltpu.VMEM((n, B_WIDTH), F32)],
        compiler_params=pltpu.CompilerParams(vmem_limit_bytes=VMEM_LIMIT),
        name="s5_sample",
    )(u, h0r, h0i, wb, wc, prl, pil, d_skip, w_glu, b_glu)


def _attn_prompt_kernel(q_ref, kp_ref, kc_ref, vp_ref, vc_ref, o_ref, st_ref, kx, vx):
    tq = q_ref.shape[0]
    t = pl.program_id(2)
    kx[0:CHUNK, :] = kp_ref[...]
    kx[CHUNK:CHUNK + tq, :] = kc_ref[...]
    vx[0:CHUNK, :] = vp_ref[...]
    vx[CHUNK:CHUNK + tq, :] = vc_ref[...]
    r = lax.broadcasted_iota(jnp.int32, (CHUNK, 2 * CHUNK), 0)
    c = lax.broadcasted_iota(jnp.int32, (CHUNK, 2 * CHUNK), 1)
    band = (c >= r) & (c <= r + CHUNK)
    lane = lax.broadcasted_iota(jnp.int32, (CHUNK, LANES), 1)
    lo_half = lane < HEAD_DIM
    for u in range(tq // CHUNK):
        rows = slice(u * CHUNK, (u + 1) * CHUNK)
        keys = slice(u * CHUNK, (u + 2) * CHUNK)
        if u == 0:
            valid = band & (c >= jnp.where(t == 0, CHUNK, 0))
        else:
            valid = band
        bias = jnp.where(valid, 0.0, NEG_INF)
        st = jnp.zeros((CHUNK, LANES), F32)
        for p in range(C_WIDTH // LANES):
            cols = slice(p * LANES, (p + 1) * LANES)
            qp = q_ref[rows, cols]
            kpair = kx[keys, cols]
            vpair = vx[keys, cols]
            outs = []
            for hh in range(2):
                qm = jnp.where(lo_half if hh == 0 else ~lo_half, qp, jnp.zeros_like(qp))
                sc = lax.dot_general(qm, kpair, (((1,), (1,)), ((), ())), preferred_element_type=F32) + bias
                m = jnp.max(sc, -1, keepdims=True)
                pe = jnp.exp(sc - m)
                l = jnp.sum(pe, -1, keepdims=True)
                outs.append(jnp.dot(pe.astype(BF16), vpair, preferred_element_type=F32) / l)
                st = jnp.where(lane == 2 * p + hh, m + jnp.log(l), st)
            o_ref[rows, cols] = jnp.where(lo_half, outs[0], outs[1])
        st_ref[rows, :] = st


def attn_prompt(q, k, v, dil, *, tq):
    bsz, ln, _ = q.shape
    tq = min(tq, ln)
    cur = pl.BlockSpec((None, tq, C_WIDTH), lambda b, r, t: (b, t, r))
    prev = pl.BlockSpec((None, CHUNK, C_WIDTH), lambda b, r, t: (b, jnp.maximum(t * (tq // CHUNK) - 1, 0), r))
    return pl.pallas_call(
        _attn_prompt_kernel,
        grid=(bsz, dil, ln // tq),
        in_specs=[cur, prev, cur, prev, cur],
        out_specs=(cur, pl.BlockSpec((None, tq, LANES), lambda b, r, t: (b, t, r))),
        out_shape=(jax.ShapeDtypeStruct((bsz, ln, dil * C_WIDTH), F32),
                   jax.ShapeDtypeStruct((bsz, ln, dil * LANES), F32)),
        scratch_shapes=[pltpu.VMEM((CHUNK + tq, C_WIDTH), BF16), pltpu.VMEM((CHUNK + tq, C_WIDTH), BF16)],
        compiler_params=_params("arbitrary", "arbitrary", "arbitrary"),
        name=f"attn_prompt_d{dil}",
    )(q, k, k, v, v)


def _attn_sample_kernel(q_ref, kn_ref, vn_ref, ck_ref, cv_ref, o_ref, *, steps):
    cw = ck_ref.shape[-1]

    def mult(delta):
        cnt = jnp.zeros(delta.shape, F32)
        for window, dil in DILATIONS:
            cnt = cnt + jnp.where((delta >= 0) & (delta <= window) & ((delta & (dil - 1)) == 0), 1.0, 0.0)
        return cnt

    cnt_c = mult(cw + lax.broadcasted_iota(jnp.int32, (steps, cw), 0)
                 - lax.broadcasted_iota(jnp.int32, (steps, cw), 1))
    cnt_n = mult(lax.broadcasted_iota(jnp.int32, (steps, CHUNK), 0)
                 - lax.broadcasted_iota(jnp.int32, (steps, CHUNK), 1))
    live_c, live_n = cnt_c > 0, cnt_n > 0
    zpad = jnp.zeros((CHUNK - steps, C_WIDTH), F32)
    kn = jnp.concatenate([kn_ref[...], zpad], 0).astype(BF16)
    vn = jnp.concatenate([vn_ref[...], zpad], 0).astype(BF16)
    q = q_ref[...].astype(BF16)
    nt_dims = (((1,), (1,)), ((), ()))
    for hh in range(C_HEADS):
        cols = slice(hh * HEAD_DIM, (hh + 1) * HEAD_DIM)
        qh = q[:, cols]
        sc = jnp.dot(qh, ck_ref[hh].astype(BF16), preferred_element_type=F32)
        sn = lax.dot_general(qh, kn[:, cols], nt_dims, preferred_element_type=F32)
        m = jnp.maximum(jnp.max(jnp.where(live_c, sc, NEG_INF), -1, keepdims=True),
                        jnp.max(jnp.where(live_n, sn, NEG_INF), -1, keepdims=True))
        ec = cnt_c * jnp.exp(jnp.where(live_c, sc - m, NEG_INF))
        en = cnt_n * jnp.exp(jnp.where(live_n, sn - m, NEG_INF))
        l = jnp.sum(ec, -1, keepdims=True) + jnp.sum(en, -1, keepdims=True)
        acc = (lax.dot_general(ec.astype(BF16), cv_ref[hh].astype(BF16), nt_dims, preferred_element_type=F32)
               + jnp.dot(en.astype(BF16), vn[:, cols], preferred_element_type=F32))
        o_ref[:, cols] = acc / l


def attn_sample(q, k_new, v_new, cache_kt, cache_vt, layer, *, steps):
    _, nseq, _, _, cw = cache_kt.shape
    rows = pl.BlockSpec((steps, C_WIDTH), lambda b: (b, 0))
    cache = pl.BlockSpec((None, None, C_HEADS, HEAD_DIM, cw), lambda b: (layer, b, 0, 0, 0))
    return pl.pallas_call(
        functools.partial(_attn_sample_kernel, steps=steps),
        grid=(nseq,),
        in_specs=[rows, rows, rows, cache, cache],
        out_specs=rows,
        out_shape=jax.ShapeDtypeStruct((nseq * steps, C_WIDTH), F32),
        compiler_params=_params("arbitrary"),
        name="attn_sample",
    )(q, k_new, v_new, cache_kt, cache_vt)


FF_ROWS = 128


def _residual_and_norm(x_ref, oa_ref, ob_ref, oc, wo_ref, g2_ref, y_ref, xn_scr):
    mix = (_dot(oa_ref[...], wo_ref[0:A_WIDTH, :]) + _dot(ob_ref[...], wo_ref[A_WIDTH:A_WIDTH + B_WIDTH, :])
           + _dot(oc, wo_ref[A_WIDTH + B_WIDTH:, :]))
    x1 = x_ref[...] + mix
    y_ref[...] = x1
    xn_scr[...] = _rms(x1, g2_ref[...]).astype(BF16)


def _outproj_prompt_kernel(x_ref, oa_ref, ob_ref, *rest, tiles_per_seq, dils, final_norm):
    nb = len(dils)
    o_refs, st_refs = rest[:nb], rest[nb:2 * nb]
    (e_ref, wo_ref, g2_ref, wug_ref, wuu_ref, cwg_ref, cwu_ref, cbg_ref, cbu_ref, wd_ref, fg_ref,
     y_ref, cg_ref, cu_ref, xn_scr, carry_scr, unp_scr) = rest[2 * nb:]
    i = pl.program_id(0)
    j = pl.program_id(1)
    tm = x_ref.shape[0]
    nslab = C_WIDTH // LANES

    @pl.when(j == 0)
    def _():
        outs, sts = [], []
        slot = 0
        for dil, o_ref, s_ref in zip(dils, o_refs, st_refs):
            if dil == 1:
                outs.append(o_ref[...])
                sts.append(s_ref[...])
                continue
            for r in range(dil):
                rows = pl.ds(r, tm // dil, stride=dil)
                for p in range(nslab):
                    c0 = r * C_WIDTH + p * LANES
                    unp_scr[slot + p, rows, :] = o_ref[:, c0:c0 + LANES]
                unp_scr[slot + nslab, rows, :] = s_ref[:, r * LANES:(r + 1) * LANES]
            outs.append(jnp.concatenate([unp_scr[slot + p] for p in range(nslab)], axis=1))
            sts.append(unp_scr[slot + nslab])
            slot += nslab + 1
        mx = functools.reduce(jnp.maximum, sts)
        ws = [jnp.exp(s - mx) for s in sts]
        tot = functools.reduce(lambda a, b: a + b, ws)
        oc = jnp.zeros((tm, C_WIDTH), F32)
        for w, o in zip(ws, outs):
            alpha = w / tot
            hi = alpha.astype(BF16)
            lo = (alpha - hi.astype(F32)).astype(BF16)
            wide = (jnp.dot(hi, e_ref[...], preferred_element_type=F32)
                    + jnp.dot(lo, e_ref[...], preferred_element_type=F32))
            oc = oc + wide * o
        _residual_and_norm(x_ref, oa_ref, ob_ref, oc, wo_ref, g2_ref, y_ref, xn_scr)

    @pl.when((i % tiles_per_seq) == 0)
    def _():
        carry_scr[j] = jnp.zeros((SUBLANES, 2 * FF_CHUNK), F32)

    cwg, cwu, cbg, cbu = cwg_ref[...], cwu_ref[...], cbg_ref[...], cbu_ref[...]
    prev_g, prev_u = carry_scr[j, :, 0:FF_CHUNK], carry_scr[j, :, FF_CHUNK:]

    def conv(h, prev, cw, cb):
        ext = jnp.concatenate([prev, h], axis=0)
        n = h.shape[0]
        return cb + (ext[SUBLANES - 2:SUBLANES - 2 + n] * cw[0:1] + ext[SUBLANES - 1:SUBLANES - 1 + n] * cw[1:2]
                     + h * cw[2:3])

    def up(r):
        xr = xn_scr[r * FF_ROWS:(r + 1) * FF_ROWS, :]
        return (jnp.dot(xr, wug_ref[...], preferred_element_type=F32),
                jnp.dot(xr, wuu_ref[...], preferred_element_type=F32))

    nxt = up(0)
    for r in range(tm // FF_ROWS):
        hg, hu = nxt
        if r + 1 < tm // FF_ROWS:
            nxt = up(r + 1)
        act = jax.nn.gelu(conv(hg, prev_g, cwg, cbg)) * conv(hu, prev_u, cwu, cbu)
        y_ref[r * FF_ROWS:(r + 1) * FF_ROWS, :] += _dot(act, wd_ref[...])
        prev_g, prev_u = hg[FF_ROWS - SUBLANES:], hu[FF_ROWS - SUBLANES:]
    carry_scr[j, :, 0:FF_CHUNK] = prev_g
    carry_scr[j, :, FF_CHUNK:] = prev_u
    cg_ref[0] = prev_g
    cu_ref[0] = prev_u

    if final_norm:
        @pl.when(j == pl.num_programs(1) - 1)
        def _():
            y_ref[...] = _rms(y_ref[...], fg_ref[...])


def _outproj_sample_kernel(x_ref, oa_ref, ob_ref, oc_ref, wo_ref, g2_ref, wug_ref, wuu_ref, cwg_ref, cwu_ref,
                           cbg_ref, cbu_ref, wd_ref, fg_ref, bufg_ref, bufu_ref,
                           y_ref, cg_ref, cu_ref, xn_scr, hs_scr, *, nseq, seq_len, final_norm):
    j = pl.program_id(1)
    tm = x_ref.shape[0]

    @pl.when(j == 0)
    def _():
        _residual_and_norm(x_ref, oa_ref, ob_ref, oc_ref[...], wo_ref, g2_ref, y_ref, xn_scr)

    xn = xn_scr[...]
    hg = jnp.dot(xn, wug_ref[...], preferred_element_type=F32)
    hu = jnp.dot(xn, wuu_ref[...], preferred_element_type=F32)
    lo, hi_ = SUBLANES, SUBLANES + seq_len
    hs_scr[:, lo:hi_, 0:FF_CHUNK] = hg.reshape(nseq, seq_len, FF_CHUNK)
    hs_scr[:, lo:hi_, FF_CHUNK:] = hu.reshape(nseq, seq_len, FF_CHUNK)
    hs_scr[:, lo - 2:lo, 0:FF_CHUNK] = bufg_ref[...]
    hs_scr[:, lo - 2:lo, FF_CHUNK:] = bufu_ref[...]
    cg_ref[...] = hs_scr[:, hi_ - SUBLANES:hi_, 0:FF_CHUNK]
    cu_ref[...] = hs_scr[:, hi_ - SUBLANES:hi_, FF_CHUNK:]

    cw = jnp.concatenate([cwg_ref[...], cwu_ref[...]], axis=1)
    cb = jnp.concatenate([cbg_ref[...], cbu_ref[...]], axis=1)
    hc = cb + (hs_scr[:, lo - 2:hi_ - 2, :] * cw[0:1] + hs_scr[:, lo - 1:hi_ - 1, :] * cw[1:2]
               + hs_scr[:, lo:hi_, :] * cw[2:3])
    hc = hc.reshape(tm, 2 * FF_CHUNK)
    act = jax.nn.gelu(hc[:, :FF_CHUNK]) * hc[:, FF_CHUNK:]
    y_ref[...] += _dot(act, wd_ref[...])

    if final_norm:
        @pl.when(j == pl.num_programs(1) - 1)
        def _():
            y_ref[...] = _rms(y_ref[...], fg_ref[...])


def _ffn_specs(w_out, g2, w_up, conv_w, conv_b, w_down, fg):
    full = lambda shp: pl.BlockSpec(shp, lambda i, j: (0,) * len(shp))
    specs = [full(w_out.shape), full((1, D_MODEL)),
             pl.BlockSpec((D_MODEL, FF_CHUNK), lambda i, j: (0, j)),
             pl.BlockSpec((D_MODEL, FF_CHUNK), lambda i, j: (0, N_FF + j)),
             pl.BlockSpec((3, FF_CHUNK), lambda i, j: (0, j)),
             pl.BlockSpec((3, FF_CHUNK), lambda i, j: (0, N_FF + j)),
             pl.BlockSpec((1, FF_CHUNK), lambda i, j: (0, j)),
             pl.BlockSpec((1, FF_CHUNK), lambda i, j: (0, N_FF + j)),
             pl.BlockSpec((FF_CHUNK, D_MODEL), lambda i, j: (j, 0)),
             full((1, D_MODEL))]
    return specs, [w_out, g2, w_up, w_up, conv_w, conv_w, conv_b, conv_b, w_down, fg]


def out_projection_prompt(x, oa, ob, o_list, st_list, dils, expand, ffw, *, tm, tiles_per_seq, final_norm):
    n = x.shape[0]
    nt = n // tm
    row = lambda w: pl.BlockSpec((tm, w), lambda i, j: (i, 0))
    perm = lambda w: [pl.BlockSpec((tm // d, d * w), lambda i, j: (i, 0)) for d in dils]
    wspecs, wargs = _ffn_specs(*ffw)
    conv_spec = pl.BlockSpec((1, SUBLANES, FF_CHUNK), lambda i, j: (i, 0, j))
    n_unp = sum(C_WIDTH // LANES + 1 for d in dils if d > 1)
    return pl.pallas_call(
        functools.partial(_outproj_prompt_kernel, tiles_per_seq=tiles_per_seq, dils=tuple(dils),
                          final_norm=final_norm),
        grid=(nt, N_FF),
        in_specs=[row(D_MODEL), row(A_WIDTH), row(B_WIDTH)] + perm(C_WIDTH) + perm(LANES)
                 + [pl.BlockSpec(expand.shape, lambda i, j: (0, 0))] + wspecs,
        out_specs=(row(D_MODEL), conv_spec, conv_spec),
        out_shape=(jax.ShapeDtypeStruct((n, D_MODEL), F32),
                   jax.ShapeDtypeStruct((nt, SUBLANES, D_FF), F32),
                   jax.ShapeDtypeStruct((nt, SUBLANES, D_FF), F32)),
        scratch_shapes=[pltpu.VMEM((tm, D_MODEL), BF16),
                        pltpu.VMEM((N_FF, SUBLANES, 2 * FF_CHUNK), F32),
                        pltpu.VMEM((n_unp, tm, LANES), F32)],
        compiler_params=_params("arbitrary", "arbitrary"),
        name="out_projection_prompt",
    )(x, oa, ob, *o_list, *st_list, expand, *wargs)


def out_projection_sample(x, oa, ob, oc, ffw, bufs, *, nseq, seq_len, final_norm):
    n = x.shape[0]
    row = lambda w: pl.BlockSpec((n, w), lambda i, j: (0, 0))
    wspecs, wargs = _ffn_specs(*ffw)
    conv_spec = pl.BlockSpec((nseq, SUBLANES, FF_CHUNK), lambda i, j: (0, 0, j))
    return pl.pallas_call(
        functools.partial(_outproj_sample_kernel, nseq=nseq, seq_len=seq_len, final_norm=final_norm),
        grid=(1, N_FF),
        in_specs=[row(D_MODEL), row(A_WIDTH), row(B_WIDTH), row(C_WIDTH)] + wspecs
                 + [pl.BlockSpec((nseq, 2, FF_CHUNK), lambda i, j: (0, 0, j)),
                    pl.BlockSpec((nseq, 2, FF_CHUNK), lambda i, j: (0, 0, N_FF + j))],
        out_specs=(row(D_MODEL), conv_spec, conv_spec),
        out_shape=(jax.ShapeDtypeStruct((n, D_MODEL), F32),
                   jax.ShapeDtypeStruct((nseq, SUBLANES, D_FF), F32),
                   jax.ShapeDtypeStruct((nseq, SUBLANES, D_FF), F32)),
        scratch_shapes=[pltpu.VMEM((n, D_MODEL), BF16),
                        pltpu.VMEM((nseq, SUBLANES + seq_len, 2 * FF_CHUNK), F32)],
        compiler_params=_params("arbitrary", "arbitrary"),
        name="out_projection_sample",
    )(x, oa, ob, oc, *wargs, bufs, bufs)


PROMPT_TM = 512
S5_SEGLEN = 64
ATTN_TQ = 512


def kernel(x_prompt, x_sample, cache_c_k, cache_c_v, state_ssm_re, state_ssm_im, state_ffn_conv, norm1_g, w_in, w_s, b_s, ssm_lam_re, ssm_lam_im, ssm_log_dt, ssm_b_re, ssm_b_im, ssm_c_re, ssm_c_im, ssm_d, w_glu, b_glu, w_out, norm2_g, w_up, conv_w, conv_b, w_down, final_g):
    bp, sp, _ = x_prompt.shape
    nseq, steps, _ = x_sample.shape
    ns = nseq * steps
    cw_p = min(MAX_WINDOW, sp)
    cw_s = cache_c_k.shape[2]

    rope_p = rope_tables(sp, 0)
    rope_s = tuple(jnp.tile(t, (nseq, 1)) for t in rope_tables(steps, PAST_LEN))
    bbr, bbi, pr, pi, ncim = s5_params(ssm_lam_re, ssm_lam_im, ssm_log_dt, ssm_b_re, ssm_b_im, ssm_c_im, S5_SEGLEN)

    w_in_b, w_out_b, w_up_b, w_down_b, w_glu_b = (w.astype(BF16) for w in (w_in, w_out, w_up, w_down, w_glu))
    expand = (jnp.arange(LANES)[:, None] == jnp.arange(C_WIDTH)[None, :] // HEAD_DIM).astype(BF16)
    bs_p = jnp.repeat(jnp.transpose(b_s, (0, 2, 1)), HEAD_DIM, axis=2)
    bs_s = jnp.tile(bs_p[:, :steps], (1, CHUNK // steps, 1))
    ws_s = jnp.tile(w_s[:, :, :steps, :steps], (1, 1, CHUNK // steps, CHUNK // steps))
    row1 = lambda a: a.reshape(1, -1)

    cache_kt = jnp.transpose(cache_c_k, (0, 1, 3, 4, 2))
    cache_vt = jnp.transpose(cache_c_v, (0, 1, 3, 4, 2))
    dils = tuple(d for _, d in DILATIONS)

    hp = x_prompt
    hs = x_sample.reshape(1, ns, D_MODEL)
    outs = [[] for _ in range(11)]
    for l in range(DEPTH):
        last = l == DEPTH - 1
        wb, wc, prl, pil = s5_layouts(bbr[l], bbi[l], ssm_c_re[l], ncim[l], pr[l], pi[l])
        s5w = (wb, wc, prl, pil, row1(ssm_d[l]), w_glu_b[l], row1(b_glu[l]))
        ffw = (w_out_b[l], row1(norm2_g[l]), w_up_b[l], conv_w[l], row1(conv_b[l]), w_down_b[l], row1(final_g))

        oa, _, ub, kf, vf, *qkv = in_projection(hp, row1(norm1_g[l]), w_in_b[l], w_s[l], bs_p[l], rope_p,
                                                tm=PROMPT_TM, seq_rows=CHUNK, tail_rows=cw_p, dils=dils)
        nd = len(dils)
        ob, hr, hi = s5_prompt(ub, *s5w, seglen=S5_SEGLEN)
        o_list, st_list = zip(*[attn_prompt(qkv[n], qkv[nd + n], qkv[2 * nd + n], dil, tq=ATTN_TQ)
                                for n, dil in enumerate(dils)])
        flat = lambda a: a.reshape(a.shape[0] * a.shape[1], a.shape[2])
        y, cg, cu = out_projection_prompt(flat(hp), flat(oa), flat(ob), [flat(o) for o in o_list],
                                          [flat(s) for s in st_list], dils, expand, ffw,
                                          tm=PROMPT_TM, tiles_per_seq=sp // PROMPT_TM, final_norm=last)
        hp = y.reshape(bp, sp, D_MODEL)
        outs[0].append(kf.reshape(bp, cw_p, C_HEADS, HEAD_DIM))
        outs[1].append(vf.reshape(bp, cw_p, C_HEADS, HEAD_DIM))
        outs[4].append(hr.reshape(bp, B_GROUPS, SSM_STATE))
        outs[5].append(hi.reshape(bp, B_GROUPS, SSM_STATE))
        seq_end = lambda a: a.reshape(bp, sp // PROMPT_TM, SUBLANES, D_FF)[:, -1, -2:]
        outs[8].append(jnp.concatenate([seq_end(cg), seq_end(cu)], -1))

        oa, vn, ub, kf, vf, q, k, v = in_projection(hs, row1(norm1_g[l]), w_in_b[l], ws_s[l], bs_s[l], rope_s,
                                                    tm=ns, seq_rows=steps, tail_rows=ns, dils=(1,))
        h0r = jnp.transpose(state_ssm_re[l].reshape(nseq, S5_BLOCKS, S5_HALF), (1, 0, 2))
        h0i = jnp.transpose(state_ssm_im[l].reshape(nseq, S5_BLOCKS, S5_HALF), (1, 0, 2))
        ob, hr, hi = s5_sample(ub[0], h0r, h0i, *s5w, nseq=nseq, steps=steps)
        oc = attn_sample(q[0].astype(F32), k[0].astype(F32), v[0].astype(F32), cache_kt, cache_vt, l, steps=steps)
        y, cg, cu = out_projection_sample(hs[0], oa[0], ob, oc, ffw, state_ffn_conv[l],
                                          nseq=nseq, seq_len=steps, final_norm=last)
        hs = y.reshape(1, ns, D_MODEL)
        outs[2].append(kf.reshape(nseq, steps, C_HEADS, HEAD_DIM))
        outs[3].append(vf.reshape(nseq, steps, C_HEADS, HEAD_DIM))
        outs[6].append(jnp.transpose(hr, (1, 0, 2)).reshape(nseq, B_GROUPS, SSM_STATE))
        outs[7].append(jnp.transpose(hi, (1, 0, 2)).reshape(nseq, B_GROUPS, SSM_STATE))
        outs[9].append(jnp.concatenate([cg[:, -2:], cu[:, -2:]], -1))
        outs[10].append(vn.reshape(nseq, steps, A_WIDTH))

    return (hp, hs.reshape(nseq, steps, D_MODEL)) + tuple(jnp.stack(o) for o in outs)
```

```python
import functools
import math

import jax
import jax.numpy as jnp
from jax import lax
from jax.experimental import pallas as pl
from jax.experimental.pallas import tpu as pltpu

F32 = jnp.float32
BF16 = jnp.bfloat16

D_MODEL = 1024
DEPTH = 4
PAST_LEN = 8192
HEAD_DIM = 64
A_WIDTH = 256
B_WIDTH = 384
C_WIDTH = 384
A_HEADS = 4
C_HEADS = 6
CHUNK = 128
SSM_GROUP = 16
B_GROUPS = 24
SSM_STATE = 64
DILATIONS = ((128, 1), (512, 4), (2048, 16))
MAX_WINDOW = 2048
ROPE_THETA = 500000.0
ROT_DIM = 16
D_FF = 2816
EPS = 1e-6
NEG_INF = -1e30

O_A, O_B, O_Q, O_K, O_V = 0, 2 * A_WIDTH, 896, 1280, 1664

LANES = 128
SUBLANES = 8
S5_BLOCKS = B_WIDTH // LANES
S5_HALF = 512
FF_CHUNK = 256
N_FF = D_FF // FF_CHUNK
VMEM_LIMIT = 56 * 1024 * 1024


def _dot(a, b):
    return jnp.dot(a.astype(BF16), b.astype(BF16), preferred_element_type=F32)


def _rms(x, g):
    return x * lax.rsqrt(jnp.mean(x * x, -1, keepdims=True) + EPS) * g


def _params(*sem):
    return pltpu.CompilerParams(dimension_semantics=sem, vmem_limit_bytes=VMEM_LIMIT)


def _rope_tables_kernel(c_ref, a_ref, b_ref, *, pos0):
    n = c_ref.shape[0]
    pos = (lax.broadcasted_iota(jnp.int32, (n, LANES), 0) + (pos0 + pl.program_id(0) * n)).astype(F32)
    d = lax.broadcasted_iota(jnp.int32, (n, LANES), 1) & (HEAD_DIM - 1)
    k = (d & (ROT_DIM // 2 - 1)).astype(F32)
    inv = jnp.exp(k * (-2.0 / ROT_DIM * math.log(ROPE_THETA)))
    ang = pos * inv
    cos, sin = jnp.cos(ang), jnp.sin(ang)
    c_ref[...] = jnp.where(d < ROT_DIM, cos, 1.0)
    a_ref[...] = jnp.where(d < ROT_DIM // 2, -sin, 0.0)
    b_ref[...] = jnp.where((d >= ROT_DIM // 2) & (d < ROT_DIM), sin, 0.0)


def rope_tables(n, pos0):
    sds = jax.ShapeDtypeStruct((n, LANES), F32)
    tr = min(n, 1024)
    spec = pl.BlockSpec((tr, LANES), lambda i: (i, 0))
    return pl.pallas_call(functools.partial(_rope_tables_kernel, pos0=pos0), grid=(n // tr,),
                          out_specs=(spec, spec, spec), out_shape=(sds, sds, sds),
                          compiler_params=_params("arbitrary"), name="rope_tables")()


def _s5_params_kernel(lr_ref, li_ref, ldt_ref, br_ref, bi_ref, cim_ref,
                      bbr_ref, bbi_ref, pr_ref, pi_ref, ncim_ref, *, n_pow):
    lr, li = lr_ref[...], li_ref[...]
    dt = jnp.exp(ldt_ref[...])
    mag = jnp.exp(lr * dt)
    ar, ai = mag * jnp.cos(li * dt), mag * jnp.sin(li * dt)
    nr, ni = ar - 1.0, ai
    den = lr * lr + li * li
    fr, fi = (nr * lr + ni * li) / den, (ni * lr - nr * li) / den
    br, bi = br_ref[...], bi_ref[...]
    bbr_ref[...] = fr[None] * br - fi[None] * bi
    bbi_ref[...] = fr[None] * bi + fi[None] * br
    k = (lax.broadcasted_iota(jnp.int32, (n_pow,) + lr.shape, 0) + 1).astype(F32)
    magk = jnp.exp(k * (lr * dt)[None])
    angk = k * (li * dt)[None]
    pr_ref[...] = magk * jnp.cos(angk)
    pi_ref[...] = magk * jnp.sin(angk)
    ncim_ref[...] = -cim_ref[...]


def s5_params(lam_re, lam_im, log_dt, b_re, b_im, c_im, n_pow):
    depth, g, p = lam_re.shape
    c = b_re.shape[-1]
    ldt = jnp.broadcast_to(log_dt[:, :, None], (depth, g, p))
    brt = jnp.transpose(b_re, (0, 3, 1, 2))
    bit = jnp.transpose(b_im, (0, 3, 1, 2))
    gp = pl.BlockSpec((None, g, p), lambda l: (l, 0, 0))
    cgp = pl.BlockSpec((None, c, g, p), lambda l: (l, 0, 0, 0))
    gcp = pl.BlockSpec((None, g, c, p), lambda l: (l, 0, 0, 0))
    kgp = pl.BlockSpec((None, n_pow, g, p), lambda l: (l, 0, 0, 0))
    return pl.pallas_call(
        functools.partial(_s5_params_kernel, n_pow=n_pow),
        grid=(depth,),
        in_specs=[gp, gp, gp, cgp, cgp, gcp],
        out_specs=(cgp, cgp, kgp, kgp, gcp),
        out_shape=(jax.ShapeDtypeStruct((depth, c, g, p), F32), jax.ShapeDtypeStruct((depth, c, g, p), F32),
                   jax.ShapeDtypeStruct((depth, n_pow, g, p), F32), jax.ShapeDtypeStruct((depth, n_pow, g, p), F32),
                   jax.ShapeDtypeStruct((depth, g, c, p), F32)),
        compiler_params=_params("arbitrary"),
        name="s5_params",
    )(lam_re, lam_im, ldt, brt, bit, c_im)


def _s5_lane_vec(t):
    return t.reshape(t.shape[:-2] + (S5_BLOCKS, 1, S5_HALF))


def s5_layouts(bbr, bbi, c_re, ncim, pr, pi):
    eye = jnp.eye(SUBLANES, dtype=F32)
    bb = jnp.stack([bbr, bbi], 0).reshape(2, SSM_GROUP, S5_BLOCKS, 8, SSM_STATE)
    wb = jnp.einsum('ecjgp,gh->jgcehp', bb, eye).reshape(S5_BLOCKS, LANES, 2 * S5_HALF)
    cc = jnp.stack([c_re, ncim], 0).reshape(2, S5_BLOCKS, 8, SSM_GROUP, SSM_STATE)
    wc = jnp.einsum('ejgcp,gh->jeghpc', cc, eye)
    wc = jnp.transpose(wc, (0, 1, 2, 4, 3, 5)).reshape(S5_BLOCKS, 2 * S5_HALF, LANES)
    n_pow = pr.shape[0]
    prl = jnp.transpose(pr.reshape(n_pow, S5_BLOCKS, S5_HALF), (1, 0, 2))
    pil = jnp.transpose(pi.reshape(n_pow, S5_BLOCKS, S5_HALF), (1, 0, 2))
    return wb.astype(BF16), wc.astype(BF16), prl, pil


def _rope_apply(x, c, a, b):
    return x * c + pltpu.roll(x, LANES - ROT_DIM // 2, 1) * a + pltpu.roll(x, ROT_DIM // 2, 1) * b


def _inproj_kernel(x_ref, g_ref, w_ref, ws_ref, bs_ref, rc_ref, ra_ref, rb_ref,
                   oa_ref, vn_ref, ub_ref, kf_ref, vf_ref, *rest, seq_rows, dils):
    qkv_refs, perm_scr = rest[:-1], rest[-1]
    tm = x_ref.shape[0]
    xn = _rms(x_ref[...], g_ref[...]).astype(BF16)

    h = jax.nn.gelu(jnp.dot(xn, w_ref[:, O_A:O_B], preferred_element_type=F32))
    u, v = h[:, :A_WIDTH], h[:, A_WIDTH:]
    mu = jnp.mean(v, -1, keepdims=True)
    var = jnp.mean(jnp.square(v - mu), -1, keepdims=True)
    vn = (v - mu) * lax.rsqrt(var + EPS)
    vn_ref[...] = vn
    ri = lax.broadcasted_iota(jnp.int32, (CHUNK, CHUNK), 0)
    ci = lax.broadcasted_iota(jnp.int32, (CHUNK, CHUNK), 1)
    keep = (ci <= ri) & ((ri // seq_rows) == (ci // seq_rows))
    lane = lax.broadcasted_iota(jnp.int32, (CHUNK, LANES), 1)
    wm = [jnp.where(keep, ws_ref[hh], 0.0).astype(BF16) for hh in range(A_HEADS)]
    vnb = vn.astype(BF16)
    for c in range(tm // CHUNK):
        rows = slice(c * CHUNK, (c + 1) * CHUNK)
        for p in range(A_WIDTH // LANES):
            cols = slice(p * LANES, (p + 1) * LANES)
            vp = vnb[rows, cols]
            m0 = jnp.dot(wm[2 * p], vp, preferred_element_type=F32)
            m1 = jnp.dot(wm[2 * p + 1], vp, preferred_element_type=F32)
            mixed = jnp.where(lane < HEAD_DIM, m0, m1) + bs_ref[:, cols]
            oa_ref[rows, cols] = u[rows, cols] * mixed

    ub_ref[...] = jnp.dot(xn, w_ref[:, O_B:O_Q], preferred_element_type=F32)

    rc, ra, rb = rc_ref[...], ra_ref[...], rb_ref[...]
    def emit(val, outs, slot, p):
        if len(dils) > 1:
            perm_scr[slot] = val
        for dil, ref in zip(dils, outs):
            if dil == 1:
                ref[:, p * LANES:(p + 1) * LANES] = val.astype(BF16)
                continue
            for r in range(dil):
                c0 = r * C_WIDTH + p * LANES
                ref[:, c0:c0 + LANES] = perm_scr[slot, pl.ds(r, tm // dil, stride=dil), :].astype(BF16)

    nd = len(dils)
    q_outs, k_outs, v_outs = qkv_refs[0:nd], qkv_refs[nd:2 * nd], qkv_refs[2 * nd:3 * nd]
    for p in range(C_WIDTH // LANES):
        cols = slice(p * LANES, (p + 1) * LANES)
        q = jnp.dot(xn, w_ref[:, O_Q + p * LANES:O_Q + (p + 1) * LANES], preferred_element_type=F32)
        emit(_rope_apply(q, rc, ra, rb) * (HEAD_DIM ** -0.5), q_outs, 3 * p, p)
        k = _rope_apply(jnp.dot(xn, w_ref[:, O_K + p * LANES:O_K + (p + 1) * LANES], preferred_element_type=F32),
                        rc, ra, rb)
        emit(k, k_outs, 3 * p + 1, p)
        kf_ref[:, cols] = k
        vv = jnp.dot(xn, w_ref[:, O_V + p * LANES:O_V + (p + 1) * LANES], preferred_element_type=F32)
        emit(vv, v_outs, 3 * p + 2, p)
        vf_ref[:, cols] = vv


def in_projection(x, g1, w_in, w_s, bs_rows, rope, *, tm, seq_rows, tail_rows, dils):
    bsz, s, _ = x.shape
    nt = s // tm
    tail_t = tail_rows // tm
    row = lambda w: pl.BlockSpec((None, tm, w), lambda b, t: (b, t, 0))
    tail = pl.BlockSpec((None, tm, C_WIDTH), lambda b, t: (b, jnp.maximum(t - (nt - tail_t), 0), 0))
    full = lambda shp: pl.BlockSpec(shp, lambda b, t: (0,) * len(shp))
    ropespec = pl.BlockSpec((tm, LANES), lambda b, t: (t, 0))
    sd = lambda w, dt: jax.ShapeDtypeStruct((bsz, s, w), dt)
    tail_sd = jax.ShapeDtypeStruct((bsz, tail_rows, C_WIDTH), F32)
    qkv_specs = [pl.BlockSpec((None, tm // d, d * C_WIDTH), lambda b, t: (b, t, 0)) for d in dils] * 3
    qkv_sds = [jax.ShapeDtypeStruct((bsz, s // d, d * C_WIDTH), BF16) for d in dils] * 3
    return pl.pallas_call(
        functools.partial(_inproj_kernel, seq_rows=seq_rows, dils=tuple(dils)),
        grid=(bsz, nt),
        in_specs=[row(D_MODEL), full((1, D_MODEL)), full(w_in.shape), full(w_s.shape), full(bs_rows.shape),
                  ropespec, ropespec, ropespec],
        out_specs=(row(A_WIDTH), row(A_WIDTH), row(B_WIDTH), tail, tail, *qkv_specs),
        out_shape=(sd(A_WIDTH, F32), sd(A_WIDTH, F32), sd(B_WIDTH, F32), tail_sd, tail_sd, *qkv_sds),
        scratch_shapes=[pltpu.VMEM((3 * (C_WIDTH // LANES), tm, LANES), F32)],
        compiler_params=_params("arbitrary", "arbitrary"),
        name="in_projection",
    )(x, g1, w_in, w_s, bs_rows, *rope)


def _s5_readout(hb, uperm, gs, wc_ref, d_ref, wg_ref, bg_ref):
    for j in range(S5_BLOCKS):
        cols = slice(j * LANES, (j + 1) * LANES)
        y = jnp.dot(hb[j], wc_ref[j], preferred_element_type=F32) + d_ref[:, cols] * uperm[:, cols]
        gs[:, cols] = jax.nn.gelu(y)
    g = gs[...]
    gs[...] = g * jax.nn.sigmoid(_dot(g, wg_ref[...]) + bg_ref[...])


def _s5_prompt_kernel(u_ref, wb_ref, wc_ref, pr_ref, pi_ref, d_ref, wg_ref, bg_ref,
                      o_ref, hr_ref, hi_ref,
                      upad, uperm, xs, hb, cs, carry, gs, *, seglen):
    pitch = seglen + SUBLANES

    @pl.when(pl.program_id(1) == 0)
    def _():
        carry[...] = jnp.zeros_like(carry)

    for s in range(SUBLANES):
        for p in range(S5_BLOCKS):
            upad[p, s * pitch:s * pitch + seglen, :] = u_ref[s * seglen:(s + 1) * seglen, p * LANES:(p + 1) * LANES]

    def perm(i, _):
        r = pl.multiple_of(i * SUBLANES, SUBLANES)
        for p in range(S5_BLOCKS):
            uperm[pl.ds(r, SUBLANES), p * LANES:(p + 1) * LANES] = upad[p, pl.ds(i, SUBLANES, stride=pitch), :]
        return 0
    lax.fori_loop(0, seglen, perm, 0)

    for j in range(S5_BLOCKS):
        xs[j] = _dot(uperm[:, j * LANES:(j + 1) * LANES], wb_ref[j])

    for j in range(S5_BLOCKS):
        ar = jnp.broadcast_to(pr_ref[j, 0:1, :], (SUBLANES, S5_HALF))
        ai = jnp.broadcast_to(pi_ref[j, 0:1, :], (SUBLANES, S5_HALF))

        def scan(i, hc, j=j, ar=ar, ai=ai):
            hr, hi = hc
            r = pl.multiple_of(i * SUBLANES, SUBLANES)
            nhr = ar * hr - ai * hi + xs[j, pl.ds(r, SUBLANES), 0:S5_HALF]
            nhi = ar * hi + ai * hr + xs[j, pl.ds(r, SUBLANES), S5_HALF:2 * S5_HALF]
            xs[j, pl.ds(r, SUBLANES), 0:S5_HALF] = nhr
            xs[j, pl.ds(r, SUBLANES), S5_HALF:2 * S5_HALF] = nhi
            return nhr, nhi
        z = jnp.zeros((SUBLANES, S5_HALF), F32)
        er, ei = lax.fori_loop(0, seglen, scan, (z, z))

        a64r, a64i = pr_ref[j, seglen - 1:seglen, :], pi_ref[j, seglen - 1:seglen, :]
        cr, ci = carry[j, 0, 0:1, :], carry[j, 1, 0:1, :]
        for s in range(SUBLANES):
            cs[j, 0, s:s + 1, :] = cr
            cs[j, 1, s:s + 1, :] = ci
            cr, ci = (a64r * cr - a64i * ci + er[s:s + 1, :], a64r * ci + a64i * cr + ei[s:s + 1, :])
        carry[j, 0, 0:1, :] = cr
        carry[j, 1, 0:1, :] = ci
        hr_ref[j] = cr
        hi_ref[j] = ci

        cr8, ci8 = cs[j, 0], cs[j, 1]

        def fix(i2, _, j=j, cr8=cr8, ci8=ci8):
            r = pl.multiple_of(i2 * 2 * SUBLANES, 2 * SUBLANES)
            hrs, his = [], []
            for t in range(2):
                i = i2 * 2 + t
                pr, pi = pr_ref[j, pl.ds(i, 1), :], pi_ref[j, pl.ds(i, 1), :]
                rows = pl.ds(r + t * SUBLANES, SUBLANES)
                hrs.append(xs[j, rows, 0:S5_HALF] + pr * cr8 - pi * ci8)
                his.append(xs[j, rows, S5_HALF:2 * S5_HALF] + pr * ci8 + pi * cr8)
            hb[j, pl.ds(r, 2 * SUBLANES), 0:S5_HALF] = jnp.concatenate(hrs, 0).astype(BF16)
            hb[j, pl.ds(r, 2 * SUBLANES), S5_HALF:2 * S5_HALF] = jnp.concatenate(his, 0).astype(BF16)
            return 0
        lax.fori_loop(0, seglen // 2, fix, 0)

    _s5_readout(hb, uperm, gs, wc_ref, d_ref, wg_ref, bg_ref)

    def unperm(i, _):
        r = pl.multiple_of(i * SUBLANES, SUBLANES)
        for p in range(S5_BLOCKS):
            upad[p, pl.ds(i, SUBLANES, stride=pitch), :] = gs[pl.ds(r, SUBLANES), p * LANES:(p + 1) * LANES]
        return 0
    lax.fori_loop(0, seglen, unperm, 0)
    for s in range(SUBLANES):
        for p in range(S5_BLOCKS):
            o_ref[s * seglen:(s + 1) * seglen, p * LANES:(p + 1) * LANES] = upad[p, s * pitch:s * pitch + seglen, :]


def s5_prompt(u, wb, wc, prl, pil, d_skip, w_glu, b_glu, *, seglen):
    bsz, s, _ = u.shape
    t = SUBLANES * seglen
    pitch = seglen + SUBLANES
    full = lambda shp: pl.BlockSpec(shp, lambda b, c: (0,) * len(shp))
    st = pl.BlockSpec((None, S5_BLOCKS, 1, S5_HALF), lambda b, c: (b, 0, 0, 0))
    rows = pl.BlockSpec((None, t, B_WIDTH), lambda b, c: (b, c, 0))
    return pl.pallas_call(
        functools.partial(_s5_prompt_kernel, seglen=seglen),
        grid=(bsz, s // t),
        in_specs=[rows, full(wb.shape), full(wc.shape), full(prl.shape), full(pil.shape),
                  full((1, B_WIDTH)), full(w_glu.shape), full((1, B_WIDTH))],
        out_specs=(rows, st, st),
        out_shape=(jax.ShapeDtypeStruct((bsz, s, B_WIDTH), F32),
                   jax.ShapeDtypeStruct((bsz, S5_BLOCKS, 1, S5_HALF), F32),
                   jax.ShapeDtypeStruct((bsz, S5_BLOCKS, 1, S5_HALF), F32)),
        scratch_shapes=[pltpu.VMEM((S5_BLOCKS, SUBLANES * pitch, LANES), F32),
                        pltpu.VMEM((t, B_WIDTH), F32),
                        pltpu.VMEM((S5_BLOCKS, t, 2 * S5_HALF), F32),
                        pltpu.VMEM((S5_BLOCKS, t, 2 * S5_HALF), BF16),
                        pltpu.VMEM((S5_BLOCKS, 2, SUBLANES, S5_HALF), F32),
                        pltpu.VMEM((S5_BLOCKS, 2, SUBLANES, S5_HALF), F32),
                        pltpu.VMEM((t, B_WIDTH), F32)],
        compiler_params=_params("arbitrary", "arbitrary"),
        name="s5_prompt",
    )(u, wb, wc, prl, pil, d_skip, w_glu, b_glu)


def _s5_sample_kernel(u_ref, h0r_ref, h0i_ref, wb_ref, wc_ref, pr_ref, pi_ref, d_ref, wg_ref, bg_ref,
                      o_ref, hr_ref, hi_ref,
                      upad, uperm, xs, hb, gs, *, nseq, steps):
    nblk = nseq // SUBLANES
    for p in range(S5_BLOCKS):
        upad[p] = u_ref[:, p * LANES:(p + 1) * LANES]
    for st in range(steps):
        for bb in range(nblk):
            r = (st * nblk + bb) * SUBLANES
            for p in range(S5_BLOCKS):
                uperm[r:r + SUBLANES, p * LANES:(p + 1) * LANES] = (
                    upad[p, pl.ds(bb * SUBLANES * steps + st, SUBLANES, stride=steps), :])
    for j in range(S5_BLOCKS):
        xs[j] = _dot(uperm[:, j * LANES:(j + 1) * LANES], wb_ref[j])
    for j in range(S5_BLOCKS):
        ar = jnp.broadcast_to(pr_ref[j, 0:1, :], (SUBLANES, S5_HALF))
        ai = jnp.broadcast_to(pi_ref[j, 0:1, :], (SUBLANES, S5_HALF))
        for bb in range(nblk):
            seqs = slice(bb * SUBLANES, (bb + 1) * SUBLANES)
            hr, hi = h0r_ref[j, seqs, :], h0i_ref[j, seqs, :]
            for st in range(steps):
                r = (st * nblk + bb) * SUBLANES
                hr, hi = (ar * hr - ai * hi + xs[j, r:r + SUBLANES, 0:S5_HALF],
                          ar * hi + ai * hr + xs[j, r:r + SUBLANES, S5_HALF:2 * S5_HALF])
                xs[j, r:r + SUBLANES, 0:S5_HALF] = hr
                xs[j, r:r + SUBLANES, S5_HALF:2 * S5_HALF] = hi
            hr_ref[j, seqs, :] = hr
            hi_ref[j, seqs, :] = hi
        hb[j] = xs[j].astype(BF16)
    _s5_readout(hb, uperm, gs, wc_ref, d_ref, wg_ref, bg_ref)
    for st in range(steps):
        for bb in range(nblk):
            r = (st * nblk + bb) * SUBLANES
            for p in range(S5_BLOCKS):
                upad[p, pl.ds(bb * SUBLANES * steps + st, SUBLANES, stride=steps), :] = (
                    gs[r:r + SUBLANES, p * LANES:(p + 1) * LANES])
    for p in range(S5_BLOCKS):
        o_ref[:, p * LANES:(p + 1) * LANES] = upad[p]


def s5_sample(u, h0r, h0i, wb, wc, prl, pil, d_skip, w_glu, b_glu, *, nseq, steps):
    n = nseq * steps
    st_sd = jax.ShapeDtypeStruct((S5_BLOCKS, nseq, S5_HALF), F32)
    return pl.pallas_call(
        functools.partial(_s5_sample_kernel, nseq=nseq, steps=steps),
        out_shape=(jax.ShapeDtypeStruct((n, B_WIDTH), F32), st_sd, st_sd),
        scratch_shapes=[pltpu.VMEM((S5_BLOCKS, n, LANES), F32),
                        pltpu.VMEM((n, B_WIDTH), F32),
                        pltpu.VMEM((S5_BLOCKS, n, 2 * S5_HALF), F32),
                        pltpu.VMEM((S5_BLOCKS, n, 2 * S5_HALF), BF16),
                        pltpu.VMEM((n, B_WIDTH), F32)],
        compiler_params=pltpu.CompilerParams(vmem_limit_bytes=VMEM_LIMIT),
        name="s5_sample",
    )(u, h0r, h0i, wb, wc, prl, pil, d_skip, w_glu, b_glu)


def _attn_prompt_kernel(q_ref, kp_ref, kc_ref, vp_ref, vc_ref, o_ref, st_ref, kx, vx):
    tq = q_ref.shape[0]
    t = pl.program_id(2)
    kx[0:CHUNK, :] = kp_ref[...]
    kx[CHUNK:CHUNK + tq, :] = kc_ref[...]
    vx[0:CHUNK, :] = vp_ref[...]
    vx[CHUNK:CHUNK + tq, :] = vc_ref[...]
    r = lax.broadcasted_iota(jnp.int32, (CHUNK, 2 * CHUNK), 0)
    c = lax.broadcasted_iota(jnp.int32, (CHUNK, 2 * CHUNK), 1)
    band = (c >= r) & (c <= r + CHUNK)
    lane = lax.broadcasted_iota(jnp.int32, (CHUNK, LANES), 1)
    lo_half = lane < HEAD_DIM
    for u in range(tq // CHUNK):
        rows = slice(u * CHUNK, (u + 1) * CHUNK)
        keys = slice(u * CHUNK, (u + 2) * CHUNK)
        if u == 0:
            valid = band & (c >= jnp.where(t == 0, CHUNK, 0))
        else:
            valid = band
        bias = jnp.where(valid, 0.0, NEG_INF)
        st = jnp.zeros((CHUNK, LANES), F32)
        for p in range(C_WIDTH // LANES):
            cols = slice(p * LANES, (p + 1) * LANES)
            qp = q_ref[rows, cols]
            kpair = kx[keys, cols]
            vpair = vx[keys, cols]
            outs = []
            for hh in range(2):
                qm = jnp.where(lo_half if hh == 0 else ~lo_half, qp, jnp.zeros_like(qp))
                sc = lax.dot_general(qm, kpair, (((1,), (1,)), ((), ())), preferred_element_type=F32) + bias
                m = jnp.max(sc, -1, keepdims=True)
                pe = jnp.exp(sc - m)
                l = jnp.sum(pe, -1, keepdims=True)
                outs.append(jnp.dot(pe.astype(BF16), vpair, preferred_element_type=F32) / l)
                st = jnp.where(lane == 2 * p + hh, m + jnp.log(l), st)
            o_ref[rows, cols] = jnp.where(lo_half, outs[0], outs[1])
        st_ref[rows, :] = st


def attn_prompt(q, k, v, dil, *, tq):
    bsz, ln, _ = q.shape
    tq = min(tq, ln)
    cur = pl.BlockSpec((None, tq, C_WIDTH), lambda b, r, t: (b, t, r))
    prev = pl.BlockSpec((None, CHUNK, C_WIDTH), lambda b, r, t: (b, jnp.maximum(t * (tq // CHUNK) - 1, 0), r))
    return pl.pallas_call(
        _attn_prompt_kernel,
        grid=(bsz, dil, ln // tq),
        in_specs=[cur, prev, cur, prev, cur],
        out_specs=(cur, pl.BlockSpec((None, tq, LANES), lambda b, r, t: (b, t, r))),
        out_shape=(jax.ShapeDtypeStruct((bsz, ln, dil * C_WIDTH), F32),
                   jax.ShapeDtypeStruct((bsz, ln, dil * LANES), F32)),
        scratch_shapes=[pltpu.VMEM((CHUNK + tq, C_WIDTH), BF16), pltpu.VMEM((CHUNK + tq, C_WIDTH), BF16)],
        compiler_params=_params("arbitrary", "arbitrary", "arbitrary"),
        name=f"attn_prompt_d{dil}",
    )(q, k, k, v, v)


def _attn_sample_kernel(q_ref, kn_ref, vn_ref, ck_ref, cv_ref, o_ref, *, steps):
    cw = ck_ref.shape[-1]

    def mult(delta):
        cnt = jnp.zeros(delta.shape, F32)
        for window, dil in DILATIONS:
            cnt = cnt + jnp.where((delta >= 0) & (delta <= window) & ((delta & (dil - 1)) == 0), 1.0, 0.0)
        return cnt

    cnt_c = mult(cw + lax.broadcasted_iota(jnp.int32, (steps, cw), 0)
                 - lax.broadcasted_iota(jnp.int32, (steps, cw), 1))
    cnt_n = mult(lax.broadcasted_iota(jnp.int32, (steps, CHUNK), 0)
                 - lax.broadcasted_iota(jnp.int32, (steps, CHUNK), 1))
    live_c, live_n = cnt_c > 0, cnt_n > 0
    zpad = jnp.zeros((CHUNK - steps, C_WIDTH), F32)
    kn = jnp.concatenate([kn_ref[...], zpad], 0).astype(BF16)
    vn = jnp.concatenate([vn_ref[...], zpad], 0).astype(BF16)
    q = q_ref[...].astype(BF16)
    nt_dims = (((1,), (1,)), ((), ()))
    for hh in range(C_HEADS):
        cols = slice(hh * HEAD_DIM, (hh + 1) * HEAD_DIM)
        qh = q[:, cols]
        sc = jnp.dot(qh, ck_ref[hh].astype(BF16), preferred_element_type=F32)
        sn = lax.dot_general(qh, kn[:, cols], nt_dims, preferred_element_type=F32)
        m = jnp.maximum(jnp.max(jnp.where(live_c, sc, NEG_INF), -1, keepdims=True),
                        jnp.max(jnp.where(live_n, sn, NEG_INF), -1, keepdims=True))
        ec = cnt_c * jnp.exp(jnp.where(live_c, sc - m, NEG_INF))
        en = cnt_n * jnp.exp(jnp.where(live_n, sn - m, NEG_INF))
        l = jnp.sum(ec, -1, keepdims=True) + jnp.sum(en, -1, keepdims=True)
        acc = (lax.dot_general(ec.astype(BF16), cv_ref[hh].astype(BF16), nt_dims, preferred_element_type=F32)
               + jnp.dot(en.astype(BF16), vn[:, cols], preferred_element_type=F32))
        o_ref[:, cols] = acc / l


def attn_sample(q, k_new, v_new, cache_kt, cache_vt, layer, *, steps):
    _, nseq, _, _, cw = cache_kt.shape
    rows = pl.BlockSpec((steps, C_WIDTH), lambda b: (b, 0))
    cache = pl.BlockSpec((None, None, C_HEADS, HEAD_DIM, cw), lambda b: (layer, b, 0, 0, 0))
    return pl.pallas_call(
        functools.partial(_attn_sample_kernel, steps=steps),
        grid=(nseq,),
        in_specs=[rows, rows, rows, cache, cache],
        out_specs=rows,
        out_shape=jax.ShapeDtypeStruct((nseq * steps, C_WIDTH), F32),
        compiler_params=_params("arbitrary"),
        name="attn_sample",
    )(q, k_new, v_new, cache_kt, cache_vt)


FF_ROWS = 128


def _residual_and_norm(x_ref, oa_ref, ob_ref, oc, wo_ref, g2_ref, y_ref, xn_scr):
    mix = (_dot(oa_ref[...], wo_ref[0:A_WIDTH, :]) + _dot(ob_ref[...], wo_ref[A_WIDTH:A_WIDTH + B_WIDTH, :])
           + _dot(oc, wo_ref[A_WIDTH + B_WIDTH:, :]))
    x1 = x_ref[...] + mix
    y_ref[...] = x1
    xn_scr[...] = _rms(x1, g2_ref[...]).astype(BF16)


def _outproj_prompt_kernel(x_ref, oa_ref, ob_ref, *rest, tiles_per_seq, dils, final_norm):
    nb = len(dils)
    o_refs, st_refs = rest[:nb], rest[nb:2 * nb]
    (e_ref, wo_ref, g2_ref, wup_ref, cw_ref, cb_ref, wd_ref, fg_ref,
     y_ref, cg_ref, cu_ref, xn_scr, carry_scr, unp_scr, hs_scr) = rest[2 * nb:]
    i = pl.program_id(0)
    tm = x_ref.shape[0]
    nslab = C_WIDTH // LANES

    outs, sts = [], []
    slot = 0
    for dil, o_ref, s_ref in zip(dils, o_refs, st_refs):
        if dil == 1:
            outs.append(o_ref[...])
            sts.append(s_ref[...])
            continue
        for r in range(dil):
            rows = pl.ds(r, tm // dil, stride=dil)
            for p in range(nslab):
                c0 = r * C_WIDTH + p * LANES
                unp_scr[slot + p, rows, :] = o_ref[:, c0:c0 + LANES]
            unp_scr[slot + nslab, rows, :] = s_ref[:, r * LANES:(r + 1) * LANES]
        outs.append(jnp.concatenate([unp_scr[slot + p] for p in range(nslab)], axis=1))
        sts.append(unp_scr[slot + nslab])
        slot += nslab + 1
    mx = functools.reduce(jnp.maximum, sts)
    ws = [jnp.exp(s - mx) for s in sts]
    tot = functools.reduce(lambda a, b: a + b, ws)
    oc = jnp.zeros((tm, C_WIDTH), F32)
    for w, o in zip(ws, outs):
        alpha = w / tot
        hi = alpha.astype(BF16)
        lo = (alpha - hi.astype(F32)).astype(BF16)
        wide = (jnp.dot(hi, e_ref[...], preferred_element_type=F32)
                + jnp.dot(lo, e_ref[...], preferred_element_type=F32))
        oc = oc + wide * o
    _residual_and_norm(x_ref, oa_ref, ob_ref, oc, wo_ref, g2_ref, y_ref, xn_scr)

    @pl.when((i % tiles_per_seq) == 0)
    def _():
        carry_scr[...] = jnp.zeros_like(carry_scr)

    nhs = 2 * FF_CHUNK // LANES
    gate = FF_CHUNK // LANES

    def up(jj, r):
        xr = xn_scr[r * FF_ROWS:(r + 1) * FF_ROWS, :]
        rows = slice(SUBLANES + r * FF_ROWS, SUBLANES + (r + 1) * FF_ROWS)
        for half in range(2):
            h = jnp.dot(xr, wup_ref[half * N_FF + jj], preferred_element_type=F32)
            for k in range(gate):
                hs_scr[2 * (half * gate + k), rows, :] = h[:, k * LANES:(k + 1) * LANES]

    def chunk(j, _):
        for k in range(nhs):
            hs_scr[2 * k, 0:SUBLANES, :] = carry_scr[j, k]
        cwj = jnp.concatenate([cw_ref[j], cw_ref[N_FF + j]], axis=1)
        cbj = jnp.concatenate([cb_ref[j], cb_ref[N_FF + j]], axis=1)
        slabs = lambda row: jnp.stack([row[:, k * LANES:(k + 1) * LANES] for k in range(nhs)])
        taps = [slabs(cwj[t:t + 1]) for t in range(3)]
        bias = slabs(cbj)

        def down(r):
            every_other = pl.ds(0, nhs, stride=2)
            shifted = lambda back: hs_scr[every_other, pl.ds(SUBLANES - back + r * FF_ROWS, FF_ROWS), :]
            hc = bias + (shifted(2) * taps[0] + shifted(1) * taps[1] + shifted(0) * taps[2])
            act = jax.nn.gelu(hc[:gate]) * hc[gate:]
            act = jnp.concatenate([act[k] for k in range(gate)], axis=1)
            y_ref[r * FF_ROWS:(r + 1) * FF_ROWS, :] += _dot(act, wd_ref[j])

        nchain = tm // FF_ROWS
        for r in range(nchain):
            if r + 1 < nchain:
                up(j, r + 1)
            else:
                up(jnp.minimum(j + 1, N_FF - 1), 0)
            down(r)
        for k in range(nhs):
            last = hs_scr[2 * k, tm:tm + SUBLANES, :]
            carry_scr[j, k] = last
            if k < gate:
                cg_ref[0, j, :, k * LANES:(k + 1) * LANES] = last
            else:
                cu_ref[0, j, :, (k - gate) * LANES:(k - gate + 1) * LANES] = last
        return 0

    up(0, 0)
    lax.fori_loop(0, N_FF, chunk, 0)

    if final_norm:
        y_ref[...] = _rms(y_ref[...], fg_ref[...])


def _outproj_sample_kernel(x_ref, oa_ref, ob_ref, oc_ref, wo_ref, g2_ref, wug_ref, wuu_ref, cwg_ref, cwu_ref,
                           cbg_ref, cbu_ref, wd_ref, fg_ref, bufg_ref, bufu_ref,
                           y_ref, cg_ref, cu_ref, xn_scr, hs_scr, *, nseq, seq_len, final_norm):
    j = pl.program_id(1)
    tm = x_ref.shape[0]

    @pl.when(j == 0)
    def _():
        _residual_and_norm(x_ref, oa_ref, ob_ref, oc_ref[...], wo_ref, g2_ref, y_ref, xn_scr)

    xn = xn_scr[...]
    hg = jnp.dot(xn, wug_ref[...], preferred_element_type=F32)
    hu = jnp.dot(xn, wuu_ref[...], preferred_element_type=F32)
    lo, hi_ = SUBLANES, SUBLANES + seq_len
    hs_scr[:, lo:hi_, 0:FF_CHUNK] = hg.reshape(nseq, seq_len, FF_CHUNK)
    hs_scr[:, lo:hi_, FF_CHUNK:] = hu.reshape(nseq, seq_len, FF_CHUNK)
    hs_scr[:, lo - 2:lo, 0:FF_CHUNK] = bufg_ref[...]
    hs_scr[:, lo - 2:lo, FF_CHUNK:] = bufu_ref[...]
    cg_ref[...] = hs_scr[:, hi_ - SUBLANES:hi_, 0:FF_CHUNK]
    cu_ref[...] = hs_scr[:, hi_ - SUBLANES:hi_, FF_CHUNK:]

    cw = jnp.concatenate([cwg_ref[...], cwu_ref[...]], axis=1)
    cb = jnp.concatenate([cbg_ref[...], cbu_ref[...]], axis=1)
    hc = cb + (hs_scr[:, lo - 2:hi_ - 2, :] * cw[0:1] + hs_scr[:, lo - 1:hi_ - 1, :] * cw[1:2]
               + hs_scr[:, lo:hi_, :] * cw[2:3])
    hc = hc.reshape(tm, 2 * FF_CHUNK)
    act = jax.nn.gelu(hc[:, :FF_CHUNK]) * hc[:, FF_CHUNK:]
    y_ref[...] += _dot(act, wd_ref[...])

    if final_norm:
        @pl.when(j == pl.num_programs(1) - 1)
        def _():
            y_ref[...] = _rms(y_ref[...], fg_ref[...])


def ffn_layouts(w_up, conv_w, conv_b, w_down):
    chunks = lambda a: jnp.transpose(a.reshape(a.shape[0], 2 * N_FF, FF_CHUNK), (1, 0, 2))
    return (chunks(w_up), chunks(conv_w), chunks(conv_b.reshape(1, -1)), w_down.reshape(N_FF, FF_CHUNK, D_MODEL))


def out_projection_prompt(x, oa, ob, o_list, st_list, dils, expand, ffw, *, tm, tiles_per_seq, final_norm):
    n = x.shape[0]
    nt = n // tm
    row = lambda w: pl.BlockSpec((tm, w), lambda i: (i, 0))
    perm = lambda w: [pl.BlockSpec((tm // d, d * w), lambda i: (i, 0)) for d in dils]
    resident = lambda a: pl.BlockSpec(a.shape, lambda i: (0,) * a.ndim, pipeline_mode=pl.Buffered(1))
    conv_spec = pl.BlockSpec((1, N_FF, SUBLANES, FF_CHUNK), lambda i: (i, 0, 0, 0))
    conv_sd = jax.ShapeDtypeStruct((nt, N_FF, SUBLANES, FF_CHUNK), F32)
    n_unp = sum(C_WIDTH // LANES + 1 for d in dils if d > 1)
    nhs = 2 * FF_CHUNK // LANES
    return pl.pallas_call(
        functools.partial(_outproj_prompt_kernel, tiles_per_seq=tiles_per_seq, dils=tuple(dils),
                          final_norm=final_norm),
        grid=(nt,),
        in_specs=[row(D_MODEL), row(A_WIDTH), row(B_WIDTH)] + perm(C_WIDTH) + perm(LANES)
                 + [resident(expand)] + [resident(a) for a in ffw],
        out_specs=(row(D_MODEL), conv_spec, conv_spec),
        out_shape=(jax.ShapeDtypeStruct((n, D_MODEL), F32), conv_sd, conv_sd),
        scratch_shapes=[pltpu.VMEM((tm, D_MODEL), BF16),
                        pltpu.VMEM((N_FF, nhs, SUBLANES, LANES), F32),
                        pltpu.VMEM((n_unp, tm, LANES), F32),
                        pltpu.VMEM((2 * nhs, SUBLANES + tm, LANES), F32)],
        compiler_params=_params("arbitrary"),
        name="out_projection_prompt",
    )(x, oa, ob, *o_list, *st_list, expand, *ffw)


def out_projection_sample(x, oa, ob, oc, ffw, bufs, *, nseq, seq_len, final_norm):
    n = x.shape[0]
    row = lambda w: pl.BlockSpec((n, w), lambda i, j: (0, 0))
    full = lambda a: pl.BlockSpec(a.shape, lambda i, j: (0,) * a.ndim)
    w_out, g2, w_up, conv_w, conv_b, w_down, fg = ffw
    chunk = lambda a, off: pl.BlockSpec((None,) + a.shape[1:], lambda i, j: (off + j, 0, 0))
    wspecs = [full(w_out), full(g2), chunk(w_up, 0), chunk(w_up, N_FF), chunk(conv_w, 0), chunk(conv_w, N_FF),
              chunk(conv_b, 0), chunk(conv_b, N_FF), chunk(w_down, 0), full(fg)]
    wargs = [w_out, g2, w_up, w_up, conv_w, conv_w, conv_b, conv_b, w_down, fg]
    conv_spec = pl.BlockSpec((nseq, SUBLANES, FF_CHUNK), lambda i, j: (0, 0, j))
    return pl.pallas_call(
        functools.partial(_outproj_sample_kernel, nseq=nseq, seq_len=seq_len, final_norm=final_norm),
        grid=(1, N_FF),
        in_specs=[row(D_MODEL), row(A_WIDTH), row(B_WIDTH), row(C_WIDTH)] + wspecs
                 + [pl.BlockSpec((nseq, 2, FF_CHUNK), lambda i, j: (0, 0, j)),
                    pl.BlockSpec((nseq, 2, FF_CHUNK), lambda i, j: (0, 0, N_FF + j))],
        out_specs=(row(D_MODEL), conv_spec, conv_spec),
        out_shape=(jax.ShapeDtypeStruct((n, D_MODEL), F32),
                   jax.ShapeDtypeStruct((nseq, SUBLANES, D_FF), F32),
                   jax.ShapeDtypeStruct((nseq, SUBLANES, D_FF), F32)),
        scratch_shapes=[pltpu.VMEM((n, D_MODEL), BF16),
                        pltpu.VMEM((nseq, SUBLANES + seq_len, 2 * FF_CHUNK), F32)],
        compiler_params=_params("arbitrary", "arbitrary"),
        name="out_projection_sample",
    )(x, oa, ob, oc, *wargs, bufs, bufs)


PROMPT_TM = 512
S5_SEGLEN = 64
ATTN_TQ = 512


def kernel(x_prompt, x_sample, cache_c_k, cache_c_v, state_ssm_re, state_ssm_im, state_ffn_conv, norm1_g, w_in, w_s, b_s, ssm_lam_re, ssm_lam_im, ssm_log_dt, ssm_b_re, ssm_b_im, ssm_c_re, ssm_c_im, ssm_d, w_glu, b_glu, w_out, norm2_g, w_up, conv_w, conv_b, w_down, final_g):
    bp, sp, _ = x_prompt.shape
    nseq, steps, _ = x_sample.shape
    ns = nseq * steps
    cw_p = min(MAX_WINDOW, sp)
    cw_s = cache_c_k.shape[2]

    rope_p = rope_tables(sp, 0)
    rope_s = tuple(jnp.tile(t, (nseq, 1)) for t in rope_tables(steps, PAST_LEN))
    bbr, bbi, pr, pi, ncim = s5_params(ssm_lam_re, ssm_lam_im, ssm_log_dt, ssm_b_re, ssm_b_im, ssm_c_im, S5_SEGLEN)

    w_in_b, w_out_b, w_up_b, w_down_b, w_glu_b = (w.astype(BF16) for w in (w_in, w_out, w_up, w_down, w_glu))
    expand = (jnp.arange(LANES)[:, None] == jnp.arange(C_WIDTH)[None, :] // HEAD_DIM).astype(BF16)
    bs_p = jnp.repeat(jnp.transpose(b_s, (0, 2, 1)), HEAD_DIM, axis=2)
    bs_s = jnp.tile(bs_p[:, :steps], (1, CHUNK // steps, 1))
    ws_s = jnp.tile(w_s[:, :, :steps, :steps], (1, 1, CHUNK // steps, CHUNK // steps))
    row1 = lambda a: a.reshape(1, -1)

    cache_kt = jnp.transpose(cache_c_k, (0, 1, 3, 4, 2))
    cache_vt = jnp.transpose(cache_c_v, (0, 1, 3, 4, 2))
    dils = tuple(d for _, d in DILATIONS)

    hp = x_prompt
    hs = x_sample.reshape(1, ns, D_MODEL)
    outs = [[] for _ in range(11)]
    for l in range(DEPTH):
        last = l == DEPTH - 1
        wb, wc, prl, pil = s5_layouts(bbr[l], bbi[l], ssm_c_re[l], ncim[l], pr[l], pi[l])
        s5w = (wb, wc, prl, pil, row1(ssm_d[l]), w_glu_b[l], row1(b_glu[l]))
        ffw = (w_out_b[l], row1(norm2_g[l]), *ffn_layouts(w_up_b[l], conv_w[l], conv_b[l], w_down_b[l]),
               row1(final_g))

        oa, _, ub, kf, vf, *qkv = in_projection(hp, row1(norm1_g[l]), w_in_b[l], w_s[l], bs_p[l], rope_p,
                                                tm=PROMPT_TM, seq_rows=CHUNK, tail_rows=cw_p, dils=dils)
        nd = len(dils)
        ob, hr, hi = s5_prompt(ub, *s5w, seglen=S5_SEGLEN)
        o_list, st_list = zip(*[attn_prompt(qkv[n], qkv[nd + n], qkv[2 * nd + n], dil, tq=ATTN_TQ)
                                for n, dil in enumerate(dils)])
        flat = lambda a: a.reshape(a.shape[0] * a.shape[1], a.shape[2])
        y, cg, cu = out_projection_prompt(flat(hp), flat(oa), flat(ob), [flat(o) for o in o_list],
                                          [flat(s) for s in st_list], dils, expand, ffw,
                                          tm=PROMPT_TM, tiles_per_seq=sp // PROMPT_TM, final_norm=last)
        hp = y.reshape(bp, sp, D_MODEL)
        outs[0].append(kf.reshape(bp, cw_p, C_HEADS, HEAD_DIM))
        outs[1].append(vf.reshape(bp, cw_p, C_HEADS, HEAD_DIM))
        outs[4].append(hr.reshape(bp, B_GROUPS, SSM_STATE))
        outs[5].append(hi.reshape(bp, B_GROUPS, SSM_STATE))
        seq_end = lambda a: jnp.transpose(a.reshape(bp, sp // PROMPT_TM, N_FF, SUBLANES, FF_CHUNK)[:, -1, :, -2:],
                                          (0, 2, 1, 3)).reshape(bp, 2, D_FF)
        outs[8].append(jnp.concatenate([seq_end(cg), seq_end(cu)], -1))

        oa, vn, ub, kf, vf, q, k, v = in_projection(hs, row1(norm1_g[l]), w_in_b[l], ws_s[l], bs_s[l], rope_s,
                                                    tm=ns, seq_rows=steps, tail_rows=ns, dils=(1,))
        h0r = jnp.transpose(state_ssm_re[l].reshape(nseq, S5_BLOCKS, S5_HALF), (1, 0, 2))
        h0i = jnp.transpose(state_ssm_im[l].reshape(nseq, S5_BLOCKS, S5_HALF), (1, 0, 2))
        ob, hr, hi = s5_sample(ub[0], h0r, h0i, *s5w, nseq=nseq, steps=steps)
        oc = attn_sample(q[0].astype(F32), k[0].astype(F32), v[0].astype(F32), cache_kt, cache_vt, l, steps=steps)
        y, cg, cu = out_projection_sample(hs[0], oa[0], ob, oc, ffw, state_ffn_conv[l],
                                          nseq=nseq, seq_len=steps, final_norm=last)
        hs = y.reshape(1, ns, D_MODEL)
        outs[2].append(kf.reshape(nseq, steps, C_HEADS, HEAD_DIM))
        outs[3].append(vf.reshape(nseq, steps, C_HEADS, HEAD_DIM))
        outs[6].append(jnp.transpose(hr, (1, 0, 2)).reshape(nseq, B_GROUPS, SSM_STATE))
        outs[7].append(jnp.transpose(hi, (1, 0, 2)).reshape(nseq, B_GROUPS, SSM_STATE))
        outs[9].append(jnp.concatenate([cg[:, -2:], cu[:, -2:]], -1))
        outs[10].append(vn.reshape(nseq, steps, A_WIDTH))

    return (hp, hs.reshape(nseq, steps, D_MODEL)) + tuple(jnp.stack(o) for o in outs)
```

```python
import functools
import math

import jax
import jax.numpy as jnp
from jax import lax
from jax.experimental import pallas as pl
from jax.experimental.pallas import tpu as pltpu

F32 = jnp.float32
BF16 = jnp.bfloat16

D_MODEL = 1024
DEPTH = 4
PAST_LEN = 8192
HEAD_DIM = 64
A_WIDTH = 256
B_WIDTH = 384
C_WIDTH = 384
A_HEADS = 4
C_HEADS = 6
CHUNK = 128
SSM_GROUP = 16
B_GROUPS = 24
SSM_STATE = 64
DILATIONS = ((128, 1), (512, 4), (2048, 16))
MAX_WINDOW = 2048
ROPE_THETA = 500000.0
ROT_DIM = 16
D_FF = 2816
EPS = 1e-6
NEG_INF = -1e30

O_A, O_B, O_Q, O_K, O_V = 0, 2 * A_WIDTH, 896, 1280, 1664

LANES = 128
SUBLANES = 8
S5_BLOCKS = B_WIDTH // LANES
S5_HALF = 512
FF_CHUNK = 256
N_FF = D_FF // FF_CHUNK
VMEM_LIMIT = 56 * 1024 * 1024


def _dot(a, b):
    return jnp.dot(a.astype(BF16), b.astype(BF16), preferred_element_type=F32)


def _rms(x, g):
    return x * lax.rsqrt(jnp.mean(x * x, -1, keepdims=True) + EPS) * g


def _params(*sem):
    return pltpu.CompilerParams(dimension_semantics=sem, vmem_limit_bytes=VMEM_LIMIT)


def _rope_tables_kernel(c_ref, a_ref, b_ref, *, pos0):
    n = c_ref.shape[0]
    pos = (lax.broadcasted_iota(jnp.int32, (n, LANES), 0) + (pos0 + pl.program_id(0) * n)).astype(F32)
    d = lax.broadcasted_iota(jnp.int32, (n, LANES), 1) & (HEAD_DIM - 1)
    k = (d & (ROT_DIM // 2 - 1)).astype(F32)
    inv = jnp.exp(k * (-2.0 / ROT_DIM * math.log(ROPE_THETA)))
    ang = pos * inv
    cos, sin = jnp.cos(ang), jnp.sin(ang)
    c_ref[...] = jnp.where(d < ROT_DIM, cos, 1.0)
    a_ref[...] = jnp.where(d < ROT_DIM // 2, -sin, 0.0)
    b_ref[...] = jnp.where((d >= ROT_DIM // 2) & (d < ROT_DIM), sin, 0.0)


def rope_tables(n, pos0):
    sds = jax.ShapeDtypeStruct((n, LANES), F32)
    tr = min(n, 1024)
    spec = pl.BlockSpec((tr, LANES), lambda i: (i, 0))
    return pl.pallas_call(functools.partial(_rope_tables_kernel, pos0=pos0), grid=(n // tr,),
                          out_specs=(spec, spec, spec), out_shape=(sds, sds, sds),
                          compiler_params=_params("arbitrary"), name="rope_tables")()


def _s5_params_kernel(lr_ref, li_ref, ldt_ref, br_ref, bi_ref, cim_ref,
                      bbr_ref, bbi_ref, pr_ref, pi_ref, ncim_ref, *, n_pow):
    lr, li = lr_ref[...], li_ref[...]
    dt = jnp.exp(ldt_ref[...])
    mag = jnp.exp(lr * dt)
    ar, ai = mag * jnp.cos(li * dt), mag * jnp.sin(li * dt)
    nr, ni = ar - 1.0, ai
    den = lr * lr + li * li
    fr, fi = (nr * lr + ni * li) / den, (ni * lr - nr * li) / den
    br, bi = br_ref[...], bi_ref[...]
    bbr_ref[...] = fr[None] * br - fi[None] * bi
    bbi_ref[...] = fr[None] * bi + fi[None] * br
    k = (lax.broadcasted_iota(jnp.int32, (n_pow,) + lr.shape, 0) + 1).astype(F32)
    magk = jnp.exp(k * (lr * dt)[None])
    angk = k * (li * dt)[None]
    pr_ref[...] = magk * jnp.cos(angk)
    pi_ref[...] = magk * jnp.sin(angk)
    ncim_ref[...] = -cim_ref[...]


def s5_params(lam_re, lam_im, log_dt, b_re, b_im, c_im, n_pow):
    depth, g, p = lam_re.shape
    c = b_re.shape[-1]
    ldt = jnp.broadcast_to(log_dt[:, :, None], (depth, g, p))
    brt = jnp.transpose(b_re, (0, 3, 1, 2))
    bit = jnp.transpose(b_im, (0, 3, 1, 2))
    gp = pl.BlockSpec((None, g, p), lambda l: (l, 0, 0))
    cgp = pl.BlockSpec((None, c, g, p), lambda l: (l, 0, 0, 0))
    gcp = pl.BlockSpec((None, g, c, p), lambda l: (l, 0, 0, 0))
    kgp = pl.BlockSpec((None, n_pow, g, p), lambda l: (l, 0, 0, 0))
    return pl.pallas_call(
        functools.partial(_s5_params_kernel, n_pow=n_pow),
        grid=(depth,),
        in_specs=[gp, gp, gp, cgp, cgp, gcp],
        out_specs=(cgp, cgp, kgp, kgp, gcp),
        out_shape=(jax.ShapeDtypeStruct((depth, c, g, p), F32), jax.ShapeDtypeStruct((depth, c, g, p), F32),
                   jax.ShapeDtypeStruct((depth, n_pow, g, p), F32), jax.ShapeDtypeStruct((depth, n_pow, g, p), F32),
                   jax.ShapeDtypeStruct((depth, g, c, p), F32)),
        compiler_params=_params("arbitrary"),
        name="s5_params",
    )(lam_re, lam_im, ldt, brt, bit, c_im)


def _s5_lane_vec(t):
    return t.reshape(t.shape[:-2] + (S5_BLOCKS, 1, S5_HALF))


def s5_layouts(bbr, bbi, c_re, ncim, pr, pi):
    eye = jnp.eye(SUBLANES, dtype=F32)
    bb = jnp.stack([bbr, bbi], 0).reshape(2, SSM_GROUP, S5_BLOCKS, 8, SSM_STATE)
    wb = jnp.einsum('ecjgp,gh->jgcehp', bb, eye).reshape(S5_BLOCKS, LANES, 2 * S5_HALF)
    cc = jnp.stack([c_re, ncim], 0).reshape(2, S5_BLOCKS, 8, SSM_GROUP, SSM_STATE)
    wc = jnp.einsum('ejgcp,gh->jeghpc', cc, eye)
    wc = jnp.transpose(wc, (0, 1, 2, 4, 3, 5)).reshape(S5_BLOCKS, 2 * S5_HALF, LANES)
    n_pow = pr.shape[0]
    prl = jnp.transpose(pr.reshape(n_pow, S5_BLOCKS, S5_HALF), (1, 0, 2))
    pil = jnp.transpose(pi.reshape(n_pow, S5_BLOCKS, S5_HALF), (1, 0, 2))
    return wb.astype(BF16), wc.astype(BF16), prl, pil


def _rope_apply(x, c, a, b):
    return x * c + pltpu.roll(x, LANES - ROT_DIM // 2, 1) * a + pltpu.roll(x, ROT_DIM // 2, 1) * b


def _inproj_kernel(x_ref, g_ref, w_ref, ws_ref, bs_ref, rc_ref, ra_ref, rb_ref,
                   oa_ref, vn_ref, ub_ref, kf_ref, vf_ref, *rest, seq_rows, dils):
    qkv_refs, proj_scr, perm_scr, perm2_scr = rest[:-3], rest[-3], rest[-2], rest[-1]
    tm = x_ref.shape[0]
    xn = _rms(x_ref[...], g_ref[...]).astype(BF16)
    proj_scr[...] = jnp.dot(xn, w_ref[...], preferred_element_type=F32)

    h = jax.nn.gelu(proj_scr[:, O_A:O_B])
    u, v = h[:, :A_WIDTH], h[:, A_WIDTH:]
    mu = jnp.mean(v, -1, keepdims=True)
    var = jnp.mean(jnp.square(v - mu), -1, keepdims=True)
    vn = (v - mu) * lax.rsqrt(var + EPS)
    vn_ref[...] = vn
    ri = lax.broadcasted_iota(jnp.int32, (CHUNK, CHUNK), 0)
    ci = lax.broadcasted_iota(jnp.int32, (CHUNK, CHUNK), 1)
    keep = (ci <= ri) & ((ri // seq_rows) == (ci // seq_rows))
    lane = lax.broadcasted_iota(jnp.int32, (CHUNK, LANES), 1)
    wm = [jnp.where(keep, ws_ref[hh], 0.0).astype(BF16) for hh in range(A_HEADS)]
    vnb = vn.astype(BF16)
    for c in range(tm // CHUNK):
        rows = slice(c * CHUNK, (c + 1) * CHUNK)
        for p in range(A_WIDTH // LANES):
            cols = slice(p * LANES, (p + 1) * LANES)
            vp = vnb[rows, cols]
            m0 = jnp.dot(wm[2 * p], vp, preferred_element_type=F32)
            m1 = jnp.dot(wm[2 * p + 1], vp, preferred_element_type=F32)
            mixed = jnp.where(lane < HEAD_DIM, m0, m1) + bs_ref[:, cols]
            oa_ref[rows, cols] = u[rows, cols] * mixed

    ub_ref[...] = proj_scr[:, O_B:O_Q]

    rc, ra, rb = rc_ref[...], ra_ref[...], rb_ref[...]
    def emit(val, outs, slot, p):
        outs[0][:, p * LANES:(p + 1) * LANES] = val.astype(BF16)
        if len(dils) == 1:
            return
        perm_scr[slot] = val
        src, dst, prev = perm_scr, perm2_scr, 1
        for n in range(1, len(dils)):
            dil = dils[n]
            f = dil // prev
            for c in range(prev):
                for s in range(f):
                    cls = c + prev * s
                    blk = src[slot, pl.ds(c * (tm // prev) + s, tm // dil, stride=f), :]
                    c0 = cls * C_WIDTH + p * LANES
                    outs[n][:, c0:c0 + LANES] = blk.astype(BF16)
                    if n + 1 < len(dils):
                        dst[slot, cls * (tm // dil):(cls + 1) * (tm // dil), :] = blk
            src, dst, prev = dst, src, dil

    nd = len(dils)
    assert dils[0] == 1
    q_outs, k_outs, v_outs = qkv_refs[0:nd], qkv_refs[nd:2 * nd], qkv_refs[2 * nd:3 * nd]
    for p in range(C_WIDTH // LANES):
        cols = slice(p * LANES, (p + 1) * LANES)
        q = proj_scr[:, O_Q + p * LANES:O_Q + (p + 1) * LANES]
        emit(_rope_apply(q, rc, ra, rb) * (HEAD_DIM ** -0.5), q_outs, 3 * p, p)
        k = _rope_apply(proj_scr[:, O_K + p * LANES:O_K + (p + 1) * LANES], rc, ra, rb)
        emit(k, k_outs, 3 * p + 1, p)
        kf_ref[:, cols] = k
        vv = proj_scr[:, O_V + p * LANES:O_V + (p + 1) * LANES]
        emit(vv, v_outs, 3 * p + 2, p)
        vf_ref[:, cols] = vv


def in_projection(x, g1, w_in, w_s, bs_rows, rope, *, tm, seq_rows, tail_rows, dils):
    bsz, s, _ = x.shape
    nt = s // tm
    tail_t = tail_rows // tm
    row = lambda w: pl.BlockSpec((None, tm, w), lambda b, t: (b, t, 0))
    tail = pl.BlockSpec((None, tm, C_WIDTH), lambda b, t: (b, jnp.maximum(t - (nt - tail_t), 0), 0))
    full = lambda shp: pl.BlockSpec(shp, lambda b, t: (0,) * len(shp))
    ropespec = pl.BlockSpec((tm, LANES), lambda b, t: (t, 0))
    sd = lambda w, dt: jax.ShapeDtypeStruct((bsz, s, w), dt)
    tail_sd = jax.ShapeDtypeStruct((bsz, tail_rows, C_WIDTH), F32)
    qkv_specs = [pl.BlockSpec((None, tm // d, d * C_WIDTH), lambda b, t: (b, t, 0)) for d in dils] * 3
    qkv_sds = [jax.ShapeDtypeStruct((bsz, s // d, d * C_WIDTH), BF16) for d in dils] * 3
    return pl.pallas_call(
        functools.partial(_inproj_kernel, seq_rows=seq_rows, dils=tuple(dils)),
        grid=(bsz, nt),
        in_specs=[row(D_MODEL), full((1, D_MODEL)), full(w_in.shape), full(w_s.shape), full(bs_rows.shape),
                  ropespec, ropespec, ropespec],
        out_specs=(row(A_WIDTH), row(A_WIDTH), row(B_WIDTH), tail, tail, *qkv_specs),
        out_shape=(sd(A_WIDTH, F32), sd(A_WIDTH, F32), sd(B_WIDTH, F32), tail_sd, tail_sd, *qkv_sds),
        scratch_shapes=[pltpu.VMEM((tm, w_in.shape[1]), F32),
                        pltpu.VMEM((3 * (C_WIDTH // LANES), tm, LANES), F32),
                        pltpu.VMEM((3 * (C_WIDTH // LANES), tm, LANES), F32)],
        compiler_params=_params("arbitrary", "arbitrary"),
        name="in_projection",
    )(x, g1, w_in, w_s, bs_rows, *rope)


def _s5_readout(hb, uperm, gs, wc_ref, d_ref, wg_ref, bg_ref):
    for j in range(S5_BLOCKS):
        cols = slice(j * LANES, (j + 1) * LANES)
        y = jnp.dot(hb[j], wc_ref[j], preferred_element_type=F32) + d_ref[:, cols] * uperm[:, cols]
        gs[:, cols] = jax.nn.gelu(y)
    g = gs[...]
    gs[...] = g * jax.nn.sigmoid(_dot(g, wg_ref[...]) + bg_ref[...])


def _s5_prompt_kernel(u_ref, wb_ref, wc_ref, pr_ref, pi_ref, d_ref, wg_ref, bg_ref,
                      o_ref, hr_ref, hi_ref,
                      upad, uperm, xs, hb, cs, carry, gs, *, seglen):
    pitch = seglen + SUBLANES

    @pl.when(pl.program_id(1) == 0)
    def _():
        carry[...] = jnp.zeros_like(carry)

    for s in range(SUBLANES):
        for p in range(S5_BLOCKS):
            upad[p, s * pitch:s * pitch + seglen, :] = u_ref[s * seglen:(s + 1) * seglen, p * LANES:(p + 1) * LANES]

    def perm(i, _):
        r = pl.multiple_of(i * SUBLANES, SUBLANES)
        for p in range(S5_BLOCKS):
            uperm[pl.ds(r, SUBLANES), p * LANES:(p + 1) * LANES] = upad[p, pl.ds(i, SUBLANES, stride=pitch), :]
        return 0
    lax.fori_loop(0, seglen, perm, 0)

    for j in range(S5_BLOCKS):
        xs[j] = _dot(uperm[:, j * LANES:(j + 1) * LANES], wb_ref[j])

    for j in range(S5_BLOCKS):
        ar = jnp.broadcast_to(pr_ref[j, 0:1, :], (SUBLANES, S5_HALF))
        ai = jnp.broadcast_to(pi_ref[j, 0:1, :], (SUBLANES, S5_HALF))

        def scan(i, hc, j=j, ar=ar, ai=ai):
            hr, hi = hc
            r = pl.multiple_of(i * SUBLANES, SUBLANES)
            nhr = ar * hr - ai * hi + xs[j, pl.ds(r, SUBLANES), 0:S5_HALF]
            nhi = ar * hi + ai * hr + xs[j, pl.ds(r, SUBLANES), S5_HALF:2 * S5_HALF]
            xs[j, pl.ds(r, SUBLANES), 0:S5_HALF] = nhr
            xs[j, pl.ds(r, SUBLANES), S5_HALF:2 * S5_HALF] = nhi
            return nhr, nhi
        z = jnp.zeros((SUBLANES, S5_HALF), F32)
        er, ei = lax.fori_loop(0, seglen, scan, (z, z))

        a64r, a64i = pr_ref[j, seglen - 1:seglen, :], pi_ref[j, seglen - 1:seglen, :]
        cr, ci = carry[j, 0, 0:1, :], carry[j, 1, 0:1, :]
        for s in range(SUBLANES):
            cs[j, 0, s:s + 1, :] = cr
            cs[j, 1, s:s + 1, :] = ci
            cr, ci = (a64r * cr - a64i * ci + er[s:s + 1, :], a64r * ci + a64i * cr + ei[s:s + 1, :])
        carry[j, 0, 0:1, :] = cr
        carry[j, 1, 0:1, :] = ci
        hr_ref[j] = cr
        hi_ref[j] = ci

        cr8, ci8 = cs[j, 0], cs[j, 1]

        def fix(i2, _, j=j, cr8=cr8, ci8=ci8):
            r = pl.multiple_of(i2 * 2 * SUBLANES, 2 * SUBLANES)
            hrs, his = [], []
            for t in range(2):
                i = i2 * 2 + t
                pr, pi = pr_ref[j, pl.ds(i, 1), :], pi_ref[j, pl.ds(i, 1), :]
                rows = pl.ds(r + t * SUBLANES, SUBLANES)
                hrs.append(xs[j, rows, 0:S5_HALF] + pr * cr8 - pi * ci8)
                his.append(xs[j, rows, S5_HALF:2 * S5_HALF] + pr * ci8 + pi * cr8)
            hb[j, pl.ds(r, 2 * SUBLANES), 0:S5_HALF] = jnp.concatenate(hrs, 0).astype(BF16)
            hb[j, pl.ds(r, 2 * SUBLANES), S5_HALF:2 * S5_HALF] = jnp.concatenate(his, 0).astype(BF16)
            return 0
        lax.fori_loop(0, seglen // 2, fix, 0, unroll=2)

    _s5_readout(hb, uperm, gs, wc_ref, d_ref, wg_ref, bg_ref)

    def unperm(i, _):
        r = pl.multiple_of(i * SUBLANES, SUBLANES)
        for p in range(S5_BLOCKS):
            upad[p, pl.ds(i, SUBLANES, stride=pitch), :] = gs[pl.ds(r, SUBLANES), p * LANES:(p + 1) * LANES]
        return 0
    lax.fori_loop(0, seglen, unperm, 0)
    for s in range(SUBLANES):
        for p in range(S5_BLOCKS):
            o_ref[s * seglen:(s + 1) * seglen, p * LANES:(p + 1) * LANES] = upad[p, s * pitch:s * pitch + seglen, :]


def s5_prompt(u, wb, wc, prl, pil, d_skip, w_glu, b_glu, *, seglen):
    bsz, s, _ = u.shape
    t = SUBLANES * seglen
    pitch = seglen + SUBLANES
    full = lambda shp: pl.BlockSpec(shp, lambda b, c: (0,) * len(shp))
    st = pl.BlockSpec((None, S5_BLOCKS, 1, S5_HALF), lambda b, c: (b, 0, 0, 0))
    rows = pl.BlockSpec((None, t, B_WIDTH), lambda b, c: (b, c, 0))
    return pl.pallas_call(
        functools.partial(_s5_prompt_kernel, seglen=seglen),
        grid=(bsz, s // t),
        in_specs=[rows, full(wb.shape), full(wc.shape), full(prl.shape), full(pil.shape),
                  full((1, B_WIDTH)), full(w_glu.shape), full((1, B_WIDTH))],
        out_specs=(rows, st, st),
        out_shape=(jax.ShapeDtypeStruct((bsz, s, B_WIDTH), F32),
                   jax.ShapeDtypeStruct((bsz, S5_BLOCKS, 1, S5_HALF), F32),
                   jax.ShapeDtypeStruct((bsz, S5_BLOCKS, 1, S5_HALF), F32)),
        scratch_shapes=[pltpu.VMEM((S5_BLOCKS, SUBLANES * pitch, LANES), F32),
                        pltpu.VMEM((t, B_WIDTH), F32),
                        pltpu.VMEM((S5_BLOCKS, t, 2 * S5_HALF), F32),
                        pltpu.VMEM((S5_BLOCKS, t, 2 * S5_HALF), BF16),
                        pltpu.VMEM((S5_BLOCKS, 2, SUBLANES, S5_HALF), F32),
                        pltpu.VMEM((S5_BLOCKS, 2, SUBLANES, S5_HALF), F32),
                        pltpu.VMEM((t, B_WIDTH), F32)],
        compiler_params=_params("arbitrary", "arbitrary"),
        name="s5_prompt",
    )(u, wb, wc, prl, pil, d_skip, w_glu, b_glu)


def _s5_sample_kernel(u_ref, h0r_ref, h0i_ref, wb_ref, wc_ref, pr_ref, pi_ref, d_ref, wg_ref, bg_ref,
                      o_ref, hr_ref, hi_ref,
                      upad, uperm, xs, hb, gs, *, nseq, steps):
    nblk = nseq // SUBLANES
    for p in range(S5_BLOCKS):
        upad[p] = u_ref[:, p * LANES:(p + 1) * LANES]
    for st in range(steps):
        for bb in range(nblk):
            r = (st * nblk + bb) * SUBLANES
            for p in range(S5_BLOCKS):
                uperm[r:r + SUBLANES, p * LANES:(p + 1) * LANES] = (
                    upad[p, pl.ds(bb * SUBLANES * steps + st, SUBLANES, stride=steps), :])
    for j in range(S5_BLOCKS):
        xs[j] = _dot(uperm[:, j * LANES:(j + 1) * LANES], wb_ref[j])
    for j in range(S5_BLOCKS):
        ar = jnp.broadcast_to(pr_ref[j, 0:1, :], (SUBLANES, S5_HALF))
        ai = jnp.broadcast_to(pi_ref[j, 0:1, :], (SUBLANES, S5_HALF))
        for bb in range(nblk):
            seqs = slice(bb * SUBLANES, (bb + 1) * SUBLANES)
            hr, hi = h0r_ref[j, seqs, :], h0i_ref[j, seqs, :]
            for st in range(steps):
                r = (st * nblk + bb) * SUBLANES
                hr, hi = (ar * hr - ai * hi + xs[j, r:r + SUBLANES, 0:S5_HALF],
                          ar * hi + ai * hr + xs[j, r:r + SUBLANES, S5_HALF:2 * S5_HALF])
                xs[j, r:r + SUBLANES, 0:S5_HALF] = hr
                xs[j, r:r + SUBLANES, S5_HALF:2 * S5_HALF] = hi
            hr_ref[j, seqs, :] = hr
            hi_ref[j, seqs, :] = hi
        hb[j] = xs[j].astype(BF16)
    _s5_readout(hb, uperm, gs, wc_ref, d_ref, wg_ref, bg_ref)
    for st in range(steps):
        for bb in range(nblk):
            r = (st * nblk + bb) * SUBLANES
            for p in range(S5_BLOCKS):
                upad[p, pl.ds(bb * SUBLANES * steps + st, SUBLANES, stride=steps), :] = (
                    gs[r:r + SUBLANES, p * LANES:(p + 1) * LANES])
    for p in range(S5_BLOCKS):
        o_ref[:, p * LANES:(p + 1) * LANES] = upad[p]


def s5_sample(u, h0r, h0i, wb, wc, prl, pil, d_skip, w_glu, b_glu, *, nseq, steps):
    n = nseq * steps
    st_sd = jax.ShapeDtypeStruct((S5_BLOCKS, nseq, S5_HALF), F32)
    return pl.pallas_call(
        functools.partial(_s5_sample_kernel, nseq=nseq, steps=steps),
        out_shape=(jax.ShapeDtypeStruct((n, B_WIDTH), F32), st_sd, st_sd),
        scratch_shapes=[pltpu.VMEM((S5_BLOCKS, n, LANES), F32),
                        pltpu.VMEM((n, B_WIDTH), F32),
                        pltpu.VMEM((S5_BLOCKS, n, 2 * S5_HALF), F32),
                        pltpu.VMEM((S5_BLOCKS, n, 2 * S5_HALF), BF16),
                        pltpu.VMEM((n, B_WIDTH), F32)],
        compiler_params=pltpu.CompilerParams(vmem_limit_bytes=VMEM_LIMIT),
        name="s5_sample",
    )(u, h0r, h0i, wb, wc, prl, pil, d_skip, w_glu, b_glu)


def _attn_prompt_kernel(q_ref, kp_ref, kc_ref, vp_ref, vc_ref, o_ref, st_ref, kx, vx):
    tq = q_ref.shape[0]
    t = pl.program_id(2)
    kx[0:CHUNK, :] = kp_ref[...]
    kx[CHUNK:CHUNK + tq, :] = kc_ref[...]
    vx[0:CHUNK, :] = vp_ref[...]
    vx[CHUNK:CHUNK + tq, :] = vc_ref[...]
    r = lax.broadcasted_iota(jnp.int32, (CHUNK, 2 * CHUNK), 0)
    c = lax.broadcasted_iota(jnp.int32, (CHUNK, 2 * CHUNK), 1)
    band = (c >= r) & (c <= r + CHUNK)
    lane = lax.broadcasted_iota(jnp.int32, (CHUNK, LANES), 1)
    lo_half = lane < HEAD_DIM
    for u in range(tq // CHUNK):
        rows = slice(u * CHUNK, (u + 1) * CHUNK)
        keys = slice(u * CHUNK, (u + 2) * CHUNK)
        if u == 0:
            valid = band & (c >= jnp.where(t == 0, CHUNK, 0))
        else:
            valid = band
        bias = jnp.where(valid, 0.0, NEG_INF)
        st = jnp.zeros((CHUNK, LANES), F32)
        for p in range(C_WIDTH // LANES):
            cols = slice(p * LANES, (p + 1) * LANES)
            qp = q_ref[rows, cols]
            kpair = kx[keys, cols]
            vpair = vx[keys, cols]
            outs = []
            for hh in range(2):
                qm = jnp.where(lo_half if hh == 0 else ~lo_half, qp, jnp.zeros_like(qp))
                sc = lax.dot_general(qm, kpair, (((1,), (1,)), ((), ())), preferred_element_type=F32) + bias
                m = jnp.max(sc, -1, keepdims=True)
                pe = jnp.exp(sc - m)
                l = jnp.sum(pe, -1, keepdims=True)
                outs.append(jnp.dot(pe.astype(BF16), vpair, preferred_element_type=F32) / l)
                st = jnp.where(lane == 2 * p + hh, m + jnp.log(l), st)
            o_ref[rows, cols] = jnp.where(lo_half, outs[0], outs[1])
        st_ref[rows, :] = st


def attn_prompt(q, k, v, dil, *, tq):
    bsz, ln, _ = q.shape
    tq = min(tq, ln)
    cur = pl.BlockSpec((None, tq, C_WIDTH), lambda b, r, t: (b, t, r))
    prev = pl.BlockSpec((None, CHUNK, C_WIDTH), lambda b, r, t: (b, jnp.maximum(t * (tq // CHUNK) - 1, 0), r))
    return pl.pallas_call(
        _attn_prompt_kernel,
        grid=(bsz, dil, ln // tq),
        in_specs=[cur, prev, cur, prev, cur],
        out_specs=(cur, pl.BlockSpec((None, tq, LANES), lambda b, r, t: (b, t, r))),
        out_shape=(jax.ShapeDtypeStruct((bsz, ln, dil * C_WIDTH), F32),
                   jax.ShapeDtypeStruct((bsz, ln, dil * LANES), F32)),
        scratch_shapes=[pltpu.VMEM((CHUNK + tq, C_WIDTH), BF16), pltpu.VMEM((CHUNK + tq, C_WIDTH), BF16)],
        compiler_params=_params("arbitrary", "arbitrary", "arbitrary"),
        name=f"attn_prompt_d{dil}",
    )(q, k, k, v, v)


def _attn_sample_kernel(q_ref, kn_ref, vn_ref, ck_ref, cv_ref, o_ref, *, steps):
    cw = ck_ref.shape[-1]

    def mult(delta):
        cnt = jnp.zeros(delta.shape, F32)
        for window, dil in DILATIONS:
            cnt = cnt + jnp.where((delta >= 0) & (delta <= window) & ((delta & (dil - 1)) == 0), 1.0, 0.0)
        return cnt

    cnt_c = mult(cw + lax.broadcasted_iota(jnp.int32, (steps, cw), 0)
                 - lax.broadcasted_iota(jnp.int32, (steps, cw), 1))
    cnt_n = mult(lax.broadcasted_iota(jnp.int32, (steps, CHUNK), 0)
                 - lax.broadcasted_iota(jnp.int32, (steps, CHUNK), 1))
    live_c, live_n = cnt_c > 0, cnt_n > 0
    zpad = jnp.zeros((CHUNK - steps, C_WIDTH), F32)
    kn = jnp.concatenate([kn_ref[...], zpad], 0).astype(BF16)
    vn = jnp.concatenate([vn_ref[...], zpad], 0).astype(BF16)
    q = q_ref[...].astype(BF16)
    nt_dims = (((1,), (1,)), ((), ()))
    for hh in range(C_HEADS):
        cols = slice(hh * HEAD_DIM, (hh + 1) * HEAD_DIM)
        qh = q[:, cols]
        sc = jnp.dot(qh, ck_ref[hh].astype(BF16), preferred_element_type=F32)
        sn = lax.dot_general(qh, kn[:, cols], nt_dims, preferred_element_type=F32)
        m = jnp.maximum(jnp.max(jnp.where(live_c, sc, NEG_INF), -1, keepdims=True),
                        jnp.max(jnp.where(live_n, sn, NEG_INF), -1, keepdims=True))
        ec = cnt_c * jnp.exp(jnp.where(live_c, sc - m, NEG_INF))
        en = cnt_n * jnp.exp(jnp.where(live_n, sn - m, NEG_INF))
        l = jnp.sum(ec, -1, keepdims=True) + jnp.sum(en, -1, keepdims=True)
        acc = (lax.dot_general(ec.astype(BF16), cv_ref[hh].astype(BF16), nt_dims, preferred_element_type=F32)
               + jnp.dot(en.astype(BF16), vn[:, cols], preferred_element_type=F32))
        o_ref[:, cols] = acc / l


def attn_sample(q, k_new, v_new, cache_kt, cache_vt, layer, *, steps):
    _, nseq, _, _, cw = cache_kt.shape
    rows = pl.BlockSpec((steps, C_WIDTH), lambda b: (b, 0))
    cache = pl.BlockSpec((None, None, C_HEADS, HEAD_DIM, cw), lambda b: (layer, b, 0, 0, 0))
    return pl.pallas_call(
        functools.partial(_attn_sample_kernel, steps=steps),
        grid=(nseq,),
        in_specs=[rows, rows, rows, cache, cache],
        out_specs=rows,
        out_shape=jax.ShapeDtypeStruct((nseq * steps, C_WIDTH), F32),
        compiler_params=_params("arbitrary"),
        name="attn_sample",
    )(q, k_new, v_new, cache_kt, cache_vt)


FF_ROWS = 128


def _residual_and_norm(x_ref, oa_ref, ob_ref, oc, wo_ref, g2_ref, y_ref, xn_scr):
    mix = (_dot(oa_ref[...], wo_ref[0:A_WIDTH, :]) + _dot(ob_ref[...], wo_ref[A_WIDTH:A_WIDTH + B_WIDTH, :])
           + _dot(oc, wo_ref[A_WIDTH + B_WIDTH:, :]))
    x1 = x_ref[...] + mix
    y_ref[...] = x1
    xn_scr[...] = _rms(x1, g2_ref[...]).astype(BF16)


def _outproj_prompt_kernel(x_ref, oa_ref, ob_ref, *rest, tiles_per_seq, dils, final_norm):
    nb = len(dils)
    o_refs, st_refs = rest[:nb], rest[nb:2 * nb]
    (e_ref, wo_ref, g2_ref, wup_ref, cw_ref, cb_ref, wd_ref, fg_ref,
     y_ref, cg_ref, cu_ref, xn_scr, carry_scr, unp_scr, hs_scr) = rest[2 * nb:]
    i = pl.program_id(0)
    tm = x_ref.shape[0]
    nslab = C_WIDTH // LANES

    outs, sts = [], []
    slot = 0
    for dil, o_ref, s_ref in zip(dils, o_refs, st_refs):
        if dil == 1:
            outs.append(o_ref[...])
            sts.append(s_ref[...])
            continue
        for r in range(dil):
            rows = pl.ds(r, tm // dil, stride=dil)
            for p in range(nslab):
                c0 = r * C_WIDTH + p * LANES
                unp_scr[slot + p, rows, :] = o_ref[:, c0:c0 + LANES]
            unp_scr[slot + nslab, rows, :] = s_ref[:, r * LANES:(r + 1) * LANES]
        outs.append(jnp.concatenate([unp_scr[slot + p] for p in range(nslab)], axis=1))
        sts.append(unp_scr[slot + nslab])
        slot += nslab + 1
    mx = functools.reduce(jnp.maximum, sts)
    ws = [jnp.exp(s - mx) for s in sts]
    tot = functools.reduce(lambda a, b: a + b, ws)
    oc = jnp.zeros((tm, C_WIDTH), F32)
    for w, o in zip(ws, outs):
        alpha = w / tot
        hi = alpha.astype(BF16)
        lo = (alpha - hi.astype(F32)).astype(BF16)
        wide = (jnp.dot(hi, e_ref[...], preferred_element_type=F32)
                + jnp.dot(lo, e_ref[...], preferred_element_type=F32))
        oc = oc + wide * o
    _residual_and_norm(x_ref, oa_ref, ob_ref, oc, wo_ref, g2_ref, y_ref, xn_scr)

    @pl.when((i % tiles_per_seq) == 0)
    def _():
        carry_scr[...] = jnp.zeros_like(carry_scr)

    nhs = 2 * FF_CHUNK // LANES
    gate = FF_CHUNK // LANES

    def up(jj, r):
        xr = xn_scr[r * FF_ROWS:(r + 1) * FF_ROWS, :]
        rows = slice(SUBLANES + r * FF_ROWS, SUBLANES + (r + 1) * FF_ROWS)
        for half in range(2):
            h = jnp.dot(xr, wup_ref[half * N_FF + jj], preferred_element_type=F32)
            for k in range(gate):
                hs_scr[2 * (half * gate + k), rows, :] = h[:, k * LANES:(k + 1) * LANES]

    def chunk(j, _):
        for k in range(nhs):
            hs_scr[2 * k, 0:SUBLANES, :] = carry_scr[j, k]
        cwj = jnp.concatenate([cw_ref[j], cw_ref[N_FF + j]], axis=1)
        cbj = jnp.concatenate([cb_ref[j], cb_ref[N_FF + j]], axis=1)
        slabs = lambda row: jnp.stack([row[:, k * LANES:(k + 1) * LANES] for k in range(nhs)])
        taps = [slabs(cwj[t:t + 1]) for t in range(3)]
        bias = slabs(cbj)

        def down(r):
            every_other = pl.ds(0, nhs, stride=2)
            shifted = lambda back: hs_scr[every_other, pl.ds(SUBLANES - back + r * FF_ROWS, FF_ROWS), :]
            hc = bias + (shifted(2) * taps[0] + shifted(1) * taps[1] + shifted(0) * taps[2])
            act = jax.nn.gelu(hc[:gate]) * hc[gate:]
            act = jnp.concatenate([act[k] for k in range(gate)], axis=1)
            y_ref[r * FF_ROWS:(r + 1) * FF_ROWS, :] += _dot(act, wd_ref[j])

        nchain = tm // FF_ROWS
        for r in range(nchain):
            if r + 1 < nchain:
                up(j, r + 1)
            else:
                up(jnp.minimum(j + 1, N_FF - 1), 0)
            down(r)
        for k in range(nhs):
            last = hs_scr[2 * k, tm:tm + SUBLANES, :]
            carry_scr[j, k] = last
            if k < gate:
                cg_ref[0, j, :, k * LANES:(k + 1) * LANES] = last
            else:
                cu_ref[0, j, :, (k - gate) * LANES:(k - gate + 1) * LANES] = last
        return 0

    up(0, 0)
    lax.fori_loop(0, N_FF, chunk, 0)

    if final_norm:
        y_ref[...] = _rms(y_ref[...], fg_ref[...])


def _outproj_sample_kernel(x_ref, oa_ref, ob_ref, oc_ref, wo_ref, g2_ref, wug_ref, wuu_ref, cwg_ref, cwu_ref,
                           cbg_ref, cbu_ref, wd_ref, fg_ref, bufg_ref, bufu_ref,
                           y_ref, cg_ref, cu_ref, xn_scr, hs_scr, *, nseq, seq_len, final_norm):
    j = pl.program_id(1)
    tm = x_ref.shape[0]

    @pl.when(j == 0)
    def _():
        _residual_and_norm(x_ref, oa_ref, ob_ref, oc_ref[...], wo_ref, g2_ref, y_ref, xn_scr)

    xn = xn_scr[...]
    hg = jnp.dot(xn, wug_ref[...], preferred_element_type=F32)
    hu = jnp.dot(xn, wuu_ref[...], preferred_element_type=F32)
    lo, hi_ = SUBLANES, SUBLANES + seq_len
    hs_scr[:, lo:hi_, 0:FF_CHUNK] = hg.reshape(nseq, seq_len, FF_CHUNK)
    hs_scr[:, lo:hi_, FF_CHUNK:] = hu.reshape(nseq, seq_len, FF_CHUNK)
    hs_scr[:, lo - 2:lo, 0:FF_CHUNK] = bufg_ref[...]
    hs_scr[:, lo - 2:lo, FF_CHUNK:] = bufu_ref[...]
    cg_ref[...] = hs_scr[:, hi_ - SUBLANES:hi_, 0:FF_CHUNK]
    cu_ref[...] = hs_scr[:, hi_ - SUBLANES:hi_, FF_CHUNK:]

    cw = jnp.concatenate([cwg_ref[...], cwu_ref[...]], axis=1)
    cb = jnp.concatenate([cbg_ref[...], cbu_ref[...]], axis=1)
    hc = cb + (hs_scr[:, lo - 2:hi_ - 2, :] * cw[0:1] + hs_scr[:, lo - 1:hi_ - 1, :] * cw[1:2]
               + hs_scr[:, lo:hi_, :] * cw[2:3])
    hc = hc.reshape(tm, 2 * FF_CHUNK)
    act = jax.nn.gelu(hc[:, :FF_CHUNK]) * hc[:, FF_CHUNK:]
    y_ref[...] += _dot(act, wd_ref[...])

    if final_norm:
        @pl.when(j == pl.num_programs(1) - 1)
        def _():
            y_ref[...] = _rms(y_ref[...], fg_ref[...])


def ffn_layouts(w_up, conv_w, conv_b, w_down):
    chunks = lambda a: jnp.transpose(a.reshape(a.shape[0], 2 * N_FF, FF_CHUNK), (1, 0, 2))
    return (chunks(w_up), chunks(conv_w), chunks(conv_b.reshape(1, -1)), w_down.reshape(N_FF, FF_CHUNK, D_MODEL))


def out_projection_prompt(x, oa, ob, o_list, st_list, dils, expand, ffw, *, tm, tiles_per_seq, final_norm):
    n = x.shape[0]
    nt = n // tm
    row = lambda w: pl.BlockSpec((tm, w), lambda i: (i, 0))
    perm = lambda w: [pl.BlockSpec((tm // d, d * w), lambda i: (i, 0)) for d in dils]
    resident = lambda a: pl.BlockSpec(a.shape, lambda i: (0,) * a.ndim, pipeline_mode=pl.Buffered(1))
    conv_spec = pl.BlockSpec((1, N_FF, SUBLANES, FF_CHUNK), lambda i: (i, 0, 0, 0))
    conv_sd = jax.ShapeDtypeStruct((nt, N_FF, SUBLANES, FF_CHUNK), F32)
    n_unp = sum(C_WIDTH // LANES + 1 for d in dils if d > 1)
    nhs = 2 * FF_CHUNK // LANES
    return pl.pallas_call(
        functools.partial(_outproj_prompt_kernel, tiles_per_seq=tiles_per_seq, dils=tuple(dils),
                          final_norm=final_norm),
        grid=(nt,),
        in_specs=[row(D_MODEL), row(A_WIDTH), row(B_WIDTH)] + perm(C_WIDTH) + perm(LANES)
                 + [resident(expand)] + [resident(a) for a in ffw],
        out_specs=(row(D_MODEL), conv_spec, conv_spec),
        out_shape=(jax.ShapeDtypeStruct((n, D_MODEL), F32), conv_sd, conv_sd),
        scratch_shapes=[pltpu.VMEM((tm, D_MODEL), BF16),
                        pltpu.VMEM((N_FF, nhs, SUBLANES, LANES), F32),
                        pltpu.VMEM((n_unp, tm, LANES), F32),
                        pltpu.VMEM((2 * nhs, SUBLANES + tm, LANES), F32)],
        compiler_params=_params("arbitrary"),
        name="out_projection_prompt",
    )(x, oa, ob, *o_list, *st_list, expand, *ffw)


def out_projection_sample(x, oa, ob, oc, ffw, bufs, *, nseq, seq_len, final_norm):
    n = x.shape[0]
    row = lambda w: pl.BlockSpec((n, w), lambda i, j: (0, 0))
    full = lambda a: pl.BlockSpec(a.shape, lambda i, j: (0,) * a.ndim)
    w_out, g2, w_up, conv_w, conv_b, w_down, fg = ffw
    chunk = lambda a, off: pl.BlockSpec((None,) + a.shape[1:], lambda i, j: (off + j, 0, 0))
    wspecs = [full(w_out), full(g2), chunk(w_up, 0), chunk(w_up, N_FF), chunk(conv_w, 0), chunk(conv_w, N_FF),
              chunk(conv_b, 0), chunk(conv_b, N_FF), chunk(w_down, 0), full(fg)]
    wargs = [w_out, g2, w_up, w_up, conv_w, conv_w, conv_b, conv_b, w_down, fg]
    conv_spec = pl.BlockSpec((nseq, SUBLANES, FF_CHUNK), lambda i, j: (0, 0, j))
    return pl.pallas_call(
        functools.partial(_outproj_sample_kernel, nseq=nseq, seq_len=seq_len, final_norm=final_norm),
        grid=(1, N_FF),
        in_specs=[row(D_MODEL), row(A_WIDTH), row(B_WIDTH), row(C_WIDTH)] + wspecs
                 + [pl.BlockSpec((nseq, 2, FF_CHUNK), lambda i, j: (0, 0, j)),
                    pl.BlockSpec((nseq, 2, FF_CHUNK), lambda i, j: (0, 0, N_FF + j))],
        out_specs=(row(D_MODEL), conv_spec, conv_spec),
        out_shape=(jax.ShapeDtypeStruct((n, D_MODEL), F32),
                   jax.ShapeDtypeStruct((nseq, SUBLANES, D_FF), F32),
                   jax.ShapeDtypeStruct((nseq, SUBLANES, D_FF), F32)),
        scratch_shapes=[pltpu.VMEM((n, D_MODEL), BF16),
                        pltpu.VMEM((nseq, SUBLANES + seq_len, 2 * FF_CHUNK), F32)],
        compiler_params=_params("arbitrary", "arbitrary"),
        name="out_projection_sample",
    )(x, oa, ob, oc, *wargs, bufs, bufs)


PROMPT_TM = 512
S5_SEGLEN = 64
ATTN_TQ = 512


def kernel(x_prompt, x_sample, cache_c_k, cache_c_v, state_ssm_re, state_ssm_im, state_ffn_conv, norm1_g, w_in, w_s, b_s, ssm_lam_re, ssm_lam_im, ssm_log_dt, ssm_b_re, ssm_b_im, ssm_c_re, ssm_c_im, ssm_d, w_glu, b_glu, w_out, norm2_g, w_up, conv_w, conv_b, w_down, final_g):
    bp, sp, _ = x_prompt.shape
    nseq, steps, _ = x_sample.shape
    ns = nseq * steps
    cw_p = min(MAX_WINDOW, sp)
    cw_s = cache_c_k.shape[2]

    rope_p = rope_tables(sp, 0)
    rope_s = tuple(jnp.tile(t, (nseq, 1)) for t in rope_tables(steps, PAST_LEN))
    bbr, bbi, pr, pi, ncim = s5_params(ssm_lam_re, ssm_lam_im, ssm_log_dt, ssm_b_re, ssm_b_im, ssm_c_im, S5_SEGLEN)

    w_in_b, w_out_b, w_up_b, w_down_b, w_glu_b = (w.astype(BF16) for w in (w_in, w_out, w_up, w_down, w_glu))
    expand = (jnp.arange(LANES)[:, None] == jnp.arange(C_WIDTH)[None, :] // HEAD_DIM).astype(BF16)
    bs_p = jnp.repeat(jnp.transpose(b_s, (0, 2, 1)), HEAD_DIM, axis=2)
    bs_s = jnp.tile(bs_p[:, :steps], (1, CHUNK // steps, 1))
    ws_s = jnp.tile(w_s[:, :, :steps, :steps], (1, 1, CHUNK // steps, CHUNK // steps))
    row1 = lambda a: a.reshape(1, -1)

    cache_kt = jnp.transpose(cache_c_k, (0, 1, 3, 4, 2))
    cache_vt = jnp.transpose(cache_c_v, (0, 1, 3, 4, 2))
    dils = tuple(d for _, d in DILATIONS)

    hp = x_prompt
    hs = x_sample.reshape(1, ns, D_MODEL)
    outs = [[] for _ in range(11)]
    for l in range(DEPTH):
        last = l == DEPTH - 1
        wb, wc, prl, pil = s5_layouts(bbr[l], bbi[l], ssm_c_re[l], ncim[l], pr[l], pi[l])
        s5w = (wb, wc, prl, pil, row1(ssm_d[l]), w_glu_b[l], row1(b_glu[l]))
        ffw = (w_out_b[l], row1(norm2_g[l]), *ffn_layouts(w_up_b[l], conv_w[l], conv_b[l], w_down_b[l]),
               row1(final_g))

        oa, _, ub, kf, vf, *qkv = in_projection(hp, row1(norm1_g[l]), w_in_b[l], w_s[l], bs_p[l], rope_p,
                                                tm=PROMPT_TM, seq_rows=CHUNK, tail_rows=cw_p, dils=dils)
        nd = len(dils)
        ob, hr, hi = s5_prompt(ub, *s5w, seglen=S5_SEGLEN)
        o_list, st_list = zip(*[attn_prompt(qkv[n], qkv[nd + n], qkv[2 * nd + n], dil, tq=ATTN_TQ)
                                for n, dil in enumerate(dils)])
        flat = lambda a: a.reshape(a.shape[0] * a.shape[1], a.shape[2])
        y, cg, cu = out_projection_prompt(flat(hp), flat(oa), flat(ob), [flat(o) for o in o_list],
                                          [flat(s) for s in st_list], dils, expand, ffw,
                                          tm=PROMPT_TM, tiles_per_seq=sp // PROMPT_TM, final_norm=last)
        hp = y.reshape(bp, sp, D_MODEL)
        outs[0].append(kf.reshape(bp, cw_p, C_HEADS, HEAD_DIM))
        outs[1].append(vf.reshape(bp, cw_p, C_HEADS, HEAD_DIM))
        outs[4].append(hr.reshape(bp, B_GROUPS, SSM_STATE))
        outs[5].append(hi.reshape(bp, B_GROUPS, SSM_STATE))
        seq_end = lambda a: jnp.transpose(a.reshape(bp, sp // PROMPT_TM, N_FF, SUBLANES, FF_CHUNK)[:, -1, :, -2:],
                                          (0, 2, 1, 3)).reshape(bp, 2, D_FF)
        outs[8].append(jnp.concatenate([seq_end(cg), seq_end(cu)], -1))

        oa, vn, ub, kf, vf, q, k, v = in_projection(hs, row1(norm1_g[l]), w_in_b[l], ws_s[l], bs_s[l], rope_s,
                                                    tm=ns, seq_rows=steps, tail_rows=ns, dils=(1,))
        h0r = jnp.transpose(state_ssm_re[l].reshape(nseq, S5_BLOCKS, S5_HALF), (1, 0, 2))
        h0i = jnp.transpose(state_ssm_im[l].reshape(nseq, S5_BLOCKS, S5_HALF), (1, 0, 2))
        ob, hr, hi = s5_sample(ub[0], h0r, h0i, *s5w, nseq=nseq, steps=steps)
        oc = attn_sample(q[0].astype(F32), k[0].astype(F32), v[0].astype(F32), cache_kt, cache_vt, l, steps=steps)
        y, cg, cu = out_projection_sample(hs[0], oa[0], ob, oc, ffw, state_ffn_conv[l],
                                          nseq=nseq, seq_len=steps, final_norm=last)
        hs = y.reshape(1, ns, D_MODEL)
        outs[2].append(kf.reshape(nseq, steps, C_HEADS, HEAD_DIM))
        outs[3].append(vf.reshape(nseq, steps, C_HEADS, HEAD_DIM))
        outs[6].append(jnp.transpose(hr, (1, 0, 2)).reshape(nseq, B_GROUPS, SSM_STATE))
        outs[7].append(jnp.transpose(hi, (1, 0, 2)).reshape(nseq, B_GROUPS, SSM_STATE))
        outs[9].append(jnp.concatenate([cg[:, -2:], cu[:, -2:]], -1))
        outs[10].append(vn.reshape(nseq, steps, A_WIDTH))

    return (hp, hs.reshape(nseq, steps, D_MODEL)) + tuple(jnp.stack(o) for o in outs)
```

```python
import functools
import math

import jax
import jax.numpy as jnp
from jax import lax
from jax.experimental import pallas as pl
from jax.experimental.pallas import tpu as pltpu

F32 = jnp.float32
BF16 = jnp.bfloat16

D_MODEL = 1024
DEPTH = 4
PAST_LEN = 8192
HEAD_DIM = 64
A_WIDTH = 256
B_WIDTH = 384
C_WIDTH = 384
A_HEADS = 4
C_HEADS = 6
CHUNK = 128
SSM_GROUP = 16
B_GROUPS = 24
SSM_STATE = 64
DILATIONS = ((128, 1), (512, 4), (2048, 16))
MAX_WINDOW = 2048
ROPE_THETA = 500000.0
ROT_DIM = 16
D_FF = 2816
EPS = 1e-6
NEG_INF = -1e30

O_A, O_B, O_Q, O_K, O_V = 0, 2 * A_WIDTH, 896, 1280, 1664

LANES = 128
SUBLANES = 8
S5_BLOCKS = B_WIDTH // LANES
S5_HALF = 512
FF_CHUNK = 256
N_FF = D_FF // FF_CHUNK
VMEM_LIMIT = 56 * 1024 * 1024


def _dot(a, b):
    return jnp.dot(a.astype(BF16), b.astype(BF16), preferred_element_type=F32)


def _rms(x, g):
    return x * lax.rsqrt(jnp.mean(x * x, -1, keepdims=True) + EPS) * g


def _params(*sem):
    return pltpu.CompilerParams(dimension_semantics=sem, vmem_limit_bytes=VMEM_LIMIT)


def _rope_tables_kernel(c_ref, a_ref, b_ref, *, pos0):
    n = c_ref.shape[0]
    pos = (lax.broadcasted_iota(jnp.int32, (n, LANES), 0) + (pos0 + pl.program_id(0) * n)).astype(F32)
    d = lax.broadcasted_iota(jnp.int32, (n, LANES), 1) & (HEAD_DIM - 1)
    k = (d & (ROT_DIM // 2 - 1)).astype(F32)
    inv = jnp.exp(k * (-2.0 / ROT_DIM * math.log(ROPE_THETA)))
    ang = pos * inv
    cos, sin = jnp.cos(ang), jnp.sin(ang)
    c_ref[...] = jnp.where(d < ROT_DIM, cos, 1.0)
    a_ref[...] = jnp.where(d < ROT_DIM // 2, -sin, 0.0)
    b_ref[...] = jnp.where((d >= ROT_DIM // 2) & (d < ROT_DIM), sin, 0.0)


def rope_tables(n, pos0):
    sds = jax.ShapeDtypeStruct((n, LANES), F32)
    tr = min(n, 1024)
    spec = pl.BlockSpec((tr, LANES), lambda i: (i, 0))
    return pl.pallas_call(functools.partial(_rope_tables_kernel, pos0=pos0), grid=(n // tr,),
                          out_specs=(spec, spec, spec), out_shape=(sds, sds, sds),
                          compiler_params=_params("arbitrary"), name="rope_tables")()


def _s5_params_kernel(lr_ref, li_ref, ldt_ref, br_ref, bi_ref, cim_ref,
                      bbr_ref, bbi_ref, pr_ref, pi_ref, ncim_ref, *, n_pow):
    lr, li = lr_ref[...], li_ref[...]
    dt = jnp.exp(ldt_ref[...])
    mag = jnp.exp(lr * dt)
    ar, ai = mag * jnp.cos(li * dt), mag * jnp.sin(li * dt)
    nr, ni = ar - 1.0, ai
    den = lr * lr + li * li
    fr, fi = (nr * lr + ni * li) / den, (ni * lr - nr * li) / den
    br, bi = br_ref[...], bi_ref[...]
    bbr_ref[...] = fr[None] * br - fi[None] * bi
    bbi_ref[...] = fr[None] * bi + fi[None] * br
    k = (lax.broadcasted_iota(jnp.int32, (n_pow,) + lr.shape, 0) + 1).astype(F32)
    magk = jnp.exp(k * (lr * dt)[None])
    angk = k * (li * dt)[None]
    pr_ref[...] = magk * jnp.cos(angk)
    pi_ref[...] = magk * jnp.sin(angk)
    ncim_ref[...] = -cim_ref[...]


def s5_params(lam_re, lam_im, log_dt, b_re, b_im, c_im, n_pow):
    depth, g, p = lam_re.shape
    c = b_re.shape[-1]
    ldt = jnp.broadcast_to(log_dt[:, :, None], (depth, g, p))
    brt = jnp.transpose(b_re, (0, 3, 1, 2))
    bit = jnp.transpose(b_im, (0, 3, 1, 2))
    gp = pl.BlockSpec((None, g, p), lambda l: (l, 0, 0))
    cgp = pl.BlockSpec((None, c, g, p), lambda l: (l, 0, 0, 0))
    gcp = pl.BlockSpec((None, g, c, p), lambda l: (l, 0, 0, 0))
    kgp = pl.BlockSpec((None, n_pow, g, p), lambda l: (l, 0, 0, 0))
    return pl.pallas_call(
        functools.partial(_s5_params_kernel, n_pow=n_pow),
        grid=(depth,),
        in_specs=[gp, gp, gp, cgp, cgp, gcp],
        out_specs=(cgp, cgp, kgp, kgp, gcp),
        out_shape=(jax.ShapeDtypeStruct((depth, c, g, p), F32), jax.ShapeDtypeStruct((depth, c, g, p), F32),
                   jax.ShapeDtypeStruct((depth, n_pow, g, p), F32), jax.ShapeDtypeStruct((depth, n_pow, g, p), F32),
                   jax.ShapeDtypeStruct((depth, g, c, p), F32)),
        compiler_params=_params("arbitrary"),
        name="s5_params",
    )(lam_re, lam_im, ldt, brt, bit, c_im)


def _s5_lane_vec(t):
    return t.reshape(t.shape[:-2] + (S5_BLOCKS, 1, S5_HALF))


def s5_layouts(bbr, bbi, c_re, ncim, pr, pi):
    eye = jnp.eye(SUBLANES, dtype=F32)
    bb = jnp.stack([bbr, bbi], 0).reshape(2, SSM_GROUP, S5_BLOCKS, 8, SSM_STATE)
    wb = jnp.einsum('ecjgp,gh->jgcehp', bb, eye).reshape(S5_BLOCKS, LANES, 2 * S5_HALF)
    cc = jnp.stack([c_re, ncim], 0).reshape(2, S5_BLOCKS, 8, SSM_GROUP, SSM_STATE)
    wc = jnp.einsum('ejgcp,gh->jeghpc', cc, eye)
    wc = jnp.transpose(wc, (0, 1, 2, 4, 3, 5)).reshape(S5_BLOCKS, 2 * S5_HALF, LANES)
    n_pow = pr.shape[0]
    prl = jnp.transpose(pr.reshape(n_pow, S5_BLOCKS, S5_HALF), (1, 0, 2))
    pil = jnp.transpose(pi.reshape(n_pow, S5_BLOCKS, S5_HALF), (1, 0, 2))
    return wb.astype(BF16), wc.astype(BF16), prl, pil


def _rope_apply(x, c, a, b):
    return x * c + pltpu.roll(x, LANES - ROT_DIM // 2, 1) * a + pltpu.roll(x, ROT_DIM // 2, 1) * b


def _inproj_kernel(x_ref, g_ref, w_ref, ws_ref, bs_ref, rc_ref, ra_ref, rb_ref,
                   oa_ref, vn_ref, ub_ref, kf_ref, vf_ref, *rest, seq_rows, dils):
    qkv_refs, proj_scr, perm_scr, perm2_scr = rest[:-3], rest[-3], rest[-2], rest[-1]
    tm = x_ref.shape[0]
    xn = _rms(x_ref[...], g_ref[...]).astype(BF16)
    proj_scr[...] = jnp.dot(xn, w_ref[...], preferred_element_type=F32)

    h = jax.nn.gelu(proj_scr[:, O_A:O_B])
    u, v = h[:, :A_WIDTH], h[:, A_WIDTH:]
    mu = jnp.mean(v, -1, keepdims=True)
    var = jnp.mean(jnp.square(v - mu), -1, keepdims=True)
    vn = (v - mu) * lax.rsqrt(var + EPS)
    vn_ref[...] = vn
    ri = lax.broadcasted_iota(jnp.int32, (CHUNK, CHUNK), 0)
    ci = lax.broadcasted_iota(jnp.int32, (CHUNK, CHUNK), 1)
    keep = (ci <= ri) & ((ri // seq_rows) == (ci // seq_rows))
    lane = lax.broadcasted_iota(jnp.int32, (CHUNK, LANES), 1)
    wm = [jnp.where(keep, ws_ref[hh], 0.0).astype(BF16) for hh in range(A_HEADS)]
    vnb = vn.astype(BF16)
    for c in range(tm // CHUNK):
        rows = slice(c * CHUNK, (c + 1) * CHUNK)
        for p in range(A_WIDTH // LANES):
            cols = slice(p * LANES, (p + 1) * LANES)
            vp = vnb[rows, cols]
            m0 = jnp.dot(wm[2 * p], vp, preferred_element_type=F32)
            m1 = jnp.dot(wm[2 * p + 1], vp, preferred_element_type=F32)
            mixed = jnp.where(lane < HEAD_DIM, m0, m1) + bs_ref[:, cols]
            oa_ref[rows, cols] = u[rows, cols] * mixed

    ub_ref[...] = proj_scr[:, O_B:O_Q]

    rc, ra, rb = rc_ref[...], ra_ref[...], rb_ref[...]
    def emit(val, outs, slot, p):
        outs[0][:, p * LANES:(p + 1) * LANES] = val.astype(BF16)
        if len(dils) == 1:
            return
        perm_scr[slot] = val
        src, dst, prev = perm_scr, perm2_scr, 1
        for n in range(1, len(dils)):
            dil = dils[n]
            f = dil // prev
            for c in range(prev):
                for s in range(f):
                    cls = c + prev * s
                    blk = src[slot, pl.ds(c * (tm // prev) + s, tm // dil, stride=f), :]
                    c0 = cls * C_WIDTH + p * LANES
                    outs[n][:, c0:c0 + LANES] = blk.astype(BF16)
                    if n + 1 < len(dils):
                        dst[slot, cls * (tm // dil):(cls + 1) * (tm // dil), :] = blk
            src, dst, prev = dst, src, dil

    nd = len(dils)
    assert dils[0] == 1
    q_outs, k_outs, v_outs = qkv_refs[0:nd], qkv_refs[nd:2 * nd], qkv_refs[2 * nd:3 * nd]
    for p in range(C_WIDTH // LANES):
        cols = slice(p * LANES, (p + 1) * LANES)
        q = proj_scr[:, O_Q + p * LANES:O_Q + (p + 1) * LANES]
        emit(_rope_apply(q, rc, ra, rb) * (HEAD_DIM ** -0.5), q_outs, 3 * p, p)
        k = _rope_apply(proj_scr[:, O_K + p * LANES:O_K + (p + 1) * LANES], rc, ra, rb)
        emit(k, k_outs, 3 * p + 1, p)
        kf_ref[:, cols] = k
        vv = proj_scr[:, O_V + p * LANES:O_V + (p + 1) * LANES]
        emit(vv, v_outs, 3 * p + 2, p)
        vf_ref[:, cols] = vv


def in_projection(x, g1, w_in, w_s, bs_rows, rope, *, tm, seq_rows, tail_rows, dils):
    bsz, s, _ = x.shape
    nt = s // tm
    tail_t = tail_rows // tm
    row = lambda w: pl.BlockSpec((None, tm, w), lambda b, t: (b, t, 0))
    tail = pl.BlockSpec((None, tm, C_WIDTH), lambda b, t: (b, jnp.maximum(t - (nt - tail_t), 0), 0))
    full = lambda shp: pl.BlockSpec(shp, lambda b, t: (0,) * len(shp))
    ropespec = pl.BlockSpec((tm, LANES), lambda b, t: (t, 0))
    sd = lambda w, dt: jax.ShapeDtypeStruct((bsz, s, w), dt)
    tail_sd = jax.ShapeDtypeStruct((bsz, tail_rows, C_WIDTH), F32)
    qkv_specs = [pl.BlockSpec((None, tm // d, d * C_WIDTH), lambda b, t: (b, t, 0)) for d in dils] * 3
    qkv_sds = [jax.ShapeDtypeStruct((bsz, s // d, d * C_WIDTH), BF16) for d in dils] * 3
    return pl.pallas_call(
        functools.partial(_inproj_kernel, seq_rows=seq_rows, dils=tuple(dils)),
        grid=(bsz, nt),
        in_specs=[row(D_MODEL), full((1, D_MODEL)), full(w_in.shape), full(w_s.shape), full(bs_rows.shape),
                  ropespec, ropespec, ropespec],
        out_specs=(row(A_WIDTH), row(A_WIDTH), row(B_WIDTH), tail, tail, *qkv_specs),
        out_shape=(sd(A_WIDTH, F32), sd(A_WIDTH, F32), sd(B_WIDTH, F32), tail_sd, tail_sd, *qkv_sds),
        scratch_shapes=[pltpu.VMEM((tm, w_in.shape[1]), F32),
                        pltpu.VMEM((3 * (C_WIDTH // LANES), tm, LANES), F32),
                        pltpu.VMEM((3 * (C_WIDTH // LANES), tm, LANES), F32)],
        compiler_params=_params("arbitrary", "arbitrary"),
        name="in_projection",
    )(x, g1, w_in, w_s, bs_rows, *rope)


def _s5_readout(hb, uperm, gs, wc_ref, d_ref, wg_ref, bg_ref):
    for j in range(S5_BLOCKS):
        cols = slice(j * LANES, (j + 1) * LANES)
        y = jnp.dot(hb[j], wc_ref[j], preferred_element_type=F32) + d_ref[:, cols] * uperm[:, cols]
        gs[:, cols] = jax.nn.gelu(y)
    g = gs[...]
    gs[...] = g * jax.nn.sigmoid(_dot(g, wg_ref[...]) + bg_ref[...])


def _s5_prompt_kernel(u_ref, wb_ref, wc_ref, pr_ref, pi_ref, d_ref, wg_ref, bg_ref,
                      o_ref, hr_ref, hi_ref,
                      upad, uperm, xs, hb, cs, carry, gs, *, seglen):
    pitch = seglen + SUBLANES

    @pl.when(pl.program_id(1) == 0)
    def _():
        carry[...] = jnp.zeros_like(carry)

    for s in range(SUBLANES):
        for p in range(S5_BLOCKS):
            upad[p, s * pitch:s * pitch + seglen, :] = u_ref[s * seglen:(s + 1) * seglen, p * LANES:(p + 1) * LANES]

    def perm(i, _):
        r = pl.multiple_of(i * SUBLANES, SUBLANES)
        for p in range(S5_BLOCKS):
            uperm[pl.ds(r, SUBLANES), p * LANES:(p + 1) * LANES] = upad[p, pl.ds(i, SUBLANES, stride=pitch), :]
        return 0
    lax.fori_loop(0, seglen, perm, 0)

    for j in range(S5_BLOCKS):
        xs[j] = _dot(uperm[:, j * LANES:(j + 1) * LANES], wb_ref[j])

    for j in range(S5_BLOCKS):
        ar = jnp.broadcast_to(pr_ref[j, 0:1, :], (SUBLANES, S5_HALF))
        ai = jnp.broadcast_to(pi_ref[j, 0:1, :], (SUBLANES, S5_HALF))

        def scan(i, hc, j=j, ar=ar, ai=ai):
            hr, hi = hc
            r = pl.multiple_of(i * SUBLANES, SUBLANES)
            nhr = ar * hr - ai * hi + xs[j, pl.ds(r, SUBLANES), 0:S5_HALF]
            nhi = ar * hi + ai * hr + xs[j, pl.ds(r, SUBLANES), S5_HALF:2 * S5_HALF]
            xs[j, pl.ds(r, SUBLANES), 0:S5_HALF] = nhr
            xs[j, pl.ds(r, SUBLANES), S5_HALF:2 * S5_HALF] = nhi
            return nhr, nhi
        z = jnp.zeros((SUBLANES, S5_HALF), F32)
        er, ei = lax.fori_loop(0, seglen, scan, (z, z))

        a64r, a64i = pr_ref[j, seglen - 1:seglen, :], pi_ref[j, seglen - 1:seglen, :]
        cr, ci = carry[j, 0, 0:1, :], carry[j, 1, 0:1, :]
        for s in range(SUBLANES):
            cs[j, 0, s:s + 1, :] = cr
            cs[j, 1, s:s + 1, :] = ci
            cr, ci = (a64r * cr - a64i * ci + er[s:s + 1, :], a64r * ci + a64i * cr + ei[s:s + 1, :])
        carry[j, 0, 0:1, :] = cr
        carry[j, 1, 0:1, :] = ci
        hr_ref[j] = cr
        hi_ref[j] = ci

        cr8, ci8 = cs[j, 0], cs[j, 1]

        def fix(i2, _, j=j, cr8=cr8, ci8=ci8):
            r = pl.multiple_of(i2 * 2 * SUBLANES, 2 * SUBLANES)
            hrs, his = [], []
            for t in range(2):
                i = i2 * 2 + t
                pr, pi = pr_ref[j, pl.ds(i, 1), :], pi_ref[j, pl.ds(i, 1), :]
                rows = pl.ds(r + t * SUBLANES, SUBLANES)
                hrs.append(xs[j, rows, 0:S5_HALF] + pr * cr8 - pi * ci8)
                his.append(xs[j, rows, S5_HALF:2 * S5_HALF] + pr * ci8 + pi * cr8)
            hb[j, pl.ds(r, 2 * SUBLANES), 0:S5_HALF] = jnp.concatenate(hrs, 0).astype(BF16)
            hb[j, pl.ds(r, 2 * SUBLANES), S5_HALF:2 * S5_HALF] = jnp.concatenate(his, 0).astype(BF16)
            return 0
        lax.fori_loop(0, seglen // 2, fix, 0, unroll=2)

    _s5_readout(hb, uperm, gs, wc_ref, d_ref, wg_ref, bg_ref)

    def unperm(i, _):
        r = pl.multiple_of(i * SUBLANES, SUBLANES)
        for p in range(S5_BLOCKS):
            upad[p, pl.ds(i, SUBLANES, stride=pitch), :] = gs[pl.ds(r, SUBLANES), p * LANES:(p + 1) * LANES]
        return 0
    lax.fori_loop(0, seglen, unperm, 0)
    for s in range(SUBLANES):
        for p in range(S5_BLOCKS):
            o_ref[s * seglen:(s + 1) * seglen, p * LANES:(p + 1) * LANES] = upad[p, s * pitch:s * pitch + seglen, :]


def s5_prompt(u, wb, wc, prl, pil, d_skip, w_glu, b_glu, *, seglen):
    bsz, s, _ = u.shape
    t = SUBLANES * seglen
    pitch = seglen + SUBLANES
    full = lambda shp: pl.BlockSpec(shp, lambda b, c: (0,) * len(shp))
    st = pl.BlockSpec((None, S5_BLOCKS, 1, S5_HALF), lambda b, c: (b, 0, 0, 0))
    rows = pl.BlockSpec((None, t, B_WIDTH), lambda b, c: (b, c, 0))
    return pl.pallas_call(
        functools.partial(_s5_prompt_kernel, seglen=seglen),
        grid=(bsz, s // t),
        in_specs=[rows, full(wb.shape), full(wc.shape), full(prl.shape), full(pil.shape),
                  full((1, B_WIDTH)), full(w_glu.shape), full((1, B_WIDTH))],
        out_specs=(rows, st, st),
        out_shape=(jax.ShapeDtypeStruct((bsz, s, B_WIDTH), F32),
                   jax.ShapeDtypeStruct((bsz, S5_BLOCKS, 1, S5_HALF), F32),
                   jax.ShapeDtypeStruct((bsz, S5_BLOCKS, 1, S5_HALF), F32)),
        scratch_shapes=[pltpu.VMEM((S5_BLOCKS, SUBLANES * pitch, LANES), F32),
                        pltpu.VMEM((t, B_WIDTH), F32),
                        pltpu.VMEM((S5_BLOCKS, t, 2 * S5_HALF), F32),
                        pltpu.VMEM((S5_BLOCKS, t, 2 * S5_HALF), BF16),
                        pltpu.VMEM((S5_BLOCKS, 2, SUBLANES, S5_HALF), F32),
                        pltpu.VMEM((S5_BLOCKS, 2, SUBLANES, S5_HALF), F32),
                        pltpu.VMEM((t, B_WIDTH), F32)],
        compiler_params=_params("arbitrary", "arbitrary"),
        name="s5_prompt",
    )(u, wb, wc, prl, pil, d_skip, w_glu, b_glu)


def _s5_sample_kernel(u_ref, h0r_ref, h0i_ref, wb_ref, wc_ref, pr_ref, pi_ref, d_ref, wg_ref, bg_ref,
                      o_ref, hr_ref, hi_ref,
                      upad, uperm, xs, hb, gs, *, nseq, steps):
    nblk = nseq // SUBLANES
    for p in range(S5_BLOCKS):
        upad[p] = u_ref[:, p * LANES:(p + 1) * LANES]
    for st in range(steps):
        for bb in range(nblk):
            r = (st * nblk + bb) * SUBLANES
            for p in range(S5_BLOCKS):
                uperm[r:r + SUBLANES, p * LANES:(p + 1) * LANES] = (
                    upad[p, pl.ds(bb * SUBLANES * steps + st, SUBLANES, stride=steps), :])
    for j in range(S5_BLOCKS):
        xs[j] = _dot(uperm[:, j * LANES:(j + 1) * LANES], wb_ref[j])
    for j in range(S5_BLOCKS):
        ar = jnp.broadcast_to(pr_ref[j, 0:1, :], (SUBLANES, S5_HALF))
        ai = jnp.broadcast_to(pi_ref[j, 0:1, :], (SUBLANES, S5_HALF))
        for bb in range(nblk):
            seqs = slice(bb * SUBLANES, (bb + 1) * SUBLANES)
            hr, hi = h0r_ref[j, seqs, :], h0i_ref[j, seqs, :]
            for st in range(steps):
                r = (st * nblk + bb) * SUBLANES
                hr, hi = (ar * hr - ai * hi + xs[j, r:r + SUBLANES, 0:S5_HALF],
                          ar * hi + ai * hr + xs[j, r:r + SUBLANES, S5_HALF:2 * S5_HALF])
                xs[j, r:r + SUBLANES, 0:S5_HALF] = hr
                xs[j, r:r + SUBLANES, S5_HALF:2 * S5_HALF] = hi
            hr_ref[j, seqs, :] = hr
            hi_ref[j, seqs, :] = hi
        hb[j] = xs[j].astype(BF16)
    _s5_readout(hb, uperm, gs, wc_ref, d_ref, wg_ref, bg_ref)
    for st in range(steps):
        for bb in range(nblk):
            r = (st * nblk + bb) * SUBLANES
            for p in range(S5_BLOCKS):
                upad[p, pl.ds(bb * SUBLANES * steps + st, SUBLANES, stride=steps), :] = (
                    gs[r:r + SUBLANES, p * LANES:(p + 1) * LANES])
    for p in range(S5_BLOCKS):
        o_ref[:, p * LANES:(p + 1) * LANES] = upad[p]


def s5_sample(u, h0r, h0i, wb, wc, prl, pil, d_skip, w_glu, b_glu, *, nseq, steps):
    n = nseq * steps
    st_sd = jax.ShapeDtypeStruct((S5_BLOCKS, nseq, S5_HALF), F32)
    return pl.pallas_call(
        functools.partial(_s5_sample_kernel, nseq=nseq, steps=steps),
        out_shape=(jax.ShapeDtypeStruct((n, B_WIDTH), F32), st_sd, st_sd),
        scratch_shapes=[pltpu.VMEM((S5_BLOCKS, n, LANES), F32),
                        pltpu.VMEM((n, B_WIDTH), F32),
                        pltpu.VMEM((S5_BLOCKS, n, 2 * S5_HALF), F32),
                        pltpu.VMEM((S5_BLOCKS, n, 2 * S5_HALF), BF16),
                        pltpu.VMEM((n, B_WIDTH), F32)],
        compiler_params=pltpu.CompilerParams(vmem_limit_bytes=VMEM_LIMIT),
        name="s5_sample",
    )(u, h0r, h0i, wb, wc, prl, pil, d_skip, w_glu, b_glu)


def _attn_prompt_kernel(q_ref, kp_ref, kc_ref, vp_ref, vc_ref, o_ref, st_ref, kx, vx):
    tq = q_ref.shape[0]
    t = pl.program_id(2)
    kx[0:CHUNK, :] = kp_ref[...]
    kx[CHUNK:CHUNK + tq, :] = kc_ref[...]
    vx[0:CHUNK, :] = vp_ref[...]
    vx[CHUNK:CHUNK + tq, :] = vc_ref[...]
    r = lax.broadcasted_iota(jnp.int32, (CHUNK, 2 * CHUNK), 0)
    c = lax.broadcasted_iota(jnp.int32, (CHUNK, 2 * CHUNK), 1)
    band = (c >= r) & (c <= r + CHUNK)
    lane = lax.broadcasted_iota(jnp.int32, (CHUNK, LANES), 1)
    lo_half = lane < HEAD_DIM
    for u in range(tq // CHUNK):
        rows = slice(u * CHUNK, (u + 1) * CHUNK)
        keys = slice(u * CHUNK, (u + 2) * CHUNK)
        if u == 0:
            valid = band & (c >= jnp.where(t == 0, CHUNK, 0))
        else:
            valid = band
        bias = jnp.where(valid, 0.0, NEG_INF)
        st = jnp.zeros((CHUNK, LANES), F32)
        for p in range(C_WIDTH // LANES):
            cols = slice(p * LANES, (p + 1) * LANES)
            qp = q_ref[rows, cols]
            kpair = kx[keys, cols]
            vpair = vx[keys, cols]
            outs = []
            for hh in range(2):
                qm = jnp.where(lo_half if hh == 0 else ~lo_half, qp, jnp.zeros_like(qp))
                sc = lax.dot_general(qm, kpair, (((1,), (1,)), ((), ())), preferred_element_type=F32) + bias
                m = jnp.max(sc, -1, keepdims=True)
                pe = jnp.exp(sc - m)
                l = jnp.sum(pe, -1, keepdims=True)
                outs.append(jnp.dot(pe.astype(BF16), vpair, preferred_element_type=F32) / l)
                st = jnp.where(lane == 2 * p + hh, m + jnp.log(l), st)
            o_ref[rows, cols] = jnp.where(lo_half, outs[0], outs[1])
        st_ref[rows, :] = st


def attn_prompt(q, k, v, dil, *, tq):
    bsz, ln, _ = q.shape
    tq = min(tq, ln)
    cur = pl.BlockSpec((None, tq, C_WIDTH), lambda b, r, t: (b, t, r))
    prev = pl.BlockSpec((None, CHUNK, C_WIDTH), lambda b, r, t: (b, jnp.maximum(t * (tq // CHUNK) - 1, 0), r))
    return pl.pallas_call(
        _attn_prompt_kernel,
        grid=(bsz, dil, ln // tq),
        in_specs=[cur, prev, cur, prev, cur],
        out_specs=(cur, pl.BlockSpec((None, tq, LANES), lambda b, r, t: (b, t, r))),
        out_shape=(jax.ShapeDtypeStruct((bsz, ln, dil * C_WIDTH), F32),
                   jax.ShapeDtypeStruct((bsz, ln, dil * LANES), F32)),
        scratch_shapes=[pltpu.VMEM((CHUNK + tq, C_WIDTH), BF16), pltpu.VMEM((CHUNK + tq, C_WIDTH), BF16)],
        compiler_params=_params("arbitrary", "arbitrary", "arbitrary"),
        name=f"attn_prompt_d{dil}",
    )(q, k, k, v, v)


def _attn_sample_kernel(q_ref, kn_ref, vn_ref, ck_ref, cv_ref, o_ref, *, steps):
    cw = ck_ref.shape[-1]

    def mult(delta):
        cnt = jnp.zeros(delta.shape, F32)
        for window, dil in DILATIONS:
            cnt = cnt + jnp.where((delta >= 0) & (delta <= window) & ((delta & (dil - 1)) == 0), 1.0, 0.0)
        return cnt

    cnt_c = mult(cw + lax.broadcasted_iota(jnp.int32, (steps, cw), 0)
                 - lax.broadcasted_iota(jnp.int32, (steps, cw), 1))
    cnt_n = mult(lax.broadcasted_iota(jnp.int32, (steps, CHUNK), 0)
                 - lax.broadcasted_iota(jnp.int32, (steps, CHUNK), 1))
    live_c, live_n = cnt_c > 0, cnt_n > 0
    zpad = jnp.zeros((CHUNK - steps, C_WIDTH), F32)
    kn = jnp.concatenate([kn_ref[...], zpad], 0).astype(BF16)
    vn = jnp.concatenate([vn_ref[...], zpad], 0).astype(BF16)
    q = q_ref[...].astype(BF16)
    nt_dims = (((1,), (1,)), ((), ()))
    for hh in range(C_HEADS):
        cols = slice(hh * HEAD_DIM, (hh + 1) * HEAD_DIM)
        qh = q[:, cols]
        sc = jnp.dot(qh, ck_ref[hh].astype(BF16), preferred_element_type=F32)
        sn = lax.dot_general(qh, kn[:, cols], nt_dims, preferred_element_type=F32)
        m = jnp.maximum(jnp.max(jnp.where(live_c, sc, NEG_INF), -1, keepdims=True),
                        jnp.max(jnp.where(live_n, sn, NEG_INF), -1, keepdims=True))
        ec = cnt_c * jnp.exp(jnp.where(live_c, sc - m, NEG_INF))
        en = cnt_n * jnp.exp(jnp.where(live_n, sn - m, NEG_INF))
        l = jnp.sum(ec, -1, keepdims=True) + jnp.sum(en, -1, keepdims=True)
        acc = (lax.dot_general(ec.astype(BF16), cv_ref[hh].astype(BF16), nt_dims, preferred_element_type=F32)
               + jnp.dot(en.astype(BF16), vn[:, cols], preferred_element_type=F32))
        o_ref[:, cols] = acc / l


def attn_sample(q, k_new, v_new, cache_kt, cache_vt, layer, *, steps):
    _, nseq, _, _, cw = cache_kt.shape
    rows = pl.BlockSpec((steps, C_WIDTH), lambda b: (b, 0))
    cache = pl.BlockSpec((None, None, C_HEADS, HEAD_DIM, cw), lambda b: (layer, b, 0, 0, 0))
    return pl.pallas_call(
        functools.partial(_attn_sample_kernel, steps=steps),
        grid=(nseq,),
        in_specs=[rows, rows, rows, cache, cache],
        out_specs=rows,
        out_shape=jax.ShapeDtypeStruct((nseq * steps, C_WIDTH), F32),
        compiler_params=_params("arbitrary"),
        name="attn_sample",
    )(q, k_new, v_new, cache_kt, cache_vt)


FF_ROWS = 128
FF_PIECES = 4


def _residual_and_norm(x_ref, oa_ref, ob_ref, oc, wo_ref, g2_ref, y_ref, xn_scr):
    mix = (_dot(oa_ref[...], wo_ref[0:A_WIDTH, :]) + _dot(ob_ref[...], wo_ref[A_WIDTH:A_WIDTH + B_WIDTH, :])
           + _dot(oc, wo_ref[A_WIDTH + B_WIDTH:, :]))
    x1 = x_ref[...] + mix
    y_ref[...] = x1
    xn_scr[...] = _rms(x1, g2_ref[...]).astype(BF16)


def _outproj_prompt_kernel(x_ref, oa_ref, ob_ref, *rest, tiles_per_seq, dils, final_norm):
    nb = len(dils)
    o_refs, st_refs = rest[:nb], rest[nb:2 * nb]
    (e_ref, wo_ref, g2_ref, wup_ref, cw_ref, cb_ref, wd_ref, fg_ref,
     y_ref, cg_ref, cu_ref, xn_scr, carry_scr, unp_scr, hs0, hs1, act0, act1) = rest[2 * nb:]
    hs_scr, act_scr = (hs0, hs1), (act0, act1)
    i = pl.program_id(0)
    tm = x_ref.shape[0]
    nslab = C_WIDTH // LANES

    outs, sts = [], []
    slot = 0
    for dil, o_ref, s_ref in zip(dils, o_refs, st_refs):
        if dil == 1:
            outs.append(o_ref[...])
            sts.append(s_ref[...])
            continue
        for r in range(dil):
            rows = pl.ds(r, tm // dil, stride=dil)
            for p in range(nslab):
                c0 = r * C_WIDTH + p * LANES
                unp_scr[slot + p, rows, :] = o_ref[:, c0:c0 + LANES]
            unp_scr[slot + nslab, rows, :] = s_ref[:, r * LANES:(r + 1) * LANES]
        outs.append(jnp.concatenate([unp_scr[slot + p] for p in range(nslab)], axis=1))
        sts.append(unp_scr[slot + nslab])
        slot += nslab + 1
    mx = functools.reduce(jnp.maximum, sts)
    ws = [jnp.exp(s - mx) for s in sts]
    tot = functools.reduce(lambda a, b: a + b, ws)
    oc = jnp.zeros((tm, C_WIDTH), F32)
    for w, o in zip(ws, outs):
        alpha = w / tot
        hi = alpha.astype(BF16)
        lo = (alpha - hi.astype(F32)).astype(BF16)
        wide = jnp.dot(jnp.concatenate([hi, lo], axis=1), e_ref[...], preferred_element_type=F32)
        oc = oc + wide * o
    _residual_and_norm(x_ref, oa_ref, ob_ref, oc, wo_ref, g2_ref, y_ref, xn_scr)

    @pl.when((i % tiles_per_seq) == 0)
    def _():
        carry_scr[...] = jnp.zeros_like(carry_scr)

    nhs = 2 * FF_CHUNK // LANES
    gate = FF_CHUNK // LANES
    piece = tm // FF_PIECES

    def up(jj, slot, pc):
        xr = xn_scr[pc * piece:(pc + 1) * piece, :]
        rows = slice(SUBLANES + pc * piece, SUBLANES + (pc + 1) * piece)
        for half in range(2):
            h = jnp.dot(xr, wup_ref[half * N_FF + jj], preferred_element_type=F32)
            for k in range(gate):
                hs_scr[slot][2 * (half * gate + k), rows, :] = h[:, k * LANES:(k + 1) * LANES]

    def down(jj, slot, pc):
        rows = slice(pc * piece, (pc + 1) * piece)
        y_ref[rows, :] += jnp.dot(act_scr[slot][rows, :], wd_ref[jj], preferred_element_type=F32)

    def conv_taps(jj, slot):
        for k in range(nhs):
            hs_scr[slot][2 * k, 0:SUBLANES, :] = carry_scr[jj, k]
        cwj = jnp.concatenate([cw_ref[jj], cw_ref[N_FF + jj]], axis=1)
        cbj = jnp.concatenate([cb_ref[jj], cb_ref[N_FF + jj]], axis=1)
        slabs = lambda row: jnp.stack([row[:, k * LANES:(k + 1) * LANES] for k in range(nhs)])
        return [slabs(cwj[t:t + 1]) for t in range(3)], slabs(cbj)

    def conv(jj, slot, pc, taps, bias):
        every_other = pl.ds(0, nhs, stride=2)
        for r in range(pc * piece // FF_ROWS, (pc + 1) * piece // FF_ROWS):
            shifted = lambda back: hs_scr[slot][every_other, pl.ds(SUBLANES - back + r * FF_ROWS, FF_ROWS), :]
            hc = bias + (shifted(2) * taps[0] + shifted(1) * taps[1] + shifted(0) * taps[2])
            act = jax.nn.gelu(hc[:gate]) * hc[gate:]
            for k in range(gate):
                act_scr[slot][r * FF_ROWS:(r + 1) * FF_ROWS, k * LANES:(k + 1) * LANES] = act[k].astype(BF16)

    def conv_tail(jj, slot):
        for k in range(nhs):
            last = hs_scr[slot][2 * k, tm:tm + SUBLANES, :]
            carry_scr[jj, k] = last
            if k < gate:
                cg_ref[0, jj, :, k * LANES:(k + 1) * LANES] = last
            else:
                cu_ref[0, jj, :, (k - gate) * LANES:(k - gate + 1) * LANES] = last

    def stage(j, slot, do_down, do_up):
        taps, bias = conv_taps(j, slot)
        for pc in range(FF_PIECES):
            conv(j, slot, pc, taps, bias)
            if do_down:
                down(j - 1, 1 - slot, pc)
            if do_up:
                up(j + 1, 1 - slot, pc)
        conv_tail(j, slot)

    for pc in range(FF_PIECES):
        up(0, 0, pc)
    stage(0, 0, False, True)

    def steady(n, _):
        stage(2 * n + 1, 1, True, True)
        stage(2 * n + 2, 0, True, True)
        return 0
    n_pairs = (N_FF - 2) // 2
    lax.fori_loop(0, n_pairs, steady, 0)
    for j in range(2 * n_pairs + 1, N_FF):
        stage(j, j % 2, True, j + 1 < N_FF)
    for pc in range(FF_PIECES):
        down(N_FF - 1, (N_FF - 1) % 2, pc)

    if final_norm:
        y_ref[...] = _rms(y_ref[...], fg_ref[...])


def _outproj_sample_kernel(x_ref, oa_ref, ob_ref, oc_ref, wo_ref, g2_ref, wug_ref, wuu_ref, cwg_ref, cwu_ref,
                           cbg_ref, cbu_ref, wd_ref, fg_ref, bufg_ref, bufu_ref,
                           y_ref, cg_ref, cu_ref, xn_scr, hs_scr, *, nseq, seq_len, final_norm):
    j = pl.program_id(1)
    tm = x_ref.shape[0]

    @pl.when(j == 0)
    def _():
        _residual_and_norm(x_ref, oa_ref, ob_ref, oc_ref[...], wo_ref, g2_ref, y_ref, xn_scr)

    xn = xn_scr[...]
    hg = jnp.dot(xn, wug_ref[...], preferred_element_type=F32)
    hu = jnp.dot(xn, wuu_ref[...], preferred_element_type=F32)
    lo, hi_ = SUBLANES, SUBLANES + seq_len
    hs_scr[:, lo:hi_, 0:FF_CHUNK] = hg.reshape(nseq, seq_len, FF_CHUNK)
    hs_scr[:, lo:hi_, FF_CHUNK:] = hu.reshape(nseq, seq_len, FF_CHUNK)
    hs_scr[:, lo - 2:lo, 0:FF_CHUNK] = bufg_ref[...]
    hs_scr[:, lo - 2:lo, FF_CHUNK:] = bufu_ref[...]
    cg_ref[...] = hs_scr[:, hi_ - SUBLANES:hi_, 0:FF_CHUNK]
    cu_ref[...] = hs_scr[:, hi_ - SUBLANES:hi_, FF_CHUNK:]

    cw = jnp.concatenate([cwg_ref[...], cwu_ref[...]], axis=1)
    cb = jnp.concatenate([cbg_ref[...], cbu_ref[...]], axis=1)
    hc = cb + (hs_scr[:, lo - 2:hi_ - 2, :] * cw[0:1] + hs_scr[:, lo - 1:hi_ - 1, :] * cw[1:2]
               + hs_scr[:, lo:hi_, :] * cw[2:3])
    hc = hc.reshape(tm, 2 * FF_CHUNK)
    act = jax.nn.gelu(hc[:, :FF_CHUNK]) * hc[:, FF_CHUNK:]
    y_ref[...] += _dot(act, wd_ref[...])

    if final_norm:
        @pl.when(j == pl.num_programs(1) - 1)
        def _():
            y_ref[...] = _rms(y_ref[...], fg_ref[...])


def ffn_layouts(w_up, conv_w, conv_b, w_down):
    chunks = lambda a: jnp.transpose(a.reshape(a.shape[0], 2 * N_FF, FF_CHUNK), (1, 0, 2))
    return (chunks(w_up), chunks(conv_w), chunks(conv_b.reshape(1, -1)), w_down.reshape(N_FF, FF_CHUNK, D_MODEL))


def out_projection_prompt(x, oa, ob, o_list, st_list, dils, expand, ffw, *, tm, tiles_per_seq, final_norm):
    n = x.shape[0]
    nt = n // tm
    row = lambda w: pl.BlockSpec((tm, w), lambda i: (i, 0))
    perm = lambda w: [pl.BlockSpec((tm // d, d * w), lambda i: (i, 0)) for d in dils]
    resident = lambda a: pl.BlockSpec(a.shape, lambda i: (0,) * a.ndim, pipeline_mode=pl.Buffered(1))
    conv_spec = pl.BlockSpec((1, N_FF, SUBLANES, FF_CHUNK), lambda i: (i, 0, 0, 0))
    conv_sd = jax.ShapeDtypeStruct((nt, N_FF, SUBLANES, FF_CHUNK), F32)
    n_unp = sum(C_WIDTH // LANES + 1 for d in dils if d > 1)
    nhs = 2 * FF_CHUNK // LANES
    return pl.pallas_call(
        functools.partial(_outproj_prompt_kernel, tiles_per_seq=tiles_per_seq, dils=tuple(dils),
                          final_norm=final_norm),
        grid=(nt,),
        in_specs=[row(D_MODEL), row(A_WIDTH), row(B_WIDTH)] + perm(C_WIDTH) + perm(LANES)
                 + [resident(expand)] + [resident(a) for a in ffw],
        out_specs=(row(D_MODEL), conv_spec, conv_spec),
        out_shape=(jax.ShapeDtypeStruct((n, D_MODEL), F32), conv_sd, conv_sd),
        scratch_shapes=[pltpu.VMEM((tm, D_MODEL), BF16),
                        pltpu.VMEM((N_FF, nhs, SUBLANES, LANES), F32),
                        pltpu.VMEM((n_unp, tm, LANES), F32),
                        pltpu.VMEM((2 * nhs, SUBLANES + tm, LANES), F32),
                        pltpu.VMEM((2 * nhs, SUBLANES + tm, LANES), F32),
                        pltpu.VMEM((tm, FF_CHUNK), BF16),
                        pltpu.VMEM((tm, FF_CHUNK), BF16)],
        compiler_params=_params("arbitrary"),
        name="out_projection_prompt",
    )(x, oa, ob, *o_list, *st_list, expand, *ffw)


def out_projection_sample(x, oa, ob, oc, ffw, bufs, *, nseq, seq_len, final_norm):
    n = x.shape[0]
    row = lambda w: pl.BlockSpec((n, w), lambda i, j: (0, 0))
    full = lambda a: pl.BlockSpec(a.shape, lambda i, j: (0,) * a.ndim)
    w_out, g2, w_up, conv_w, conv_b, w_down, fg = ffw
    chunk = lambda a, off: pl.BlockSpec((None,) + a.shape[1:], lambda i, j: (off + j, 0, 0))
    wspecs = [full(w_out), full(g2), chunk(w_up, 0), chunk(w_up, N_FF), chunk(conv_w, 0), chunk(conv_w, N_FF),
              chunk(conv_b, 0), chunk(conv_b, N_FF), chunk(w_down, 0), full(fg)]
    wargs = [w_out, g2, w_up, w_up, conv_w, conv_w, conv_b, conv_b, w_down, fg]
    conv_spec = pl.BlockSpec((nseq, SUBLANES, FF_CHUNK), lambda i, j: (0, 0, j))
    return pl.pallas_call(
        functools.partial(_outproj_sample_kernel, nseq=nseq, seq_len=seq_len, final_norm=final_norm),
        grid=(1, N_FF),
        in_specs=[row(D_MODEL), row(A_WIDTH), row(B_WIDTH), row(C_WIDTH)] + wspecs
                 + [pl.BlockSpec((nseq, 2, FF_CHUNK), lambda i, j: (0, 0, j)),
                    pl.BlockSpec((nseq, 2, FF_CHUNK), lambda i, j: (0, 0, N_FF + j))],
        out_specs=(row(D_MODEL), conv_spec, conv_spec),
        out_shape=(jax.ShapeDtypeStruct((n, D_MODEL), F32),
                   jax.ShapeDtypeStruct((nseq, SUBLANES, D_FF), F32),
                   jax.ShapeDtypeStruct((nseq, SUBLANES, D_FF), F32)),
        scratch_shapes=[pltpu.VMEM((n, D_MODEL), BF16),
                        pltpu.VMEM((nseq, SUBLANES + seq_len, 2 * FF_CHUNK), F32)],
        compiler_params=_params("arbitrary", "arbitrary"),
        name="out_projection_sample",
    )(x, oa, ob, oc, *wargs, bufs, bufs)


PROMPT_TM = 512
S5_SEGLEN = 64
ATTN_TQ = 512


def kernel(x_prompt, x_sample, cache_c_k, cache_c_v, state_ssm_re, state_ssm_im, state_ffn_conv, norm1_g, w_in, w_s, b_s, ssm_lam_re, ssm_lam_im, ssm_log_dt, ssm_b_re, ssm_b_im, ssm_c_re, ssm_c_im, ssm_d, w_glu, b_glu, w_out, norm2_g, w_up, conv_w, conv_b, w_down, final_g):
    bp, sp, _ = x_prompt.shape
    nseq, steps, _ = x_sample.shape
    ns = nseq * steps
    cw_p = min(MAX_WINDOW, sp)
    cw_s = cache_c_k.shape[2]

    rope_p = rope_tables(sp, 0)
    rope_s = tuple(jnp.tile(t, (nseq, 1)) for t in rope_tables(steps, PAST_LEN))
    bbr, bbi, pr, pi, ncim = s5_params(ssm_lam_re, ssm_lam_im, ssm_log_dt, ssm_b_re, ssm_b_im, ssm_c_im, S5_SEGLEN)

    w_in_b, w_out_b, w_up_b, w_down_b, w_glu_b = (w.astype(BF16) for w in (w_in, w_out, w_up, w_down, w_glu))
    expand = (jnp.arange(2 * LANES)[:, None] % LANES == jnp.arange(C_WIDTH)[None, :] // HEAD_DIM).astype(BF16)
    bs_p = jnp.repeat(jnp.transpose(b_s, (0, 2, 1)), HEAD_DIM, axis=2)
    bs_s = jnp.tile(bs_p[:, :steps], (1, CHUNK // steps, 1))
    ws_s = jnp.tile(w_s[:, :, :steps, :steps], (1, 1, CHUNK // steps, CHUNK // steps))
    row1 = lambda a: a.reshape(1, -1)

    cache_kt = jnp.transpose(cache_c_k, (0, 1, 3, 4, 2))
    cache_vt = jnp.transpose(cache_c_v, (0, 1, 3, 4, 2))
    dils = tuple(d for _, d in DILATIONS)

    hp = x_prompt
    hs = x_sample.reshape(1, ns, D_MODEL)
    outs = [[] for _ in range(11)]
    for l in range(DEPTH):
        last = l == DEPTH - 1
        wb, wc, prl, pil = s5_layouts(bbr[l], bbi[l], ssm_c_re[l], ncim[l], pr[l], pi[l])
        s5w = (wb, wc, prl, pil, row1(ssm_d[l]), w_glu_b[l], row1(b_glu[l]))
        ffw = (w_out_b[l], row1(norm2_g[l]), *ffn_layouts(w_up_b[l], conv_w[l], conv_b[l], w_down_b[l]),
               row1(final_g))

        oa, _, ub, kf, vf, *qkv = in_projection(hp, row1(norm1_g[l]), w_in_b[l], w_s[l], bs_p[l], rope_p,
                                                tm=PROMPT_TM, seq_rows=CHUNK, tail_rows=cw_p, dils=dils)
        nd = len(dils)
        ob, hr, hi = s5_prompt(ub, *s5w, seglen=S5_SEGLEN)
        o_list, st_list = zip(*[attn_prompt(qkv[n], qkv[nd + n], qkv[2 * nd + n], dil, tq=ATTN_TQ)
                                for n, dil in enumerate(dils)])
        flat = lambda a: a.reshape(a.shape[0] * a.shape[1], a.shape[2])
        y, cg, cu = out_projection_prompt(flat(hp), flat(oa), flat(ob), [flat(o) for o in o_list],
                                          [flat(s) for s in st_list], dils, expand, ffw,
                                          tm=PROMPT_TM, tiles_per_seq=sp // PROMPT_TM, final_norm=last)
        hp = y.reshape(bp, sp, D_MODEL)
        outs[0].append(kf.reshape(bp, cw_p, C_HEADS, HEAD_DIM))
        outs[1].append(vf.reshape(bp, cw_p, C_HEADS, HEAD_DIM))
        outs[4].append(hr.reshape(bp, B_GROUPS, SSM_STATE))
        outs[5].append(hi.reshape(bp, B_GROUPS, SSM_STATE))
        seq_end = lambda a: jnp.transpose(a.reshape(bp, sp // PROMPT_TM, N_FF, SUBLANES, FF_CHUNK)[:, -1, :, -2:],
                                          (0, 2, 1, 3)).reshape(bp, 2, D_FF)
        outs[8].append(jnp.concatenate([seq_end(cg), seq_end(cu)], -1))

        oa, vn, ub, kf, vf, q, k, v = in_projection(hs, row1(norm1_g[l]), w_in_b[l], ws_s[l], bs_s[l], rope_s,
                                                    tm=ns, seq_rows=steps, tail_rows=ns, dils=(1,))
        h0r = jnp.transpose(state_ssm_re[l].reshape(nseq, S5_BLOCKS, S5_HALF), (1, 0, 2))
        h0i = jnp.transpose(state_ssm_im[l].reshape(nseq, S5_BLOCKS, S5_HALF), (1, 0, 2))
        ob, hr, hi = s5_sample(ub[0], h0r, h0i, *s5w, nseq=nseq, steps=steps)
        oc = attn_sample(q[0].astype(F32), k[0].astype(F32), v[0].astype(F32), cache_kt, cache_vt, l, steps=steps)
        y, cg, cu = out_projection_sample(hs[0], oa[0], ob, oc, ffw, state_ffn_conv[l],
                                          nseq=nseq, seq_len=steps, final_norm=last)
        hs = y.reshape(1, ns, D_MODEL)
        outs[2].append(kf.reshape(nseq, steps, C_HEADS, HEAD_DIM))
        outs[3].append(vf.reshape(nseq, steps, C_HEADS, HEAD_DIM))
        outs[6].append(jnp.transpose(hr, (1, 0, 2)).reshape(nseq, B_GROUPS, SSM_STATE))
        outs[7].append(jnp.transpose(hi, (1, 0, 2)).reshape(nseq, B_GROUPS, SSM_STATE))
        outs[9].append(jnp.concatenate([cg[:, -2:], cu[:, -2:]], -1))
        outs[10].append(vn.reshape(nseq, steps, A_WIDTH))

    return (hp, hs.reshape(nseq, steps, D_MODEL)) + tuple(jnp.stack(o) for o in outs)
```

```python
import functools
import math

import jax
import jax.numpy as jnp
from jax import lax
from jax.experimental import pallas as pl
from jax.experimental.pallas import tpu as pltpu

F32 = jnp.float32
BF16 = jnp.bfloat16

D_MODEL = 1024
DEPTH = 4
PAST_LEN = 8192
HEAD_DIM = 64
A_WIDTH = 256
B_WIDTH = 384
C_WIDTH = 384
A_HEADS = 4
C_HEADS = 6
CHUNK = 128
SSM_GROUP = 16
B_GROUPS = 24
SSM_STATE = 64
DILATIONS = ((128, 1), (512, 4), (2048, 16))
MAX_WINDOW = 2048
ROPE_THETA = 500000.0
ROT_DIM = 16
D_FF = 2816
EPS = 1e-6
NEG_INF = -1e30

O_A, O_B, O_Q, O_K, O_V = 0, 2 * A_WIDTH, 896, 1280, 1664

LANES = 128
SUBLANES = 8
S5_BLOCKS = B_WIDTH // LANES
S5_HALF = 512
FF_CHUNK = 256
N_FF = D_FF // FF_CHUNK
VMEM_LIMIT = 56 * 1024 * 1024


def _dot(a, b):
    return jnp.dot(a.astype(BF16), b.astype(BF16), preferred_element_type=F32)


def _rms(x, g):
    return x * lax.rsqrt(jnp.mean(x * x, -1, keepdims=True) + EPS) * g


def _params(*sem):
    return pltpu.CompilerParams(dimension_semantics=sem, vmem_limit_bytes=VMEM_LIMIT)


def _rope_tables_kernel(c_ref, a_ref, b_ref, *, pos0):
    n = c_ref.shape[0]
    pos = (lax.broadcasted_iota(jnp.int32, (n, LANES), 0) + (pos0 + pl.program_id(0) * n)).astype(F32)
    d = lax.broadcasted_iota(jnp.int32, (n, LANES), 1) & (HEAD_DIM - 1)
    k = (d & (ROT_DIM // 2 - 1)).astype(F32)
    inv = jnp.exp(k * (-2.0 / ROT_DIM * math.log(ROPE_THETA)))
    ang = pos * inv
    cos, sin = jnp.cos(ang), jnp.sin(ang)
    c_ref[...] = jnp.where(d < ROT_DIM, cos, 1.0)
    a_ref[...] = jnp.where(d < ROT_DIM // 2, -sin, 0.0)
    b_ref[...] = jnp.where((d >= ROT_DIM // 2) & (d < ROT_DIM), sin, 0.0)


def rope_tables(n, pos0):
    sds = jax.ShapeDtypeStruct((n, LANES), F32)
    tr = min(n, 1024)
    spec = pl.BlockSpec((tr, LANES), lambda i: (i, 0))
    return pl.pallas_call(functools.partial(_rope_tables_kernel, pos0=pos0), grid=(n // tr,),
                          out_specs=(spec, spec, spec), out_shape=(sds, sds, sds),
                          compiler_params=_params("arbitrary"), name="rope_tables")()


def _s5_params_kernel(lr_ref, li_ref, ldt_ref, br_ref, bi_ref, cim_ref,
                      bbr_ref, bbi_ref, pr_ref, pi_ref, ncim_ref, *, n_pow):
    lr, li = lr_ref[...], li_ref[...]
    dt = jnp.exp(ldt_ref[...])
    mag = jnp.exp(lr * dt)
    ar, ai = mag * jnp.cos(li * dt), mag * jnp.sin(li * dt)
    nr, ni = ar - 1.0, ai
    den = lr * lr + li * li
    fr, fi = (nr * lr + ni * li) / den, (ni * lr - nr * li) / den
    br, bi = br_ref[...], bi_ref[...]
    bbr_ref[...] = fr[None] * br - fi[None] * bi
    bbi_ref[...] = fr[None] * bi + fi[None] * br
    k = (lax.broadcasted_iota(jnp.int32, (n_pow,) + lr.shape, 0) + 1).astype(F32)
    magk = jnp.exp(k * (lr * dt)[None])
    angk = k * (li * dt)[None]
    pr_ref[...] = magk * jnp.cos(angk)
    pi_ref[...] = magk * jnp.sin(angk)
    ncim_ref[...] = -cim_ref[...]


def s5_params(lam_re, lam_im, log_dt, b_re, b_im, c_im, n_pow):
    depth, g, p = lam_re.shape
    c = b_re.shape[-1]
    ldt = jnp.broadcast_to(log_dt[:, :, None], (depth, g, p))
    brt = jnp.transpose(b_re, (0, 3, 1, 2))
    bit = jnp.transpose(b_im, (0, 3, 1, 2))
    gp = pl.BlockSpec((None, g, p), lambda l: (l, 0, 0))
    cgp = pl.BlockSpec((None, c, g, p), lambda l: (l, 0, 0, 0))
    gcp = pl.BlockSpec((None, g, c, p), lambda l: (l, 0, 0, 0))
    kgp = pl.BlockSpec((None, n_pow, g, p), lambda l: (l, 0, 0, 0))
    return pl.pallas_call(
        functools.partial(_s5_params_kernel, n_pow=n_pow),
        grid=(depth,),
        in_specs=[gp, gp, gp, cgp, cgp, gcp],
        out_specs=(cgp, cgp, kgp, kgp, gcp),
        out_shape=(jax.ShapeDtypeStruct((depth, c, g, p), F32), jax.ShapeDtypeStruct((depth, c, g, p), F32),
                   jax.ShapeDtypeStruct((depth, n_pow, g, p), F32), jax.ShapeDtypeStruct((depth, n_pow, g, p), F32),
                   jax.ShapeDtypeStruct((depth, g, c, p), F32)),
        compiler_params=_params("arbitrary"),
        name="s5_params",
    )(lam_re, lam_im, ldt, brt, bit, c_im)


def _s5_lane_vec(t):
    return t.reshape(t.shape[:-2] + (S5_BLOCKS, 1, S5_HALF))


def s5_layouts(bbr, bbi, c_re, ncim, pr, pi):
    eye = jnp.eye(SUBLANES, dtype=F32)
    bb = jnp.stack([bbr, bbi], 0).reshape(2, SSM_GROUP, S5_BLOCKS, 8, SSM_STATE)
    wb = jnp.einsum('ecjgp,gh->jgcehp', bb, eye).reshape(S5_BLOCKS, LANES, 2 * S5_HALF)
    cc = jnp.stack([c_re, ncim], 0).reshape(2, S5_BLOCKS, 8, SSM_GROUP, SSM_STATE)
    wc = jnp.einsum('ejgcp,gh->jeghpc', cc, eye)
    wc = jnp.transpose(wc, (0, 1, 2, 4, 3, 5)).reshape(S5_BLOCKS, 2 * S5_HALF, LANES)
    n_pow = pr.shape[0]
    prl = jnp.transpose(pr.reshape(n_pow, S5_BLOCKS, S5_HALF), (1, 0, 2))
    pil = jnp.transpose(pi.reshape(n_pow, S5_BLOCKS, S5_HALF), (1, 0, 2))
    return wb.astype(BF16), wc.astype(BF16), prl, pil


SEG = 8
SEG_W = SEG * LANES


def _s5_seg_params_kernel(p0r_ref, p0i_ref, p1r_ref, p1i_ref, bbr_ref, bbi_ref, cre_ref, ncim_ref, cg_ref, ncg_ref,
                          k_ref, abr_ref, abi_ref, car_ref, cani_ref):
    abr = p0r_ref[...] * bbr_ref[...] - p0i_ref[...] * bbi_ref[...]
    abi = p0r_ref[...] * bbi_ref[...] + p0i_ref[...] * bbr_ref[...]
    abr_ref[...] = abr
    abi_ref[...] = abi
    car_ref[...] = cre_ref[...] * p1r_ref[...] + ncim_ref[...] * p1i_ref[...]
    cani_ref[...] = ncim_ref[...] * p1r_ref[...] - cre_ref[...] * p1i_ref[...]
    nt = (((1,), (1,)), ((), ()))
    for g in range(B_GROUPS):
        k_ref[g] = (lax.dot_general(cg_ref[g], abr[g], nt, precision=lax.Precision.HIGHEST,
                                    preferred_element_type=F32)
                    + lax.dot_general(ncg_ref[g], abi[g], nt, precision=lax.Precision.HIGHEST,
                                      preferred_element_type=F32))


def s5_seg_params(pr, pi, bbr, bbi, c_re, ncim):
    depth = pr.shape[0]
    g, c, p = B_GROUPS, SSM_GROUP, SSM_STATE
    over_c = lambda a: jnp.broadcast_to(jnp.transpose(a, (0, 2, 1, 3))[:, :, :, None, :],
                                        (depth, g, SEG, c, p)).reshape(depth, g, SEG * c, p)
    over_k = lambda a: jnp.broadcast_to(a[:, :, None], (depth, g, SEG, c, p)).reshape(depth, g, SEG * c, p)
    p0r = jnp.concatenate([jnp.ones_like(pr[:, :1]), pr[:, :SEG - 1]], 1)
    p0i = jnp.concatenate([jnp.zeros_like(pi[:, :1]), pi[:, :SEG - 1]], 1)
    bg = lambda a: jnp.transpose(a, (0, 2, 1, 3))
    big = pl.BlockSpec((None, g, SEG * c, p), lambda l: (l, 0, 0, 0))
    small = pl.BlockSpec((None, g, c, p), lambda l: (l, 0, 0, 0))
    big_sd = jax.ShapeDtypeStruct((depth, g, SEG * c, p), F32)
    return pl.pallas_call(
        _s5_seg_params_kernel,
        grid=(depth,),
        in_specs=[big] * 8 + [small, small],
        out_specs=(pl.BlockSpec((None, g, c, SEG * c), lambda l: (l, 0, 0, 0)), big, big, big, big),
        out_shape=(jax.ShapeDtypeStruct((depth, g, c, SEG * c), F32), big_sd, big_sd, big_sd, big_sd),
        compiler_params=_params("arbitrary"),
        name="s5_seg_params",
    )(over_c(p0r), over_c(p0i), over_c(pr[:, :SEG]), over_c(pi[:, :SEG]), over_k(bg(bbr)), over_k(bg(bbi)),
      over_k(c_re), over_k(ncim), c_re, ncim)


def s5_seg_layouts(kk, abr, abi, car, cani, pr, pi, d_skip):
    eye = jnp.eye(8, dtype=F32)
    c, p = SSM_GROUP, SSM_STATE
    lag = jnp.arange(SEG)[None, :] - jnp.arange(SEG)[:, None]
    kt = kk.reshape(B_GROUPS, c, SEG, c)
    kx = jnp.where((lag >= 0)[None, None, :, :, None], jnp.take(kt, jnp.clip(lag, 0, SEG - 1), axis=2), 0.0)
    x = jnp.transpose(kx, (0, 2, 4, 3, 1)).reshape(S5_BLOCKS, 8, SEG, c, SEG, c)
    t = jnp.einsum('pgacrd,gh->pagcrhd', x, eye).reshape(S5_BLOCKS, SEG_W, SEG_W)
    ab = jnp.stack([abr, abi], 0).reshape(2, B_GROUPS, SEG, c, p)[:, :, ::-1]
    ab = ab.reshape(2, S5_BLOCKS, 8, SEG, c, p)
    eb = jnp.einsum('epgacq,gh->pagcehq', ab, eye).reshape(S5_BLOCKS, SEG_W, 2 * S5_HALF)
    ca = jnp.stack([car, cani], 0).reshape(2, B_GROUPS, SEG, c, p)
    ca = jnp.transpose(ca, (0, 2, 1, 3, 4)).reshape(2, SEG, S5_BLOCKS, 8, c, p)
    gm = jnp.einsum('erpgcq,gh->pegqrhc', ca, eye).reshape(S5_BLOCKS, 2 * S5_HALF, SEG_W)
    a8r = pr[SEG - 1].reshape(S5_BLOCKS, 1, S5_HALF)
    a8i = pi[SEG - 1].reshape(S5_BLOCKS, 1, S5_HALF)
    d_t = jnp.tile(d_skip.reshape(S5_BLOCKS, 1, LANES), (1, SEG, 1)).reshape(1, S5_BLOCKS * SEG_W)
    return t.astype(BF16), eb.astype(BF16), gm.astype(BF16), a8r, a8i, d_t


def _rope_apply(x, c, a, b):
    return x * c + pltpu.roll(x, LANES - ROT_DIM // 2, 1) * a + pltpu.roll(x, ROT_DIM // 2, 1) * b


def _inproj_kernel(x_ref, g_ref, w_ref, ws_ref, bs_ref, rc_ref, ra_ref, rb_ref,
                   oa_ref, vn_ref, ub_ref, kf_ref, vf_ref, *rest, seq_rows, dils, seg_major):
    qkv_refs, proj_scr, perm_scr, perm2_scr = rest[:-3], rest[-3], rest[-2], rest[-1]
    tm = x_ref.shape[0]
    xn = _rms(x_ref[...], g_ref[...]).astype(BF16)
    proj_scr[...] = jnp.dot(xn, w_ref[...], preferred_element_type=F32)

    h = jax.nn.gelu(proj_scr[:, O_A:O_B])
    u, v = h[:, :A_WIDTH], h[:, A_WIDTH:]
    mu = jnp.mean(v, -1, keepdims=True)
    var = jnp.mean(jnp.square(v - mu), -1, keepdims=True)
    vn = (v - mu) * lax.rsqrt(var + EPS)
    vn_ref[...] = vn
    ri = lax.broadcasted_iota(jnp.int32, (CHUNK, CHUNK), 0)
    ci = lax.broadcasted_iota(jnp.int32, (CHUNK, CHUNK), 1)
    keep = (ci <= ri) & ((ri // seq_rows) == (ci // seq_rows))
    lane = lax.broadcasted_iota(jnp.int32, (CHUNK, LANES), 1)
    wm = [jnp.where(keep, ws_ref[hh], 0.0).astype(BF16) for hh in range(A_HEADS)]
    vnb = vn.astype(BF16)
    for c in range(tm // CHUNK):
        rows = slice(c * CHUNK, (c + 1) * CHUNK)
        for p in range(A_WIDTH // LANES):
            cols = slice(p * LANES, (p + 1) * LANES)
            vp = vnb[rows, cols]
            m0 = jnp.dot(wm[2 * p], vp, preferred_element_type=F32)
            m1 = jnp.dot(wm[2 * p + 1], vp, preferred_element_type=F32)
            mixed = jnp.where(lane < HEAD_DIM, m0, m1) + bs_ref[:, cols]
            oa_ref[rows, cols] = u[rows, cols] * mixed

    if seg_major:
        nqkv = 3 * (C_WIDTH // LANES)
        for p in range(S5_BLOCKS):
            slot = nqkv + p
            perm_scr[slot] = proj_scr[:, O_B + p * LANES:O_B + (p + 1) * LANES]
            for c4 in range(4):
                perm2_scr[slot, c4 * (tm // 4):(c4 + 1) * (tm // 4), :] = perm_scr[slot, pl.ds(c4, tm // 4, stride=4), :]
            f = SEG // 4
            for c4 in range(4):
                for s in range(f):
                    c0 = p * SEG_W + (c4 + 4 * s) * LANES
                    ub_ref[:, c0:c0 + LANES] = perm2_scr[slot, pl.ds(c4 * (tm // 4) + s, tm // SEG, stride=f), :]
    else:
        ub_ref[...] = proj_scr[:, O_B:O_Q]

    rc, ra, rb = rc_ref[...], ra_ref[...], rb_ref[...]
    def emit(val, outs, slot, p):
        outs[0][:, p * LANES:(p + 1) * LANES] = val.astype(BF16)
        if len(dils) == 1:
            return
        perm_scr[slot] = val
        src, dst, prev = perm_scr, perm2_scr, 1
        for n in range(1, len(dils)):
            dil = dils[n]
            f = dil // prev
            for c in range(prev):
                for s in range(f):
                    cls = c + prev * s
                    blk = src[slot, pl.ds(c * (tm // prev) + s, tm // dil, stride=f), :]
                    c0 = cls * C_WIDTH + p * LANES
                    outs[n][:, c0:c0 + LANES] = blk.astype(BF16)
                    if n + 1 < len(dils):
                        dst[slot, cls * (tm // dil):(cls + 1) * (tm // dil), :] = blk
            src, dst, prev = dst, src, dil

    nd = len(dils)
    assert dils[0] == 1
    q_outs, k_outs, v_outs = qkv_refs[0:nd], qkv_refs[nd:2 * nd], qkv_refs[2 * nd:3 * nd]
    for p in range(C_WIDTH // LANES):
        cols = slice(p * LANES, (p + 1) * LANES)
        q = proj_scr[:, O_Q + p * LANES:O_Q + (p + 1) * LANES]
        emit(_rope_apply(q, rc, ra, rb) * (HEAD_DIM ** -0.5), q_outs, 3 * p, p)
        k = _rope_apply(proj_scr[:, O_K + p * LANES:O_K + (p + 1) * LANES], rc, ra, rb)
        emit(k, k_outs, 3 * p + 1, p)
        kf_ref[:, cols] = k
        vv = proj_scr[:, O_V + p * LANES:O_V + (p + 1) * LANES]
        emit(vv, v_outs, 3 * p + 2, p)
        vf_ref[:, cols] = vv


def in_projection(x, g1, w_in, w_s, bs_rows, rope, *, tm, seq_rows, tail_rows, dils, seg_major=False):
    bsz, s, _ = x.shape
    nt = s // tm
    tail_t = tail_rows // tm
    row = lambda w: pl.BlockSpec((None, tm, w), lambda b, t: (b, t, 0))
    tail = pl.BlockSpec((None, tm, C_WIDTH), lambda b, t: (b, jnp.maximum(t - (nt - tail_t), 0), 0))
    full = lambda shp: pl.BlockSpec(shp, lambda b, t: (0,) * len(shp))
    ropespec = pl.BlockSpec((tm, LANES), lambda b, t: (t, 0))
    sd = lambda w, dt: jax.ShapeDtypeStruct((bsz, s, w), dt)
    tail_sd = jax.ShapeDtypeStruct((bsz, tail_rows, C_WIDTH), F32)
    qkv_specs = [pl.BlockSpec((None, tm // d, d * C_WIDTH), lambda b, t: (b, t, 0)) for d in dils] * 3
    qkv_sds = [jax.ShapeDtypeStruct((bsz, s // d, d * C_WIDTH), BF16) for d in dils] * 3
    if seg_major:
        ub_spec = pl.BlockSpec((None, tm // SEG, SEG * B_WIDTH), lambda b, t: (b, t, 0))
        ub_sd = jax.ShapeDtypeStruct((bsz, s // SEG, SEG * B_WIDTH), F32)
    else:
        ub_spec, ub_sd = row(B_WIDTH), sd(B_WIDTH, F32)
    n_perm = 3 * (C_WIDTH // LANES) + S5_BLOCKS
    return pl.pallas_call(
        functools.partial(_inproj_kernel, seq_rows=seq_rows, dils=tuple(dils), seg_major=seg_major),
        grid=(bsz, nt),
        in_specs=[row(D_MODEL), full((1, D_MODEL)), full(w_in.shape), full(w_s.shape), full(bs_rows.shape),
                  ropespec, ropespec, ropespec],
        out_specs=(row(A_WIDTH), row(A_WIDTH), ub_spec, tail, tail, *qkv_specs),
        out_shape=(sd(A_WIDTH, F32), sd(A_WIDTH, F32), ub_sd, tail_sd, tail_sd, *qkv_sds),
        scratch_shapes=[pltpu.VMEM((tm, w_in.shape[1]), F32),
                        pltpu.VMEM((n_perm, tm, LANES), F32),
                        pltpu.VMEM((n_perm, tm, LANES), F32)],
        compiler_params=_params("arbitrary", "arbitrary"),
        name="in_projection",
    )(x, g1, w_in, w_s, bs_rows, *rope)


def _s5_readout(hb, uperm, gs, wc_ref, d_ref, wg_ref, bg_ref):
    for j in range(S5_BLOCKS):
        cols = slice(j * LANES, (j + 1) * LANES)
        y = jnp.dot(hb[j], wc_ref[j], preferred_element_type=F32) + d_ref[:, cols] * uperm[:, cols]
        gs[:, cols] = jax.nn.gelu(y)
    g = gs[...]
    gs[...] = g * jax.nn.sigmoid(_dot(g, wg_ref[...]) + bg_ref[...])


def _s5_prompt_kernel(u_ref, wb_ref, wc_ref, pr_ref, pi_ref, d_ref, wg_ref, bg_ref,
                      o_ref, hr_ref, hi_ref,
                      upad, uperm, xs, hb, cs, carry, gs, *, seglen):
    pitch = seglen + SUBLANES

    @pl.when(pl.program_id(1) == 0)
    def _():
        carry[...] = jnp.zeros_like(carry)

    for s in range(SUBLANES):
        for p in range(S5_BLOCKS):
            upad[p, s * pitch:s * pitch + seglen, :] = u_ref[s * seglen:(s + 1) * seglen, p * LANES:(p + 1) * LANES]

    def perm(i, _):
        r = pl.multiple_of(i * SUBLANES, SUBLANES)
        for p in range(S5_BLOCKS):
            uperm[pl.ds(r, SUBLANES), p * LANES:(p + 1) * LANES] = upad[p, pl.ds(i, SUBLANES, stride=pitch), :]
        return 0
    lax.fori_loop(0, seglen, perm, 0)

    for j in range(S5_BLOCKS):
        xs[j] = _dot(uperm[:, j * LANES:(j + 1) * LANES], wb_ref[j])

    for j in range(S5_BLOCKS):
        ar = jnp.broadcast_to(pr_ref[j, 0:1, :], (SUBLANES, S5_HALF))
        ai = jnp.broadcast_to(pi_ref[j, 0:1, :], (SUBLANES, S5_HALF))

        def scan(i, hc, j=j, ar=ar, ai=ai):
            hr, hi = hc
            r = pl.multiple_of(i * SUBLANES, SUBLANES)
            nhr = ar * hr - ai * hi + xs[j, pl.ds(r, SUBLANES), 0:S5_HALF]
            nhi = ar * hi + ai * hr + xs[j, pl.ds(r, SUBLANES), S5_HALF:2 * S5_HALF]
            xs[j, pl.ds(r, SUBLANES), 0:S5_HALF] = nhr
            xs[j, pl.ds(r, SUBLANES), S5_HALF:2 * S5_HALF] = nhi
            return nhr, nhi
        z = jnp.zeros((SUBLANES, S5_HALF), F32)
        er, ei = lax.fori_loop(0, seglen, scan, (z, z))

        a64r, a64i = pr_ref[j, seglen - 1:seglen, :], pi_ref[j, seglen - 1:seglen, :]
        cr, ci = carry[j, 0, 0:1, :], carry[j, 1, 0:1, :]
        for s in range(SUBLANES):
            cs[j, 0, s:s + 1, :] = cr
            cs[j, 1, s:s + 1, :] = ci
            cr, ci = (a64r * cr - a64i * ci + er[s:s + 1, :], a64r * ci + a64i * cr + ei[s:s + 1, :])
        carry[j, 0, 0:1, :] = cr
        carry[j, 1, 0:1, :] = ci
        hr_ref[j] = cr
        hi_ref[j] = ci

        cr8, ci8 = cs[j, 0], cs[j, 1]

        def fix(i2, _, j=j, cr8=cr8, ci8=ci8):
            r = pl.multiple_of(i2 * 2 * SUBLANES, 2 * SUBLANES)
            hrs, his = [], []
            for t in range(2):
                i = i2 * 2 + t
                pr, pi = pr_ref[j, pl.ds(i, 1), :], pi_ref[j, pl.ds(i, 1), :]
                rows = pl.ds(r + t * SUBLANES, SUBLANES)
                hrs.append(xs[j, rows, 0:S5_HALF] + pr * cr8 - pi * ci8)
                his.append(xs[j, rows, S5_HALF:2 * S5_HALF] + pr * ci8 + pi * cr8)
            hb[j, pl.ds(r, 2 * SUBLANES), 0:S5_HALF] = jnp.concatenate(hrs, 0).astype(BF16)
            hb[j, pl.ds(r, 2 * SUBLANES), S5_HALF:2 * S5_HALF] = jnp.concatenate(his, 0).astype(BF16)
            return 0
        lax.fori_loop(0, seglen // 2, fix, 0, unroll=2)

    _s5_readout(hb, uperm, gs, wc_ref, d_ref, wg_ref, bg_ref)

    def unperm(i, _):
        r = pl.multiple_of(i * SUBLANES, SUBLANES)
        for p in range(S5_BLOCKS):
            upad[p, pl.ds(i, SUBLANES, stride=pitch), :] = gs[pl.ds(r, SUBLANES), p * LANES:(p + 1) * LANES]
        return 0
    lax.fori_loop(0, seglen, unperm, 0)
    for s in range(SUBLANES):
        for p in range(S5_BLOCKS):
            o_ref[s * seglen:(s + 1) * seglen, p * LANES:(p + 1) * LANES] = upad[p, s * pitch:s * pitch + seglen, :]


def s5_prompt(u, wb, wc, prl, pil, d_skip, w_glu, b_glu, *, seglen):
    bsz, s, _ = u.shape
    t = SUBLANES * seglen
    pitch = seglen + SUBLANES
    full = lambda shp: pl.BlockSpec(shp, lambda b, c: (0,) * len(shp))
    st = pl.BlockSpec((None, S5_BLOCKS, 1, S5_HALF), lambda b, c: (b, 0, 0, 0))
    rows = pl.BlockSpec((None, t, B_WIDTH), lambda b, c: (b, c, 0))
    return pl.pallas_call(
        functools.partial(_s5_prompt_kernel, seglen=seglen),
        grid=(bsz, s // t),
        in_specs=[rows, full(wb.shape), full(wc.shape), full(prl.shape), full(pil.shape),
                  full((1, B_WIDTH)), full(w_glu.shape), full((1, B_WIDTH))],
        out_specs=(rows, st, st),
        out_shape=(jax.ShapeDtypeStruct((bsz, s, B_WIDTH), F32),
                   jax.ShapeDtypeStruct((bsz, S5_BLOCKS, 1, S5_HALF), F32),
                   jax.ShapeDtypeStruct((bsz, S5_BLOCKS, 1, S5_HALF), F32)),
        scratch_shapes=[pltpu.VMEM((S5_BLOCKS, SUBLANES * pitch, LANES), F32),
                        pltpu.VMEM((t, B_WIDTH), F32),
                        pltpu.VMEM((S5_BLOCKS, t, 2 * S5_HALF), F32),
                        pltpu.VMEM((S5_BLOCKS, t, 2 * S5_HALF), BF16),
                        pltpu.VMEM((S5_BLOCKS, 2, SUBLANES, S5_HALF), F32),
                        pltpu.VMEM((S5_BLOCKS, 2, SUBLANES, S5_HALF), F32),
                        pltpu.VMEM((t, B_WIDTH), F32)],
        compiler_params=_params("arbitrary", "arbitrary"),
        name="s5_prompt",
    )(u, wb, wc, prl, pil, d_skip, w_glu, b_glu)


def _s5_seg_kernel(u_ref, t_ref, eb_ref, gm_ref, a8r_ref, a8i_ref, d_ref, wg_ref, bg_ref,
                   o_ref, hr_ref, hi_ref, e_scr, cs_scr, gs_scr, carry):
    nseg = u_ref.shape[0]

    @pl.when(pl.program_id(1) == 0)
    def _():
        carry[...] = jnp.zeros_like(carry)

    for p in range(S5_BLOCKS):
        ubf = u_ref[:, p * SEG_W:(p + 1) * SEG_W].astype(BF16)
        e_scr[p] = jnp.dot(ubf, eb_ref[p], preferred_element_type=F32)

    sub = lax.broadcasted_iota(jnp.int32, (SUBLANES, S5_HALF), 0)
    a8 = [(a8r_ref[p], a8i_ref[p]) for p in range(S5_BLOCKS)]

    def block(k, c):
        r0 = pl.multiple_of(k * SUBLANES, SUBLANES)
        out = []
        for p in range(S5_BLOCKS):
            cr, ci = c[2 * p], c[2 * p + 1]
            er = e_scr[p, pl.ds(r0, SUBLANES), 0:S5_HALF]
            ei = e_scr[p, pl.ds(r0, SUBLANES), S5_HALF:2 * S5_HALF]
            br = jnp.zeros((SUBLANES, S5_HALF), F32)
            bi = jnp.zeros((SUBLANES, S5_HALF), F32)
            for j in range(SUBLANES):
                br = jnp.where(sub == j, cr, br)
                bi = jnp.where(sub == j, ci, bi)
                cr, ci = (a8[p][0] * cr - a8[p][1] * ci + er[j:j + 1, :],
                          a8[p][0] * ci + a8[p][1] * cr + ei[j:j + 1, :])
            cs_scr[p, pl.ds(r0, SUBLANES), 0:S5_HALF] = br
            cs_scr[p, pl.ds(r0, SUBLANES), S5_HALF:2 * S5_HALF] = bi
            out += [cr, ci]
        return tuple(out)

    c0 = tuple(carry[p, e, 0:1, :] for p in range(S5_BLOCKS) for e in range(2))
    cf = lax.fori_loop(0, nseg // SUBLANES, block, c0)
    for p in range(S5_BLOCKS):
        carry[p, 0, 0:1, :] = cf[2 * p]
        carry[p, 1, 0:1, :] = cf[2 * p + 1]
        hr_ref[p] = cf[2 * p]
        hi_ref[p] = cf[2 * p + 1]

    for p in range(S5_BLOCKS):
        cols = slice(p * SEG_W, (p + 1) * SEG_W)
        u = u_ref[:, cols]
        y = (jnp.dot(u.astype(BF16), t_ref[p], preferred_element_type=F32)
             + jnp.dot(cs_scr[p].astype(BF16), gm_ref[p], preferred_element_type=F32) + d_ref[:, cols] * u)
        gs_scr[:, cols] = jax.nn.gelu(y)
    for r in range(SEG):
        tok = [slice(p * SEG_W + r * LANES, p * SEG_W + (r + 1) * LANES) for p in range(S5_BLOCKS)]
        g = jnp.concatenate([gs_scr[:, c] for c in tok], axis=1)
        out = g * jax.nn.sigmoid(_dot(g, wg_ref[...]) + bg_ref[...])
        for p in range(S5_BLOCKS):
            o_ref[:, tok[p]] = out[:, p * LANES:(p + 1) * LANES]


def s5_seg(u, t, eb, gm, a8r, a8i, d_t, w_glu, b_glu, *, rows):
    bsz, nseg, width = u.shape
    resident = lambda a: pl.BlockSpec(a.shape, lambda b, c: (0,) * a.ndim, pipeline_mode=pl.Buffered(1))
    blk = pl.BlockSpec((None, rows, width), lambda b, c: (b, c, 0))
    st = pl.BlockSpec((None, S5_BLOCKS, 1, S5_HALF), lambda b, c: (b, 0, 0, 0))
    st_sd = jax.ShapeDtypeStruct((bsz, S5_BLOCKS, 1, S5_HALF), F32)
    return pl.pallas_call(
        _s5_seg_kernel,
        grid=(bsz, nseg // rows),
        in_specs=[blk] + [resident(a) for a in (t, eb, gm, a8r, a8i, d_t, w_glu, b_glu)],
        out_specs=(blk, st, st),
        out_shape=(jax.ShapeDtypeStruct(u.shape, F32), st_sd, st_sd),
        scratch_shapes=[pltpu.VMEM((S5_BLOCKS, rows, 2 * S5_HALF), F32),
                        pltpu.VMEM((S5_BLOCKS, rows, 2 * S5_HALF), F32),
                        pltpu.VMEM((rows, width), F32),
                        pltpu.VMEM((S5_BLOCKS, 2, SUBLANES, S5_HALF), F32)],
        compiler_params=_params("arbitrary", "arbitrary"),
        name="s5_seg",
    )(u, t, eb, gm, a8r, a8i, d_t, w_glu, b_glu)


def _s5_sample_kernel(u_ref, h0r_ref, h0i_ref, wb_ref, wc_ref, pr_ref, pi_ref, d_ref, wg_ref, bg_ref,
                      o_ref, hr_ref, hi_ref,
                      upad, uperm, xs, hb, gs, *, nseq, steps):
    nblk = nseq // SUBLANES
    for p in range(S5_BLOCKS):
        upad[p] = u_ref[:, p * LANES:(p + 1) * LANES]
    for st in range(steps):
        for bb in range(nblk):
            r = (st * nblk + bb) * SUBLANES
            for p in range(S5_BLOCKS):
                uperm[r:r + SUBLANES, p * LANES:(p + 1) * LANES] = (
                    upad[p, pl.ds(bb * SUBLANES * steps + st, SUBLANES, stride=steps), :])
    for j in range(S5_BLOCKS):
        xs[j] = _dot(uperm[:, j * LANES:(j + 1) * LANES], wb_ref[j])
    for j in range(S5_BLOCKS):
        ar = jnp.broadcast_to(pr_ref[j, 0:1, :], (SUBLANES, S5_HALF))
        ai = jnp.broadcast_to(pi_ref[j, 0:1, :], (SUBLANES, S5_HALF))
        for bb in range(nblk):
            seqs = slice(bb * SUBLANES, (bb + 1) * SUBLANES)
            hr, hi = h0r_ref[j, seqs, :], h0i_ref[j, seqs, :]
            for st in range(steps):
                r = (st * nblk + bb) * SUBLANES
                hr, hi = (ar * hr - ai * hi + xs[j, r:r + SUBLANES, 0:S5_HALF],
                          ar * hi + ai * hr + xs[j, r:r + SUBLANES, S5_HALF:2 * S5_HALF])
                xs[j, r:r + SUBLANES, 0:S5_HALF] = hr
                xs[j, r:r + SUBLANES, S5_HALF:2 * S5_HALF] = hi
            hr_ref[j, seqs, :] = hr
            hi_ref[j, seqs, :] = hi
        hb[j] = xs[j].astype(BF16)
    _s5_readout(hb, uperm, gs, wc_ref, d_ref, wg_ref, bg_ref)
    for st in range(steps):
        for bb in range(nblk):
            r = (st * nblk + bb) * SUBLANES
            for p in range(S5_BLOCKS):
                upad[p, pl.ds(bb * SUBLANES * steps + st, SUBLANES, stride=steps), :] = (
                    gs[r:r + SUBLANES, p * LANES:(p + 1) * LANES])
    for p in range(S5_BLOCKS):
        o_ref[:, p * LANES:(p + 1) * LANES] = upad[p]


def s5_sample(u, h0r, h0i, wb, wc, prl, pil, d_skip, w_glu, b_glu, *, nseq, steps):
    n = nseq * steps
    st_sd = jax.ShapeDtypeStruct((S5_BLOCKS, nseq, S5_HALF), F32)
    return pl.pallas_call(
        functools.partial(_s5_sample_kernel, nseq=nseq, steps=steps),
        out_shape=(jax.ShapeDtypeStruct((n, B_WIDTH), F32), st_sd, st_sd),
        scratch_shapes=[pltpu.VMEM((S5_BLOCKS, n, LANES), F32),
                        pltpu.VMEM((n, B_WIDTH), F32),
                        pltpu.VMEM((S5_BLOCKS, n, 2 * S5_HALF), F32),
                        pltpu.VMEM((S5_BLOCKS, n, 2 * S5_HALF), BF16),
                        pltpu.VMEM((n, B_WIDTH), F32)],
        compiler_params=pltpu.CompilerParams(vmem_limit_bytes=VMEM_LIMIT),
        name="s5_sample",
    )(u, h0r, h0i, wb, wc, prl, pil, d_skip, w_glu, b_glu)


def _attn_prompt_kernel(q_ref, kp_ref, kc_ref, vp_ref, vc_ref, o_ref, st_ref, kx, vx):
    tq = q_ref.shape[0]
    t = pl.program_id(2)
    kx[0:CHUNK, :] = kp_ref[...]
    kx[CHUNK:CHUNK + tq, :] = kc_ref[...]
    vx[0:CHUNK, :] = vp_ref[...]
    vx[CHUNK:CHUNK + tq, :] = vc_ref[...]
    r = lax.broadcasted_iota(jnp.int32, (CHUNK, 2 * CHUNK), 0)
    c = lax.broadcasted_iota(jnp.int32, (CHUNK, 2 * CHUNK), 1)
    band = (c >= r) & (c <= r + CHUNK)
    lane = lax.broadcasted_iota(jnp.int32, (CHUNK, LANES), 1)
    lo_half = lane < HEAD_DIM
    for u in range(tq // CHUNK):
        rows = slice(u * CHUNK, (u + 1) * CHUNK)
        keys = slice(u * CHUNK, (u + 2) * CHUNK)
        if u == 0:
            valid = band & (c >= jnp.where(t == 0, CHUNK, 0))
        else:
            valid = band
        bias = jnp.where(valid, 0.0, NEG_INF)
        st = jnp.zeros((CHUNK, LANES), F32)
        for p in range(C_WIDTH // LANES):
            cols = slice(p * LANES, (p + 1) * LANES)
            qp = q_ref[rows, cols]
            kpair = kx[keys, cols]
            vpair = vx[keys, cols]
            outs = []
            for hh in range(2):
                qm = jnp.where(lo_half if hh == 0 else ~lo_half, qp, jnp.zeros_like(qp))
                sc = lax.dot_general(qm, kpair, (((1,), (1,)), ((), ())), preferred_element_type=F32) + bias
                m = jnp.max(sc, -1, keepdims=True)
                pe = jnp.exp(sc - m)
                l = jnp.sum(pe, -1, keepdims=True)
                outs.append(jnp.dot(pe.astype(BF16), vpair, preferred_element_type=F32) / l)
                st = jnp.where(lane == 2 * p + hh, m + jnp.log(l), st)
            o_ref[rows, cols] = jnp.where(lo_half, outs[0], outs[1])
        st_ref[rows, :] = st


def attn_prompt(q, k, v, dil, *, tq):
    bsz, ln, _ = q.shape
    tq = min(tq, ln)
    cur = pl.BlockSpec((None, tq, C_WIDTH), lambda b, r, t: (b, t, r))
    prev = pl.BlockSpec((None, CHUNK, C_WIDTH), lambda b, r, t: (b, jnp.maximum(t * (tq // CHUNK) - 1, 0), r))
    return pl.pallas_call(
        _attn_prompt_kernel,
        grid=(bsz, dil, ln // tq),
        in_specs=[cur, prev, cur, prev, cur],
        out_specs=(cur, pl.BlockSpec((None, tq, LANES), lambda b, r, t: (b, t, r))),
        out_shape=(jax.ShapeDtypeStruct((bsz, ln, dil * C_WIDTH), F32),
                   jax.ShapeDtypeStruct((bsz, ln, dil * LANES), F32)),
        scratch_shapes=[pltpu.VMEM((CHUNK + tq, C_WIDTH), BF16), pltpu.VMEM((CHUNK + tq, C_WIDTH), BF16)],
        compiler_params=_params("arbitrary", "arbitrary", "arbitrary"),
        name=f"attn_prompt_d{dil}",
    )(q, k, k, v, v)


def _attn_sample_kernel(q_ref, kn_ref, vn_ref, ck_ref, cv_ref, o_ref, *, steps):
    cw = ck_ref.shape[-1]

    def mult(delta):
        cnt = jnp.zeros(delta.shape, F32)
        for window, dil in DILATIONS:
            cnt = cnt + jnp.where((delta >= 0) & (delta <= window) & ((delta & (dil - 1)) == 0), 1.0, 0.0)
        return cnt

    cnt_c = mult(cw + lax.broadcasted_iota(jnp.int32, (steps, cw), 0)
                 - lax.broadcasted_iota(jnp.int32, (steps, cw), 1))
    cnt_n = mult(lax.broadcasted_iota(jnp.int32, (steps, CHUNK), 0)
                 - lax.broadcasted_iota(jnp.int32, (steps, CHUNK), 1))
    live_c, live_n = cnt_c > 0, cnt_n > 0
    zpad = jnp.zeros((CHUNK - steps, C_WIDTH), F32)
    kn = jnp.concatenate([kn_ref[...], zpad], 0).astype(BF16)
    vn = jnp.concatenate([vn_ref[...], zpad], 0).astype(BF16)
    q = q_ref[...].astype(BF16)
    nt_dims = (((1,), (1,)), ((), ()))
    for hh in range(C_HEADS):
        cols = slice(hh * HEAD_DIM, (hh + 1) * HEAD_DIM)
        qh = q[:, cols]
        sc = jnp.dot(qh, ck_ref[hh].astype(BF16), preferred_element_type=F32)
        sn = lax.dot_general(qh, kn[:, cols], nt_dims, preferred_element_type=F32)
        m = jnp.maximum(jnp.max(jnp.where(live_c, sc, NEG_INF), -1, keepdims=True),
                        jnp.max(jnp.where(live_n, sn, NEG_INF), -1, keepdims=True))
        ec = cnt_c * jnp.exp(jnp.where(live_c, sc - m, NEG_INF))
        en = cnt_n * jnp.exp(jnp.where(live_n, sn - m, NEG_INF))
        l = jnp.sum(ec, -1, keepdims=True) + jnp.sum(en, -1, keepdims=True)
        acc = (lax.dot_general(ec.astype(BF16), cv_ref[hh].astype(BF16), nt_dims, preferred_element_type=F32)
               + jnp.dot(en.astype(BF16), vn[:, cols], preferred_element_type=F32))
        o_ref[:, cols] = acc / l


def attn_sample(q, k_new, v_new, cache_kt, cache_vt, layer, *, steps):
    _, nseq, _, _, cw = cache_kt.shape
    rows = pl.BlockSpec((steps, C_WIDTH), lambda b: (b, 0))
    cache = pl.BlockSpec((None, None, C_HEADS, HEAD_DIM, cw), lambda b: (layer, b, 0, 0, 0))
    return pl.pallas_call(
        functools.partial(_attn_sample_kernel, steps=steps),
        grid=(nseq,),
        in_specs=[rows, rows, rows, cache, cache],
        out_specs=rows,
        out_shape=jax.ShapeDtypeStruct((nseq * steps, C_WIDTH), F32),
        compiler_params=_params("arbitrary"),
        name="attn_sample",
    )(q, k_new, v_new, cache_kt, cache_vt)


FF_ROWS = 128
FF_PIECES = 4


def _residual_and_norm(x_ref, oa, ob, oc, wo_ref, g2_ref, y_ref, xn_scr):
    mix = (_dot(oa, wo_ref[0:A_WIDTH, :]) + _dot(ob, wo_ref[A_WIDTH:A_WIDTH + B_WIDTH, :])
           + _dot(oc, wo_ref[A_WIDTH + B_WIDTH:, :]))
    x1 = x_ref[...] + mix
    y_ref[...] = x1
    xn_scr[...] = _rms(x1, g2_ref[...]).astype(BF16)


def _outproj_prompt_kernel(x_ref, oa_ref, ob_ref, *rest, tiles_per_seq, dils, final_norm):
    nb = len(dils)
    o_refs, st_refs = rest[:nb], rest[nb:2 * nb]
    (e_ref, wo_ref, g2_ref, wup_ref, cw_ref, cb_ref, wd_ref, fg_ref,
     y_ref, cg_ref, cu_ref, xn_scr, carry_scr, unp_scr, hs0, hs1, act0, act1) = rest[2 * nb:]
    hs_scr, act_scr = (hs0, hs1), (act0, act1)
    i = pl.program_id(0)
    tm = x_ref.shape[0]
    nslab = C_WIDTH // LANES

    outs, sts = [], []
    slot = 0
    for dil, o_ref, s_ref in zip(dils, o_refs, st_refs):
        if dil == 1:
            outs.append(o_ref[...])
            sts.append(s_ref[...])
            continue
        for r in range(dil):
            rows = pl.ds(r, tm // dil, stride=dil)
            for p in range(nslab):
                c0 = r * C_WIDTH + p * LANES
                unp_scr[slot + p, rows, :] = o_ref[:, c0:c0 + LANES]
            unp_scr[slot + nslab, rows, :] = s_ref[:, r * LANES:(r + 1) * LANES]
        outs.append(jnp.concatenate([unp_scr[slot + p] for p in range(nslab)], axis=1))
        sts.append(unp_scr[slot + nslab])
        slot += nslab + 1
    mx = functools.reduce(jnp.maximum, sts)
    ws = [jnp.exp(s - mx) for s in sts]
    tot = functools.reduce(lambda a, b: a + b, ws)
    oc = jnp.zeros((tm, C_WIDTH), F32)
    for w, o in zip(ws, outs):
        alpha = w / tot
        hi = alpha.astype(BF16)
        lo = (alpha - hi.astype(F32)).astype(BF16)
        wide = jnp.dot(jnp.concatenate([hi, lo], axis=1), e_ref[...], preferred_element_type=F32)
        oc = oc + wide * o
    for p in range(S5_BLOCKS):
        for r in range(SEG):
            c0 = p * SEG_W + r * LANES
            unp_scr[slot + p, pl.ds(r, tm // SEG, stride=SEG), :] = ob_ref[:, c0:c0 + LANES]
    ob = jnp.concatenate([unp_scr[slot + p] for p in range(S5_BLOCKS)], axis=1)
    _residual_and_norm(x_ref, oa_ref[...], ob, oc, wo_ref, g2_ref, y_ref, xn_scr)

    @pl.when((i % tiles_per_seq) == 0)
    def _():
        carry_scr[...] = jnp.zeros_like(carry_scr)

    nhs = 2 * FF_CHUNK // LANES
    gate = FF_CHUNK // LANES
    piece = tm // FF_PIECES

    def up(jj, slot, pc):
        xr = xn_scr[pc * piece:(pc + 1) * piece, :]
        rows = slice(SUBLANES + pc * piece, SUBLANES + (pc + 1) * piece)
        for half in range(2):
            h = jnp.dot(xr, wup_ref[half * N_FF + jj], preferred_element_type=F32)
            for k in range(gate):
                hs_scr[slot][2 * (half * gate + k), rows, :] = h[:, k * LANES:(k + 1) * LANES]

    def down(jj, slot, pc):
        rows = slice(pc * piece, (pc + 1) * piece)
        y_ref[rows, :] += jnp.dot(act_scr[slot][rows, :], wd_ref[jj], preferred_element_type=F32)

    def conv_taps(jj, slot):
        for k in range(nhs):
            hs_scr[slot][2 * k, 0:SUBLANES, :] = carry_scr[jj, k]
        cwj = jnp.concatenate([cw_ref[jj], cw_ref[N_FF + jj]], axis=1)
        cbj = jnp.concatenate([cb_ref[jj], cb_ref[N_FF + jj]], axis=1)
        slabs = lambda row: jnp.stack([row[:, k * LANES:(k + 1) * LANES] for k in range(nhs)])
        return [slabs(cwj[t:t + 1]) for t in range(3)], slabs(cbj)

    def conv(jj, slot, pc, taps, bias):
        every_other = pl.ds(0, nhs, stride=2)
        for r in range(pc * piece // FF_ROWS, (pc + 1) * piece // FF_ROWS):
            shifted = lambda back: hs_scr[slot][every_other, pl.ds(SUBLANES - back + r * FF_ROWS, FF_ROWS), :]
            hc = bias + (shifted(2) * taps[0] + shifted(1) * taps[1] + shifted(0) * taps[2])
            act = jax.nn.gelu(hc[:gate]) * hc[gate:]
            for k in range(gate):
                act_scr[slot][r * FF_ROWS:(r + 1) * FF_ROWS, k * LANES:(k + 1) * LANES] = act[k].astype(BF16)

    def conv_tail(jj, slot):
        for k in range(nhs):
            last = hs_scr[slot][2 * k, tm:tm + SUBLANES, :]
            carry_scr[jj, k] = last
            if k < gate:
                cg_ref[0, jj, :, k * LANES:(k + 1) * LANES] = last
            else:
                cu_ref[0, jj, :, (k - gate) * LANES:(k - gate + 1) * LANES] = last

    def stage(j, slot, do_down, do_up):
        taps, bias = conv_taps(j, slot)
        for pc in range(FF_PIECES):
            conv(j, slot, pc, taps, bias)
            if do_down:
                down(j - 1, 1 - slot, pc)
            if do_up:
                up(j + 1, 1 - slot, pc)
        conv_tail(j, slot)

    for pc in range(FF_PIECES):
        up(0, 0, pc)
    stage(0, 0, False, True)

    def steady(n, _):
        stage(2 * n + 1, 1, True, True)
        stage(2 * n + 2, 0, True, True)
        return 0
    n_pairs = (N_FF - 2) // 2
    lax.fori_loop(0, n_pairs, steady, 0)
    for j in range(2 * n_pairs + 1, N_FF):
        stage(j, j % 2, True, j + 1 < N_FF)
    for pc in range(FF_PIECES):
        down(N_FF - 1, (N_FF - 1) % 2, pc)

    if final_norm:
        y_ref[...] = _rms(y_ref[...], fg_ref[...])


def _outproj_sample_kernel(x_ref, oa_ref, ob_ref, oc_ref, wo_ref, g2_ref, wug_ref, wuu_ref, cwg_ref, cwu_ref,
                           cbg_ref, cbu_ref, wd_ref, fg_ref, bufg_ref, bufu_ref,
                           y_ref, cg_ref, cu_ref, xn_scr, hs_scr, *, nseq, seq_len, final_norm):
    j = pl.program_id(1)
    tm = x_ref.shape[0]

    @pl.when(j == 0)
    def _():
        _residual_and_norm(x_ref, oa_ref[...], ob_ref[...], oc_ref[...], wo_ref, g2_ref, y_ref, xn_scr)

    xn = xn_scr[...]
    hg = jnp.dot(xn, wug_ref[...], preferred_element_type=F32)
    hu = jnp.dot(xn, wuu_ref[...], preferred_element_type=F32)
    lo, hi_ = SUBLANES, SUBLANES + seq_len
    hs_scr[:, lo:hi_, 0:FF_CHUNK] = hg.reshape(nseq, seq_len, FF_CHUNK)
    hs_scr[:, lo:hi_, FF_CHUNK:] = hu.reshape(nseq, seq_len, FF_CHUNK)
    hs_scr[:, lo - 2:lo, 0:FF_CHUNK] = bufg_ref[...]
    hs_scr[:, lo - 2:lo, FF_CHUNK:] = bufu_ref[...]
    cg_ref[...] = hs_scr[:, hi_ - SUBLANES:hi_, 0:FF_CHUNK]
    cu_ref[...] = hs_scr[:, hi_ - SUBLANES:hi_, FF_CHUNK:]

    cw = jnp.concatenate([cwg_ref[...], cwu_ref[...]], axis=1)
    cb = jnp.concatenate([cbg_ref[...], cbu_ref[...]], axis=1)
    hc = cb + (hs_scr[:, lo - 2:hi_ - 2, :] * cw[0:1] + hs_scr[:, lo - 1:hi_ - 1, :] * cw[1:2]
               + hs_scr[:, lo:hi_, :] * cw[2:3])
    hc = hc.reshape(tm, 2 * FF_CHUNK)
    act = jax.nn.gelu(hc[:, :FF_CHUNK]) * hc[:, FF_CHUNK:]
    y_ref[...] += _dot(act, wd_ref[...])

    if final_norm:
        @pl.when(j == pl.num_programs(1) - 1)
        def _():
            y_ref[...] = _rms(y_ref[...], fg_ref[...])


def ffn_layouts(w_up, conv_w, conv_b, w_down):
    chunks = lambda a: jnp.transpose(a.reshape(a.shape[0], 2 * N_FF, FF_CHUNK), (1, 0, 2))
    return (chunks(w_up), chunks(conv_w), chunks(conv_b.reshape(1, -1)), w_down.reshape(N_FF, FF_CHUNK, D_MODEL))


def out_projection_prompt(x, oa, ob, o_list, st_list, dils, expand, ffw, *, tm, tiles_per_seq, final_norm):
    n = x.shape[0]
    nt = n // tm
    row = lambda w: pl.BlockSpec((tm, w), lambda i: (i, 0))
    perm = lambda w: [pl.BlockSpec((tm // d, d * w), lambda i: (i, 0)) for d in dils]
    resident = lambda a: pl.BlockSpec(a.shape, lambda i: (0,) * a.ndim, pipeline_mode=pl.Buffered(1))
    conv_spec = pl.BlockSpec((1, N_FF, SUBLANES, FF_CHUNK), lambda i: (i, 0, 0, 0))
    conv_sd = jax.ShapeDtypeStruct((nt, N_FF, SUBLANES, FF_CHUNK), F32)
    n_unp = sum(C_WIDTH // LANES + 1 for d in dils if d > 1) + S5_BLOCKS
    seg_rows = pl.BlockSpec((tm // SEG, SEG * B_WIDTH), lambda i: (i, 0))
    nhs = 2 * FF_CHUNK // LANES
    return pl.pallas_call(
        functools.partial(_outproj_prompt_kernel, tiles_per_seq=tiles_per_seq, dils=tuple(dils),
                          final_norm=final_norm),
        grid=(nt,),
        in_specs=[row(D_MODEL), row(A_WIDTH), seg_rows] + perm(C_WIDTH) + perm(LANES)
                 + [resident(expand)] + [resident(a) for a in ffw],
        out_specs=(row(D_MODEL), conv_spec, conv_spec),
        out_shape=(jax.ShapeDtypeStruct((n, D_MODEL), F32), conv_sd, conv_sd),
        scratch_shapes=[pltpu.VMEM((tm, D_MODEL), BF16),
                        pltpu.VMEM((N_FF, nhs, SUBLANES, LANES), F32),
                        pltpu.VMEM((n_unp, tm, LANES), F32),
                        pltpu.VMEM((2 * nhs, SUBLANES + tm, LANES), F32),
                        pltpu.VMEM((2 * nhs, SUBLANES + tm, LANES), F32),
                        pltpu.VMEM((tm, FF_CHUNK), BF16),
                        pltpu.VMEM((tm, FF_CHUNK), BF16)],
        compiler_params=_params("arbitrary"),
        name="out_projection_prompt",
    )(x, oa, ob, *o_list, *st_list, expand, *ffw)


def out_projection_sample(x, oa, ob, oc, ffw, bufs, *, nseq, seq_len, final_norm):
    n = x.shape[0]
    row = lambda w: pl.BlockSpec((n, w), lambda i, j: (0, 0))
    full = lambda a: pl.BlockSpec(a.shape, lambda i, j: (0,) * a.ndim)
    w_out, g2, w_up, conv_w, conv_b, w_down, fg = ffw
    chunk = lambda a, off: pl.BlockSpec((None,) + a.shape[1:], lambda i, j: (off + j, 0, 0))
    wspecs = [full(w_out), full(g2), chunk(w_up, 0), chunk(w_up, N_FF), chunk(conv_w, 0), chunk(conv_w, N_FF),
              chunk(conv_b, 0), chunk(conv_b, N_FF), chunk(w_down, 0), full(fg)]
    wargs = [w_out, g2, w_up, w_up, conv_w, conv_w, conv_b, conv_b, w_down, fg]
    conv_spec = pl.BlockSpec((nseq, SUBLANES, FF_CHUNK), lambda i, j: (0, 0, j))
    return pl.pallas_call(
        functools.partial(_outproj_sample_kernel, nseq=nseq, seq_len=seq_len, final_norm=final_norm),
        grid=(1, N_FF),
        in_specs=[row(D_MODEL), row(A_WIDTH), row(B_WIDTH), row(C_WIDTH)] + wspecs
                 + [pl.BlockSpec((nseq, 2, FF_CHUNK), lambda i, j: (0, 0, j)),
                    pl.BlockSpec((nseq, 2, FF_CHUNK), lambda i, j: (0, 0, N_FF + j))],
        out_specs=(row(D_MODEL), conv_spec, conv_spec),
        out_shape=(jax.ShapeDtypeStruct((n, D_MODEL), F32),
                   jax.ShapeDtypeStruct((nseq, SUBLANES, D_FF), F32),
                   jax.ShapeDtypeStruct((nseq, SUBLANES, D_FF), F32)),
        scratch_shapes=[pltpu.VMEM((n, D_MODEL), BF16),
                        pltpu.VMEM((nseq, SUBLANES + seq_len, 2 * FF_CHUNK), F32)],
        compiler_params=_params("arbitrary", "arbitrary"),
        name="out_projection_sample",
    )(x, oa, ob, oc, *wargs, bufs, bufs)


PROMPT_TM = 512
S5_ROWS = 256
ATTN_TQ = 512


def kernel(x_prompt, x_sample, cache_c_k, cache_c_v, state_ssm_re, state_ssm_im, state_ffn_conv, norm1_g, w_in, w_s, b_s, ssm_lam_re, ssm_lam_im, ssm_log_dt, ssm_b_re, ssm_b_im, ssm_c_re, ssm_c_im, ssm_d, w_glu, b_glu, w_out, norm2_g, w_up, conv_w, conv_b, w_down, final_g):
    bp, sp, _ = x_prompt.shape
    nseq, steps, _ = x_sample.shape
    ns = nseq * steps
    cw_p = min(MAX_WINDOW, sp)
    cw_s = cache_c_k.shape[2]

    rope_p = rope_tables(sp, 0)
    rope_s = tuple(jnp.tile(t, (nseq, 1)) for t in rope_tables(steps, PAST_LEN))
    bbr, bbi, pr, pi, ncim = s5_params(ssm_lam_re, ssm_lam_im, ssm_log_dt, ssm_b_re, ssm_b_im, ssm_c_im, SEG)
    lag_k, abr, abi, car, cani = s5_seg_params(pr, pi, bbr, bbi, ssm_c_re, ncim)

    w_in_b, w_out_b, w_up_b, w_down_b, w_glu_b = (w.astype(BF16) for w in (w_in, w_out, w_up, w_down, w_glu))
    expand = (jnp.arange(2 * LANES)[:, None] % LANES == jnp.arange(C_WIDTH)[None, :] // HEAD_DIM).astype(BF16)
    bs_p = jnp.repeat(jnp.transpose(b_s, (0, 2, 1)), HEAD_DIM, axis=2)
    bs_s = jnp.tile(bs_p[:, :steps], (1, CHUNK // steps, 1))
    ws_s = jnp.tile(w_s[:, :, :steps, :steps], (1, 1, CHUNK // steps, CHUNK // steps))
    row1 = lambda a: a.reshape(1, -1)

    cache_kt = jnp.transpose(cache_c_k, (0, 1, 3, 4, 2))
    cache_vt = jnp.transpose(cache_c_v, (0, 1, 3, 4, 2))
    dils = tuple(d for _, d in DILATIONS)

    hp = x_prompt
    hs = x_sample.reshape(1, ns, D_MODEL)
    outs = [[] for _ in range(11)]
    for l in range(DEPTH):
        last = l == DEPTH - 1
        wb, wc, prl, pil = s5_layouts(bbr[l], bbi[l], ssm_c_re[l], ncim[l], pr[l], pi[l])
        s5w = (wb, wc, prl, pil, row1(ssm_d[l]), w_glu_b[l], row1(b_glu[l]))
        ffw = (w_out_b[l], row1(norm2_g[l]), *ffn_layouts(w_up_b[l], conv_w[l], conv_b[l], w_down_b[l]),
               row1(final_g))

        oa, _, ub, kf, vf, *qkv = in_projection(hp, row1(norm1_g[l]), w_in_b[l], w_s[l], bs_p[l], rope_p,
                                                tm=PROMPT_TM, seq_rows=CHUNK, tail_rows=cw_p, dils=dils,
                                                seg_major=True)
        nd = len(dils)
        segw = s5_seg_layouts(lag_k[l], abr[l], abi[l], car[l], cani[l], pr[l], pi[l], ssm_d[l])
        ob, hr, hi = s5_seg(ub, *segw, w_glu_b[l], row1(b_glu[l]), rows=S5_ROWS)
        o_list, st_list = zip(*[attn_prompt(qkv[n], qkv[nd + n], qkv[2 * nd + n], dil, tq=ATTN_TQ)
                                for n, dil in enumerate(dils)])
        flat = lambda a: a.reshape(a.shape[0] * a.shape[1], a.shape[2])
        y, cg, cu = out_projection_prompt(flat(hp), flat(oa), flat(ob), [flat(o) for o in o_list],
                                          [flat(s) for s in st_list], dils, expand, ffw,
                                          tm=PROMPT_TM, tiles_per_seq=sp // PROMPT_TM, final_norm=last)
        hp = y.reshape(bp, sp, D_MODEL)
        outs[0].append(kf.reshape(bp, cw_p, C_HEADS, HEAD_DIM))
        outs[1].append(vf.reshape(bp, cw_p, C_HEADS, HEAD_DIM))
        outs[4].append(hr.reshape(bp, B_GROUPS, SSM_STATE))
        outs[5].append(hi.reshape(bp, B_GROUPS, SSM_STATE))
        seq_end = lambda a: jnp.transpose(a.reshape(bp, sp // PROMPT_TM, N_FF, SUBLANES, FF_CHUNK)[:, -1, :, -2:],
                                          (0, 2, 1, 3)).reshape(bp, 2, D_FF)
        outs[8].append(jnp.concatenate([seq_end(cg), seq_end(cu)], -1))

        oa, vn, ub, kf, vf, q, k, v = in_projection(hs, row1(norm1_g[l]), w_in_b[l], ws_s[l], bs_s[l], rope_s,
                                                    tm=ns, seq_rows=steps, tail_rows=ns, dils=(1,))
        h0r = jnp.transpose(state_ssm_re[l].reshape(nseq, S5_BLOCKS, S5_HALF), (1, 0, 2))
        h0i = jnp.transpose(state_ssm_im[l].reshape(nseq, S5_BLOCKS, S5_HALF), (1, 0, 2))
        ob, hr, hi = s5_sample(ub[0], h0r, h0i, *s5w, nseq=nseq, steps=steps)
        oc = attn_sample(q[0].astype(F32), k[0].astype(F32), v[0].astype(F32), cache_kt, cache_vt, l, steps=steps)
        y, cg, cu = out_projection_sample(hs[0], oa[0], ob, oc, ffw, state_ffn_conv[l],
                                          nseq=nseq, seq_len=steps, final_norm=last)
        hs = y.reshape(1, ns, D_MODEL)
        outs[2].append(kf.reshape(nseq, steps, C_HEADS, HEAD_DIM))
        outs[3].append(vf.reshape(nseq, steps, C_HEADS, HEAD_DIM))
        outs[6].append(jnp.transpose(hr, (1, 0, 2)).reshape(nseq, B_GROUPS, SSM_STATE))
        outs[7].append(jnp.transpose(hi, (1, 0, 2)).reshape(nseq, B_GROUPS, SSM_STATE))
        outs[9].append(jnp.concatenate([cg[:, -2:], cu[:, -2:]], -1))
        outs[10].append(vn.reshape(nseq, steps, A_WIDTH))

    return (hp, hs.reshape(nseq, steps, D_MODEL)) + tuple(jnp.stack(o) for o in outs)
```

```python
import functools
import math

import jax
import jax.numpy as jnp
from jax import lax
from jax.experimental import pallas as pl
from jax.experimental.pallas import tpu as pltpu

F32 = jnp.float32
BF16 = jnp.bfloat16

D_MODEL = 1024
DEPTH = 4
PAST_LEN = 8192
HEAD_DIM = 64
A_WIDTH = 256
B_WIDTH = 384
C_WIDTH = 384
A_HEADS = 4
C_HEADS = 6
CHUNK = 128
SSM_GROUP = 16
B_GROUPS = 24
SSM_STATE = 64
DILATIONS = ((128, 1), (512, 4), (2048, 16))
MAX_WINDOW = 2048
ROPE_THETA = 500000.0
ROT_DIM = 16
D_FF = 2816
EPS = 1e-6
NEG_INF = -1e30

O_A, O_B, O_Q, O_K, O_V = 0, 2 * A_WIDTH, 896, 1280, 1664

LANES = 128
SUBLANES = 8
S5_BLOCKS = B_WIDTH // LANES
S5_HALF = 512
FF_CHUNK = 256
N_FF = D_FF // FF_CHUNK
VMEM_LIMIT = 56 * 1024 * 1024


def _dot(a, b):
    return jnp.dot(a.astype(BF16), b.astype(BF16), preferred_element_type=F32)


def _rms(x, g):
    return x * lax.rsqrt(jnp.mean(x * x, -1, keepdims=True) + EPS) * g


def _params(*sem):
    return pltpu.CompilerParams(dimension_semantics=sem, vmem_limit_bytes=VMEM_LIMIT)


def _rope_tables_kernel(c_ref, a_ref, b_ref, *, pos0):
    n = c_ref.shape[0]
    pos = (lax.broadcasted_iota(jnp.int32, (n, LANES), 0) + (pos0 + pl.program_id(0) * n)).astype(F32)
    d = lax.broadcasted_iota(jnp.int32, (n, LANES), 1) & (HEAD_DIM - 1)
    k = (d & (ROT_DIM // 2 - 1)).astype(F32)
    inv = jnp.exp(k * (-2.0 / ROT_DIM * math.log(ROPE_THETA)))
    ang = pos * inv
    cos, sin = jnp.cos(ang), jnp.sin(ang)
    c_ref[...] = jnp.where(d < ROT_DIM, cos, 1.0)
    a_ref[...] = jnp.where(d < ROT_DIM // 2, -sin, 0.0)
    b_ref[...] = jnp.where((d >= ROT_DIM // 2) & (d < ROT_DIM), sin, 0.0)


def rope_tables(n, pos0):
    sds = jax.ShapeDtypeStruct((n, LANES), F32)
    tr = min(n, 1024)
    spec = pl.BlockSpec((tr, LANES), lambda i: (i, 0))
    return pl.pallas_call(functools.partial(_rope_tables_kernel, pos0=pos0), grid=(n // tr,),
                          out_specs=(spec, spec, spec), out_shape=(sds, sds, sds),
                          compiler_params=_params("arbitrary"), name="rope_tables")()


def _s5_params_kernel(lr_ref, li_ref, ldt_ref, br_ref, bi_ref, cim_ref,
                      bbr_ref, bbi_ref, pr_ref, pi_ref, ncim_ref, *, n_pow):
    lr, li = lr_ref[...], li_ref[...]
    dt = jnp.exp(ldt_ref[...])
    mag = jnp.exp(lr * dt)
    ar, ai = mag * jnp.cos(li * dt), mag * jnp.sin(li * dt)
    nr, ni = ar - 1.0, ai
    den = lr * lr + li * li
    fr, fi = (nr * lr + ni * li) / den, (ni * lr - nr * li) / den
    br, bi = br_ref[...], bi_ref[...]
    bbr_ref[...] = fr[None] * br - fi[None] * bi
    bbi_ref[...] = fr[None] * bi + fi[None] * br
    k = (lax.broadcasted_iota(jnp.int32, (n_pow,) + lr.shape, 0) + 1).astype(F32)
    magk = jnp.exp(k * (lr * dt)[None])
    angk = k * (li * dt)[None]
    pr_ref[...] = magk * jnp.cos(angk)
    pi_ref[...] = magk * jnp.sin(angk)
    ncim_ref[...] = -cim_ref[...]


def s5_params(lam_re, lam_im, log_dt, b_re, b_im, c_im, n_pow):
    depth, g, p = lam_re.shape
    c = b_re.shape[-1]
    ldt = jnp.broadcast_to(log_dt[:, :, None], (depth, g, p))
    brt = jnp.transpose(b_re, (0, 3, 1, 2))
    bit = jnp.transpose(b_im, (0, 3, 1, 2))
    gp = pl.BlockSpec((None, g, p), lambda l: (l, 0, 0))
    cgp = pl.BlockSpec((None, c, g, p), lambda l: (l, 0, 0, 0))
    gcp = pl.BlockSpec((None, g, c, p), lambda l: (l, 0, 0, 0))
    kgp = pl.BlockSpec((None, n_pow, g, p), lambda l: (l, 0, 0, 0))
    return pl.pallas_call(
        functools.partial(_s5_params_kernel, n_pow=n_pow),
        grid=(depth,),
        in_specs=[gp, gp, gp, cgp, cgp, gcp],
        out_specs=(cgp, cgp, kgp, kgp, gcp),
        out_shape=(jax.ShapeDtypeStruct((depth, c, g, p), F32), jax.ShapeDtypeStruct((depth, c, g, p), F32),
                   jax.ShapeDtypeStruct((depth, n_pow, g, p), F32), jax.ShapeDtypeStruct((depth, n_pow, g, p), F32),
                   jax.ShapeDtypeStruct((depth, g, c, p), F32)),
        compiler_params=_params("arbitrary"),
        name="s5_params",
    )(lam_re, lam_im, ldt, brt, bit, c_im)


def _s5_lane_vec(t):
    return t.reshape(t.shape[:-2] + (S5_BLOCKS, 1, S5_HALF))


def s5_layouts(bbr, bbi, c_re, ncim, pr, pi):
    eye = jnp.eye(SUBLANES, dtype=F32)
    bb = jnp.stack([bbr, bbi], 0).reshape(2, SSM_GROUP, S5_BLOCKS, 8, SSM_STATE)
    wb = jnp.einsum('ecjgp,gh->jgcehp', bb, eye).reshape(S5_BLOCKS, LANES, 2 * S5_HALF)
    cc = jnp.stack([c_re, ncim], 0).reshape(2, S5_BLOCKS, 8, SSM_GROUP, SSM_STATE)
    wc = jnp.einsum('ejgcp,gh->jeghpc', cc, eye)
    wc = jnp.transpose(wc, (0, 1, 2, 4, 3, 5)).reshape(S5_BLOCKS, 2 * S5_HALF, LANES)
    n_pow = pr.shape[0]
    prl = jnp.transpose(pr.reshape(n_pow, S5_BLOCKS, S5_HALF), (1, 0, 2))
    pil = jnp.transpose(pi.reshape(n_pow, S5_BLOCKS, S5_HALF), (1, 0, 2))
    return wb.astype(BF16), wc.astype(BF16), prl, pil


SEG = 8
SEG_W = SEG * LANES


def _s5_seg_params_kernel(p0r_ref, p0i_ref, p1r_ref, p1i_ref, bbr_ref, bbi_ref, cre_ref, ncim_ref, cg_ref, ncg_ref,
                          k_ref, abr_ref, abi_ref, car_ref, cani_ref):
    abr = p0r_ref[...] * bbr_ref[...] - p0i_ref[...] * bbi_ref[...]
    abi = p0r_ref[...] * bbi_ref[...] + p0i_ref[...] * bbr_ref[...]
    abr_ref[...] = abr
    abi_ref[...] = abi
    car_ref[...] = cre_ref[...] * p1r_ref[...] + ncim_ref[...] * p1i_ref[...]
    cani_ref[...] = ncim_ref[...] * p1r_ref[...] - cre_ref[...] * p1i_ref[...]
    nt = (((1,), (1,)), ((), ()))
    for g in range(B_GROUPS):
        k_ref[g] = (lax.dot_general(cg_ref[g], abr[g], nt, precision=lax.Precision.HIGHEST,
                                    preferred_element_type=F32)
                    + lax.dot_general(ncg_ref[g], abi[g], nt, precision=lax.Precision.HIGHEST,
                                      preferred_element_type=F32))


def s5_seg_params(pr, pi, bbr, bbi, c_re, ncim):
    depth = pr.shape[0]
    g, c, p = B_GROUPS, SSM_GROUP, SSM_STATE
    over_c = lambda a: jnp.broadcast_to(jnp.transpose(a, (0, 2, 1, 3))[:, :, :, None, :],
                                        (depth, g, SEG, c, p)).reshape(depth, g, SEG * c, p)
    over_k = lambda a: jnp.broadcast_to(a[:, :, None], (depth, g, SEG, c, p)).reshape(depth, g, SEG * c, p)
    p0r = jnp.concatenate([jnp.ones_like(pr[:, :1]), pr[:, :SEG - 1]], 1)
    p0i = jnp.concatenate([jnp.zeros_like(pi[:, :1]), pi[:, :SEG - 1]], 1)
    bg = lambda a: jnp.transpose(a, (0, 2, 1, 3))
    big = pl.BlockSpec((None, g, SEG * c, p), lambda l: (l, 0, 0, 0))
    small = pl.BlockSpec((None, g, c, p), lambda l: (l, 0, 0, 0))
    big_sd = jax.ShapeDtypeStruct((depth, g, SEG * c, p), F32)
    return pl.pallas_call(
        _s5_seg_params_kernel,
        grid=(depth,),
        in_specs=[big] * 8 + [small, small],
        out_specs=(pl.BlockSpec((None, g, c, SEG * c), lambda l: (l, 0, 0, 0)), big, big, big, big),
        out_shape=(jax.ShapeDtypeStruct((depth, g, c, SEG * c), F32), big_sd, big_sd, big_sd, big_sd),
        compiler_params=_params("arbitrary"),
        name="s5_seg_params",
    )(over_c(p0r), over_c(p0i), over_c(pr[:, :SEG]), over_c(pi[:, :SEG]), over_k(bg(bbr)), over_k(bg(bbi)),
      over_k(c_re), over_k(ncim), c_re, ncim)


def s5_seg_layouts(kk, abr, abi, car, cani, pr, pi, d_skip):
    depth = kk.shape[0]
    c, q = SSM_GROUP, SSM_STATE

    def state_rows(re, im):
        a = jnp.stack([re, im], 1).reshape(depth, 2, S5_BLOCKS, 8, SEG, c, q)
        a = jnp.transpose(a, (0, 2, 4, 5, 1, 3, 6)).reshape(depth, S5_BLOCKS, SEG, c, 2 * S5_HALF)
        same = jnp.arange(8)[:, None] == ((jnp.arange(2 * S5_HALF) // q) % 8)[None, :]
        a = jnp.where(same[None, None, None, :, None, :], a[:, :, :, None, :, :], 0.0)
        return a.reshape(depth, S5_BLOCKS, SEG, LANES, 2 * S5_HALF)

    eb = state_rows(abr, abi)[:, :, ::-1].reshape(depth, S5_BLOCKS, SEG_W, 2 * S5_HALF)
    gm = state_rows(car, cani).reshape(depth, S5_BLOCKS, SEG_W, 2 * S5_HALF)
    kt = jnp.transpose(kk.reshape(depth, S5_BLOCKS, 8, c, SEG, c), (0, 1, 4, 5, 2, 3))
    kt = kt.reshape(depth, S5_BLOCKS, SEG, c, LANES)
    same = jnp.arange(8)[:, None] == (jnp.arange(LANES) // c)[None, :]
    kb = jnp.where(same[None, None, None, :, None, :], kt[:, :, :, None, :, :], 0.0)
    kb = kb.reshape(depth, S5_BLOCKS, SEG, LANES, LANES)
    zero = jnp.zeros_like(kb[:, :, 0])
    t = jnp.concatenate([jnp.concatenate([kb[:, :, r - rp] if r >= rp else zero for r in range(SEG)], axis=3)
                         for rp in range(SEG)], axis=2)
    a8r = pr[:, SEG - 1].reshape(depth, S5_BLOCKS, 1, S5_HALF)
    a8i = pi[:, SEG - 1].reshape(depth, S5_BLOCKS, 1, S5_HALF)
    d_t = jnp.tile(d_skip.reshape(depth, S5_BLOCKS, 1, LANES), (1, 1, SEG, 1)).reshape(depth, 1, S5_BLOCKS * SEG_W)
    return t.astype(BF16), eb.astype(BF16), gm.astype(BF16), a8r, a8i, d_t


def _rope_apply(x, c, a, b):
    return x * c + pltpu.roll(x, LANES - ROT_DIM // 2, 1) * a + pltpu.roll(x, ROT_DIM // 2, 1) * b


def _inproj_kernel(x_ref, g_ref, w_ref, ws_ref, bs_ref, rc_ref, ra_ref, rb_ref,
                   oa_ref, vn_ref, ub_ref, kf_ref, vf_ref, *rest, seq_rows, dils, seg_major):
    qkv_refs, proj_scr, perm_scr, perm2_scr = rest[:-3], rest[-3], rest[-2], rest[-1]
    tm = x_ref.shape[0]
    xn = _rms(x_ref[...], g_ref[...]).astype(BF16)
    proj_scr[...] = jnp.dot(xn, w_ref[...], preferred_element_type=F32)

    h = jax.nn.gelu(proj_scr[:, O_A:O_B])
    u, v = h[:, :A_WIDTH], h[:, A_WIDTH:]
    mu = jnp.mean(v, -1, keepdims=True)
    var = jnp.mean(jnp.square(v - mu), -1, keepdims=True)
    vn = (v - mu) * lax.rsqrt(var + EPS)
    vn_ref[...] = vn
    ri = lax.broadcasted_iota(jnp.int32, (CHUNK, CHUNK), 0)
    ci = lax.broadcasted_iota(jnp.int32, (CHUNK, CHUNK), 1)
    keep = (ci <= ri) & ((ri // seq_rows) == (ci // seq_rows))
    lane = lax.broadcasted_iota(jnp.int32, (CHUNK, LANES), 1)
    wm = [jnp.where(keep, ws_ref[hh], 0.0).astype(BF16) for hh in range(A_HEADS)]
    vnb = vn.astype(BF16)
    for c in range(tm // CHUNK):
        rows = slice(c * CHUNK, (c + 1) * CHUNK)
        for p in range(A_WIDTH // LANES):
            cols = slice(p * LANES, (p + 1) * LANES)
            vp = vnb[rows, cols]
            m0 = jnp.dot(wm[2 * p], vp, preferred_element_type=F32)
            m1 = jnp.dot(wm[2 * p + 1], vp, preferred_element_type=F32)
            mixed = jnp.where(lane < HEAD_DIM, m0, m1) + bs_ref[:, cols]
            oa_ref[rows, cols] = u[rows, cols] * mixed

    if seg_major:
        nqkv = 3 * (C_WIDTH // LANES)
        for p in range(S5_BLOCKS):
            slot = nqkv + p
            perm_scr[slot] = proj_scr[:, O_B + p * LANES:O_B + (p + 1) * LANES]
            for c4 in range(4):
                perm2_scr[slot, c4 * (tm // 4):(c4 + 1) * (tm // 4), :] = perm_scr[slot, pl.ds(c4, tm // 4, stride=4), :]
            f = SEG // 4
            for c4 in range(4):
                for s in range(f):
                    c0 = p * SEG_W + (c4 + 4 * s) * LANES
                    ub_ref[:, c0:c0 + LANES] = perm2_scr[slot, pl.ds(c4 * (tm // 4) + s, tm // SEG, stride=f), :]
    else:
        ub_ref[...] = proj_scr[:, O_B:O_Q]

    rc, ra, rb = rc_ref[...], ra_ref[...], rb_ref[...]
    def emit(val, outs, slot, p):
        outs[0][:, p * LANES:(p + 1) * LANES] = val.astype(BF16)
        if len(dils) == 1:
            return
        perm_scr[slot] = val
        src, dst, prev = perm_scr, perm2_scr, 1
        for n in range(1, len(dils)):
            dil = dils[n]
            f = dil // prev
            for c in range(prev):
                for s in range(f):
                    cls = c + prev * s
                    blk = src[slot, pl.ds(c * (tm // prev) + s, tm // dil, stride=f), :]
                    c0 = cls * C_WIDTH + p * LANES
                    outs[n][:, c0:c0 + LANES] = blk.astype(BF16)
                    if n + 1 < len(dils):
                        dst[slot, cls * (tm // dil):(cls + 1) * (tm // dil), :] = blk
            src, dst, prev = dst, src, dil

    nd = len(dils)
    assert dils[0] == 1
    q_outs, k_outs, v_outs = qkv_refs[0:nd], qkv_refs[nd:2 * nd], qkv_refs[2 * nd:3 * nd]
    for p in range(C_WIDTH // LANES):
        cols = slice(p * LANES, (p + 1) * LANES)
        q = proj_scr[:, O_Q + p * LANES:O_Q + (p + 1) * LANES]
        emit(_rope_apply(q, rc, ra, rb) * (HEAD_DIM ** -0.5), q_outs, 3 * p, p)
        k = _rope_apply(proj_scr[:, O_K + p * LANES:O_K + (p + 1) * LANES], rc, ra, rb)
        emit(k, k_outs, 3 * p + 1, p)
        kf_ref[:, cols] = k
        vv = proj_scr[:, O_V + p * LANES:O_V + (p + 1) * LANES]
        emit(vv, v_outs, 3 * p + 2, p)
        vf_ref[:, cols] = vv


def in_projection(x, g1, w_in, w_s, bs_rows, rope, *, tm, seq_rows, tail_rows, dils, seg_major=False):
    bsz, s, _ = x.shape
    nt = s // tm
    tail_t = tail_rows // tm
    row = lambda w: pl.BlockSpec((None, tm, w), lambda b, t: (b, t, 0))
    tail = pl.BlockSpec((None, tm, C_WIDTH), lambda b, t: (b, jnp.maximum(t - (nt - tail_t), 0), 0))
    full = lambda shp: pl.BlockSpec(shp, lambda b, t: (0,) * len(shp))
    ropespec = pl.BlockSpec((tm, LANES), lambda b, t: (t, 0))
    sd = lambda w, dt: jax.ShapeDtypeStruct((bsz, s, w), dt)
    tail_sd = jax.ShapeDtypeStruct((bsz, tail_rows, C_WIDTH), F32)
    qkv_specs = [pl.BlockSpec((None, tm // d, d * C_WIDTH), lambda b, t: (b, t, 0)) for d in dils] * 3
    qkv_sds = [jax.ShapeDtypeStruct((bsz, s // d, d * C_WIDTH), BF16) for d in dils] * 3
    if seg_major:
        ub_spec = pl.BlockSpec((None, tm // SEG, SEG * B_WIDTH), lambda b, t: (b, t, 0))
        ub_sd = jax.ShapeDtypeStruct((bsz, s // SEG, SEG * B_WIDTH), F32)
    else:
        ub_spec, ub_sd = row(B_WIDTH), sd(B_WIDTH, F32)
    n_perm = 3 * (C_WIDTH // LANES) + S5_BLOCKS
    return pl.pallas_call(
        functools.partial(_inproj_kernel, seq_rows=seq_rows, dils=tuple(dils), seg_major=seg_major),
        grid=(bsz, nt),
        in_specs=[row(D_MODEL), full((1, D_MODEL)), full(w_in.shape), full(w_s.shape), full(bs_rows.shape),
                  ropespec, ropespec, ropespec],
        out_specs=(row(A_WIDTH), row(A_WIDTH), ub_spec, tail, tail, *qkv_specs),
        out_shape=(sd(A_WIDTH, F32), sd(A_WIDTH, F32), ub_sd, tail_sd, tail_sd, *qkv_sds),
        scratch_shapes=[pltpu.VMEM((tm, w_in.shape[1]), F32),
                        pltpu.VMEM((n_perm, tm, LANES), F32),
                        pltpu.VMEM((n_perm, tm, LANES), F32)],
        compiler_params=_params("arbitrary", "arbitrary"),
        name="in_projection",
    )(x, g1, w_in, w_s, bs_rows, *rope)


def _s5_readout(hb, uperm, gs, wc_ref, d_ref, wg_ref, bg_ref):
    for j in range(S5_BLOCKS):
        cols = slice(j * LANES, (j + 1) * LANES)
        y = jnp.dot(hb[j], wc_ref[j], preferred_element_type=F32) + d_ref[:, cols] * uperm[:, cols]
        gs[:, cols] = jax.nn.gelu(y)
    g = gs[...]
    gs[...] = g * jax.nn.sigmoid(_dot(g, wg_ref[...]) + bg_ref[...])


def _s5_prompt_kernel(u_ref, wb_ref, wc_ref, pr_ref, pi_ref, d_ref, wg_ref, bg_ref,
                      o_ref, hr_ref, hi_ref,
                      upad, uperm, xs, hb, cs, carry, gs, *, seglen):
    pitch = seglen + SUBLANES

    @pl.when(pl.program_id(1) == 0)
    def _():
        carry[...] = jnp.zeros_like(carry)

    for s in range(SUBLANES):
        for p in range(S5_BLOCKS):
            upad[p, s * pitch:s * pitch + seglen, :] = u_ref[s * seglen:(s + 1) * seglen, p * LANES:(p + 1) * LANES]

    def perm(i, _):
        r = pl.multiple_of(i * SUBLANES, SUBLANES)
        for p in range(S5_BLOCKS):
            uperm[pl.ds(r, SUBLANES), p * LANES:(p + 1) * LANES] = upad[p, pl.ds(i, SUBLANES, stride=pitch), :]
        return 0
    lax.fori_loop(0, seglen, perm, 0)

    for j in range(S5_BLOCKS):
        xs[j] = _dot(uperm[:, j * LANES:(j + 1) * LANES], wb_ref[j])

    for j in range(S5_BLOCKS):
        ar = jnp.broadcast_to(pr_ref[j, 0:1, :], (SUBLANES, S5_HALF))
        ai = jnp.broadcast_to(pi_ref[j, 0:1, :], (SUBLANES, S5_HALF))

        def scan(i, hc, j=j, ar=ar, ai=ai):
            hr, hi = hc
            r = pl.multiple_of(i * SUBLANES, SUBLANES)
            nhr = ar * hr - ai * hi + xs[j, pl.ds(r, SUBLANES), 0:S5_HALF]
            nhi = ar * hi + ai * hr + xs[j, pl.ds(r, SUBLANES), S5_HALF:2 * S5_HALF]
            xs[j, pl.ds(r, SUBLANES), 0:S5_HALF] = nhr
            xs[j, pl.ds(r, SUBLANES), S5_HALF:2 * S5_HALF] = nhi
            return nhr, nhi
        z = jnp.zeros((SUBLANES, S5_HALF), F32)
        er, ei = lax.fori_loop(0, seglen, scan, (z, z))

        a64r, a64i = pr_ref[j, seglen - 1:seglen, :], pi_ref[j, seglen - 1:seglen, :]
        cr, ci = carry[j, 0, 0:1, :], carry[j, 1, 0:1, :]
        for s in range(SUBLANES):
            cs[j, 0, s:s + 1, :] = cr
            cs[j, 1, s:s + 1, :] = ci
            cr, ci = (a64r * cr - a64i * ci + er[s:s + 1, :], a64r * ci + a64i * cr + ei[s:s + 1, :])
        carry[j, 0, 0:1, :] = cr
        carry[j, 1, 0:1, :] = ci
        hr_ref[j] = cr
        hi_ref[j] = ci

        cr8, ci8 = cs[j, 0], cs[j, 1]

        def fix(i2, _, j=j, cr8=cr8, ci8=ci8):
            r = pl.multiple_of(i2 * 2 * SUBLANES, 2 * SUBLANES)
            hrs, his = [], []
            for t in range(2):
                i = i2 * 2 + t
                pr, pi = pr_ref[j, pl.ds(i, 1), :], pi_ref[j, pl.ds(i, 1), :]
                rows = pl.ds(r + t * SUBLANES, SUBLANES)
                hrs.append(xs[j, rows, 0:S5_HALF] + pr * cr8 - pi * ci8)
                his.append(xs[j, rows, S5_HALF:2 * S5_HALF] + pr * ci8 + pi * cr8)
            hb[j, pl.ds(r, 2 * SUBLANES), 0:S5_HALF] = jnp.concatenate(hrs, 0).astype(BF16)
            hb[j, pl.ds(r, 2 * SUBLANES), S5_HALF:2 * S5_HALF] = jnp.concatenate(his, 0).astype(BF16)
            return 0
        lax.fori_loop(0, seglen // 2, fix, 0, unroll=2)

    _s5_readout(hb, uperm, gs, wc_ref, d_ref, wg_ref, bg_ref)

    def unperm(i, _):
        r = pl.multiple_of(i * SUBLANES, SUBLANES)
        for p in range(S5_BLOCKS):
            upad[p, pl.ds(i, SUBLANES, stride=pitch), :] = gs[pl.ds(r, SUBLANES), p * LANES:(p + 1) * LANES]
        return 0
    lax.fori_loop(0, seglen, unperm, 0)
    for s in range(SUBLANES):
        for p in range(S5_BLOCKS):
            o_ref[s * seglen:(s + 1) * seglen, p * LANES:(p + 1) * LANES] = upad[p, s * pitch:s * pitch + seglen, :]


def s5_prompt(u, wb, wc, prl, pil, d_skip, w_glu, b_glu, *, seglen):
    bsz, s, _ = u.shape
    t = SUBLANES * seglen
    pitch = seglen + SUBLANES
    full = lambda shp: pl.BlockSpec(shp, lambda b, c: (0,) * len(shp))
    st = pl.BlockSpec((None, S5_BLOCKS, 1, S5_HALF), lambda b, c: (b, 0, 0, 0))
    rows = pl.BlockSpec((None, t, B_WIDTH), lambda b, c: (b, c, 0))
    return pl.pallas_call(
        functools.partial(_s5_prompt_kernel, seglen=seglen),
        grid=(bsz, s // t),
        in_specs=[rows, full(wb.shape), full(wc.shape), full(prl.shape), full(pil.shape),
                  full((1, B_WIDTH)), full(w_glu.shape), full((1, B_WIDTH))],
        out_specs=(rows, st, st),
        out_shape=(jax.ShapeDtypeStruct((bsz, s, B_WIDTH), F32),
                   jax.ShapeDtypeStruct((bsz, S5_BLOCKS, 1, S5_HALF), F32),
                   jax.ShapeDtypeStruct((bsz, S5_BLOCKS, 1, S5_HALF), F32)),
        scratch_shapes=[pltpu.VMEM((S5_BLOCKS, SUBLANES * pitch, LANES), F32),
                        pltpu.VMEM((t, B_WIDTH), F32),
                        pltpu.VMEM((S5_BLOCKS, t, 2 * S5_HALF), F32),
                        pltpu.VMEM((S5_BLOCKS, t, 2 * S5_HALF), BF16),
                        pltpu.VMEM((S5_BLOCKS, 2, SUBLANES, S5_HALF), F32),
                        pltpu.VMEM((S5_BLOCKS, 2, SUBLANES, S5_HALF), F32),
                        pltpu.VMEM((t, B_WIDTH), F32)],
        compiler_params=_params("arbitrary", "arbitrary"),
        name="s5_prompt",
    )(u, wb, wc, prl, pil, d_skip, w_glu, b_glu)


def _s5_seg_kernel(u_ref, t_ref, eb_ref, gm_ref, a8r_ref, a8i_ref, d_ref, wg_ref, bg_ref,
                   o_ref, hr_ref, hi_ref, e_scr, cs_scr, gs_scr, carry):
    nseg = u_ref.shape[0]

    @pl.when(pl.program_id(1) == 0)
    def _():
        carry[...] = jnp.zeros_like(carry)

    for p in range(S5_BLOCKS):
        ubf = u_ref[:, p * SEG_W:(p + 1) * SEG_W].astype(BF16)
        e_scr[p] = jnp.dot(ubf, eb_ref[p], preferred_element_type=F32)

    sub = lax.broadcasted_iota(jnp.int32, (SUBLANES, S5_HALF), 0)
    a8 = [(a8r_ref[p], a8i_ref[p]) for p in range(S5_BLOCKS)]

    def block(k, c):
        r0 = pl.multiple_of(k * SUBLANES, SUBLANES)
        out = []
        for p in range(S5_BLOCKS):
            cr, ci = c[2 * p], c[2 * p + 1]
            er = e_scr[p, pl.ds(r0, SUBLANES), 0:S5_HALF]
            ei = e_scr[p, pl.ds(r0, SUBLANES), S5_HALF:2 * S5_HALF]
            br = jnp.zeros((SUBLANES, S5_HALF), F32)
            bi = jnp.zeros((SUBLANES, S5_HALF), F32)
            for j in range(SUBLANES):
                br = jnp.where(sub == j, cr, br)
                bi = jnp.where(sub == j, ci, bi)
                cr, ci = (a8[p][0] * cr - a8[p][1] * ci + er[j:j + 1, :],
                          a8[p][0] * ci + a8[p][1] * cr + ei[j:j + 1, :])
            cs_scr[p, pl.ds(r0, SUBLANES), 0:S5_HALF] = br
            cs_scr[p, pl.ds(r0, SUBLANES), S5_HALF:2 * S5_HALF] = bi
            out += [cr, ci]
        return tuple(out)

    c0 = tuple(carry[p, e, 0:1, :] for p in range(S5_BLOCKS) for e in range(2))
    cf = lax.fori_loop(0, nseg // SUBLANES, block, c0)
    for p in range(S5_BLOCKS):
        carry[p, 0, 0:1, :] = cf[2 * p]
        carry[p, 1, 0:1, :] = cf[2 * p + 1]
        hr_ref[p] = cf[2 * p]
        hi_ref[p] = cf[2 * p + 1]

    for p in range(S5_BLOCKS):
        cols = slice(p * SEG_W, (p + 1) * SEG_W)
        u = u_ref[:, cols]
        y = (jnp.dot(u.astype(BF16), t_ref[p], preferred_element_type=F32)
             + lax.dot_general(cs_scr[p].astype(BF16), gm_ref[p], (((1,), (1,)), ((), ())),
                               preferred_element_type=F32) + d_ref[:, cols] * u)
        gs_scr[:, cols] = jax.nn.gelu(y)
    for r in range(SEG):
        tok = [slice(p * SEG_W + r * LANES, p * SEG_W + (r + 1) * LANES) for p in range(S5_BLOCKS)]
        g = jnp.concatenate([gs_scr[:, c] for c in tok], axis=1)
        out = g * jax.nn.sigmoid(_dot(g, wg_ref[...]) + bg_ref[...])
        for p in range(S5_BLOCKS):
            o_ref[:, tok[p]] = out[:, p * LANES:(p + 1) * LANES]


def s5_seg(u, t, eb, gm, a8r, a8i, d_t, w_glu, b_glu, *, rows):
    bsz, nseg, width = u.shape
    resident = lambda a: pl.BlockSpec(a.shape, lambda b, c: (0,) * a.ndim, pipeline_mode=pl.Buffered(1))
    blk = pl.BlockSpec((None, rows, width), lambda b, c: (b, c, 0))
    st = pl.BlockSpec((None, S5_BLOCKS, 1, S5_HALF), lambda b, c: (b, 0, 0, 0))
    st_sd = jax.ShapeDtypeStruct((bsz, S5_BLOCKS, 1, S5_HALF), F32)
    return pl.pallas_call(
        _s5_seg_kernel,
        grid=(bsz, nseg // rows),
        in_specs=[blk] + [resident(a) for a in (t, eb, gm, a8r, a8i, d_t, w_glu, b_glu)],
        out_specs=(blk, st, st),
        out_shape=(jax.ShapeDtypeStruct(u.shape, F32), st_sd, st_sd),
        scratch_shapes=[pltpu.VMEM((S5_BLOCKS, rows, 2 * S5_HALF), F32),
                        pltpu.VMEM((S5_BLOCKS, rows, 2 * S5_HALF), F32),
                        pltpu.VMEM((rows, width), F32),
                        pltpu.VMEM((S5_BLOCKS, 2, SUBLANES, S5_HALF), F32)],
        compiler_params=_params("arbitrary", "arbitrary"),
        name="s5_seg",
    )(u, t, eb, gm, a8r, a8i, d_t, w_glu, b_glu)


def _s5_sample_kernel(u_ref, h0r_ref, h0i_ref, wb_ref, wc_ref, pr_ref, pi_ref, d_ref, wg_ref, bg_ref,
                      o_ref, hr_ref, hi_ref,
                      upad, uperm, xs, hb, gs, *, nseq, steps):
    nblk = nseq // SUBLANES
    for p in range(S5_BLOCKS):
        upad[p] = u_ref[:, p * LANES:(p + 1) * LANES]
    for st in range(steps):
        for bb in range(nblk):
            r = (st * nblk + bb) * SUBLANES
            for p in range(S5_BLOCKS):
                uperm[r:r + SUBLANES, p * LANES:(p + 1) * LANES] = (
                    upad[p, pl.ds(bb * SUBLANES * steps + st, SUBLANES, stride=steps), :])
    for j in range(S5_BLOCKS):
        xs[j] = _dot(uperm[:, j * LANES:(j + 1) * LANES], wb_ref[j])
    for j in range(S5_BLOCKS):
        ar = jnp.broadcast_to(pr_ref[j, 0:1, :], (SUBLANES, S5_HALF))
        ai = jnp.broadcast_to(pi_ref[j, 0:1, :], (SUBLANES, S5_HALF))
        for bb in range(nblk):
            seqs = slice(bb * SUBLANES, (bb + 1) * SUBLANES)
            hr, hi = h0r_ref[j, seqs, :], h0i_ref[j, seqs, :]
            for st in range(steps):
                r = (st * nblk + bb) * SUBLANES
                hr, hi = (ar * hr - ai * hi + xs[j, r:r + SUBLANES, 0:S5_HALF],
                          ar * hi + ai * hr + xs[j, r:r + SUBLANES, S5_HALF:2 * S5_HALF])
                xs[j, r:r + SUBLANES, 0:S5_HALF] = hr
                xs[j, r:r + SUBLANES, S5_HALF:2 * S5_HALF] = hi
            hr_ref[j, seqs, :] = hr
            hi_ref[j, seqs, :] = hi
        hb[j] = xs[j].astype(BF16)
    _s5_readout(hb, uperm, gs, wc_ref, d_ref, wg_ref, bg_ref)
    for st in range(steps):
        for bb in range(nblk):
            r = (st * nblk + bb) * SUBLANES
            for p in range(S5_BLOCKS):
                upad[p, pl.ds(bb * SUBLANES * steps + st, SUBLANES, stride=steps), :] = (
                    gs[r:r + SUBLANES, p * LANES:(p + 1) * LANES])
    for p in range(S5_BLOCKS):
        o_ref[:, p * LANES:(p + 1) * LANES] = upad[p]


def s5_sample(u, h0r, h0i, wb, wc, prl, pil, d_skip, w_glu, b_glu, *, nseq, steps):
    n = nseq * steps
    st_sd = jax.ShapeDtypeStruct((S5_BLOCKS, nseq, S5_HALF), F32)
    return pl.pallas_call(
        functools.partial(_s5_sample_kernel, nseq=nseq, steps=steps),
        out_shape=(jax.ShapeDtypeStruct((n, B_WIDTH), F32), st_sd, st_sd),
        scratch_shapes=[pltpu.VMEM((S5_BLOCKS, n, LANES), F32),
                        pltpu.VMEM((n, B_WIDTH), F32),
                        pltpu.VMEM((S5_BLOCKS, n, 2 * S5_HALF), F32),
                        pltpu.VMEM((S5_BLOCKS, n, 2 * S5_HALF), BF16),
                        pltpu.VMEM((n, B_WIDTH), F32)],
        compiler_params=pltpu.CompilerParams(vmem_limit_bytes=VMEM_LIMIT),
        name="s5_sample",
    )(u, h0r, h0i, wb, wc, prl, pil, d_skip, w_glu, b_glu)


def _attn_prompt_kernel(q_ref, kp_ref, kc_ref, vp_ref, vc_ref, o_ref, st_ref, kx, vx):
    tq = q_ref.shape[0]
    t = pl.program_id(2)
    kx[0:CHUNK, :] = kp_ref[...]
    kx[CHUNK:CHUNK + tq, :] = kc_ref[...]
    vx[0:CHUNK, :] = vp_ref[...]
    vx[CHUNK:CHUNK + tq, :] = vc_ref[...]
    r = lax.broadcasted_iota(jnp.int32, (CHUNK, 2 * CHUNK), 0)
    c = lax.broadcasted_iota(jnp.int32, (CHUNK, 2 * CHUNK), 1)
    band = (c >= r) & (c <= r + CHUNK)
    lane = lax.broadcasted_iota(jnp.int32, (CHUNK, LANES), 1)
    lo_half = lane < HEAD_DIM
    for u in range(tq // CHUNK):
        rows = slice(u * CHUNK, (u + 1) * CHUNK)
        keys = slice(u * CHUNK, (u + 2) * CHUNK)
        if u == 0:
            valid = band & (c >= jnp.where(t == 0, CHUNK, 0))
        else:
            valid = band
        bias = jnp.where(valid, 0.0, NEG_INF)
        st = jnp.zeros((CHUNK, LANES), F32)
        for p in range(C_WIDTH // LANES):
            cols = slice(p * LANES, (p + 1) * LANES)
            qp = q_ref[rows, cols]
            kpair = kx[keys, cols]
            vpair = vx[keys, cols]
            outs = []
            for hh in range(2):
                qm = jnp.where(lo_half if hh == 0 else ~lo_half, qp, jnp.zeros_like(qp))
                sc = lax.dot_general(qm, kpair, (((1,), (1,)), ((), ())), preferred_element_type=F32) + bias
                m = jnp.max(sc, -1, keepdims=True)
                pe = jnp.exp(sc - m)
                l = jnp.sum(pe, -1, keepdims=True)
                outs.append(jnp.dot(pe.astype(BF16), vpair, preferred_element_type=F32) / l)
                st = jnp.where(lane == 2 * p + hh, m + jnp.log(l), st)
            o_ref[rows, cols] = jnp.where(lo_half, outs[0], outs[1])
        st_ref[rows, :] = st


def attn_prompt(q, k, v, dil, *, tq):
    bsz, ln, _ = q.shape
    tq = min(tq, ln)
    cur = pl.BlockSpec((None, tq, C_WIDTH), lambda b, r, t: (b, t, r))
    prev = pl.BlockSpec((None, CHUNK, C_WIDTH), lambda b, r, t: (b, jnp.maximum(t * (tq // CHUNK) - 1, 0), r))
    return pl.pallas_call(
        _attn_prompt_kernel,
        grid=(bsz, dil, ln // tq),
        in_specs=[cur, prev, cur, prev, cur],
        out_specs=(cur, pl.BlockSpec((None, tq, LANES), lambda b, r, t: (b, t, r))),
        out_shape=(jax.ShapeDtypeStruct((bsz, ln, dil * C_WIDTH), F32),
                   jax.ShapeDtypeStruct((bsz, ln, dil * LANES), F32)),
        scratch_shapes=[pltpu.VMEM((CHUNK + tq, C_WIDTH), BF16), pltpu.VMEM((CHUNK + tq, C_WIDTH), BF16)],
        compiler_params=_params("arbitrary", "arbitrary", "arbitrary"),
        name=f"attn_prompt_d{dil}",
    )(q, k, k, v, v)


def _attn_sample_kernel(q_ref, kn_ref, vn_ref, ck_ref, cv_ref, o_ref, *, steps):
    cw = ck_ref.shape[-1]

    def mult(delta):
        cnt = jnp.zeros(delta.shape, F32)
        for window, dil in DILATIONS:
            cnt = cnt + jnp.where((delta >= 0) & (delta <= window) & ((delta & (dil - 1)) == 0), 1.0, 0.0)
        return cnt

    cnt_c = mult(cw + lax.broadcasted_iota(jnp.int32, (steps, cw), 0)
                 - lax.broadcasted_iota(jnp.int32, (steps, cw), 1))
    cnt_n = mult(lax.broadcasted_iota(jnp.int32, (steps, CHUNK), 0)
                 - lax.broadcasted_iota(jnp.int32, (steps, CHUNK), 1))
    live_c, live_n = cnt_c > 0, cnt_n > 0
    zpad = jnp.zeros((CHUNK - steps, C_WIDTH), F32)
    kn = jnp.concatenate([kn_ref[...], zpad], 0).astype(BF16)
    vn = jnp.concatenate([vn_ref[...], zpad], 0).astype(BF16)
    q = q_ref[...].astype(BF16)
    nt_dims = (((1,), (1,)), ((), ()))
    for hh in range(C_HEADS):
        cols = slice(hh * HEAD_DIM, (hh + 1) * HEAD_DIM)
        qh = q[:, cols]
        sc = jnp.dot(qh, ck_ref[hh].astype(BF16), preferred_element_type=F32)
        sn = lax.dot_general(qh, kn[:, cols], nt_dims, preferred_element_type=F32)
        m = jnp.maximum(jnp.max(jnp.where(live_c, sc, NEG_INF), -1, keepdims=True),
                        jnp.max(jnp.where(live_n, sn, NEG_INF), -1, keepdims=True))
        ec = cnt_c * jnp.exp(jnp.where(live_c, sc - m, NEG_INF))
        en = cnt_n * jnp.exp(jnp.where(live_n, sn - m, NEG_INF))
        l = jnp.sum(ec, -1, keepdims=True) + jnp.sum(en, -1, keepdims=True)
        acc = (lax.dot_general(ec.astype(BF16), cv_ref[hh].astype(BF16), nt_dims, preferred_element_type=F32)
               + jnp.dot(en.astype(BF16), vn[:, cols], preferred_element_type=F32))
        o_ref[:, cols] = acc / l


def attn_sample(q, k_new, v_new, cache_kt, cache_vt, layer, *, steps):
    _, nseq, _, _, cw = cache_kt.shape
    rows = pl.BlockSpec((steps, C_WIDTH), lambda b: (b, 0))
    cache = pl.BlockSpec((None, None, C_HEADS, HEAD_DIM, cw), lambda b: (layer, b, 0, 0, 0))
    return pl.pallas_call(
        functools.partial(_attn_sample_kernel, steps=steps),
        grid=(nseq,),
        in_specs=[rows, rows, rows, cache, cache],
        out_specs=rows,
        out_shape=jax.ShapeDtypeStruct((nseq * steps, C_WIDTH), F32),
        compiler_params=_params("arbitrary"),
        name="attn_sample",
    )(q, k_new, v_new, cache_kt, cache_vt)


FF_ROWS = 128
FF_PIECES = 4


def _residual_and_norm(x_ref, oa, ob, oc, wo_ref, g2_ref, y_ref, xn_scr):
    mix = (_dot(oa, wo_ref[0:A_WIDTH, :]) + _dot(ob, wo_ref[A_WIDTH:A_WIDTH + B_WIDTH, :])
           + _dot(oc, wo_ref[A_WIDTH + B_WIDTH:, :]))
    x1 = x_ref[...] + mix
    y_ref[...] = x1
    xn_scr[...] = _rms(x1, g2_ref[...]).astype(BF16)


def _outproj_prompt_kernel(x_ref, oa_ref, ob_ref, *rest, tiles_per_seq, dils, final_norm):
    nb = len(dils)
    o_refs, st_refs = rest[:nb], rest[nb:2 * nb]
    (e_ref, wo_ref, g2_ref, wup_ref, cw_ref, cb_ref, wd_ref, fg_ref,
     y_ref, cg_ref, cu_ref, xn_scr, carry_scr, unp_scr, hs0, hs1, act0, act1) = rest[2 * nb:]
    hs_scr, act_scr = (hs0, hs1), (act0, act1)
    i = pl.program_id(0)
    tm = x_ref.shape[0]
    nslab = C_WIDTH // LANES

    outs, sts = [], []
    slot = 0
    for dil, o_ref, s_ref in zip(dils, o_refs, st_refs):
        if dil == 1:
            outs.append(o_ref[...])
            sts.append(s_ref[...])
            continue
        for r in range(dil):
            rows = pl.ds(r, tm // dil, stride=dil)
            for p in range(nslab):
                c0 = r * C_WIDTH + p * LANES
                unp_scr[slot + p, rows, :] = o_ref[:, c0:c0 + LANES]
            unp_scr[slot + nslab, rows, :] = s_ref[:, r * LANES:(r + 1) * LANES]
        outs.append(jnp.concatenate([unp_scr[slot + p] for p in range(nslab)], axis=1))
        sts.append(unp_scr[slot + nslab])
        slot += nslab + 1
    mx = functools.reduce(jnp.maximum, sts)
    ws = [jnp.exp(s - mx) for s in sts]
    tot = functools.reduce(lambda a, b: a + b, ws)
    oc = jnp.zeros((tm, C_WIDTH), F32)
    for w, o in zip(ws, outs):
        alpha = w / tot
        hi = alpha.astype(BF16)
        lo = (alpha - hi.astype(F32)).astype(BF16)
        wide = jnp.dot(jnp.concatenate([hi, lo], axis=1), e_ref[...], preferred_element_type=F32)
        oc = oc + wide * o
    for p in range(S5_BLOCKS):
        for r in range(SEG):
            c0 = p * SEG_W + r * LANES
            unp_scr[slot + p, pl.ds(r, tm // SEG, stride=SEG), :] = ob_ref[:, c0:c0 + LANES]
    ob = jnp.concatenate([unp_scr[slot + p] for p in range(S5_BLOCKS)], axis=1)
    _residual_and_norm(x_ref, oa_ref[...], ob, oc, wo_ref, g2_ref, y_ref, xn_scr)

    @pl.when((i % tiles_per_seq) == 0)
    def _():
        carry_scr[...] = jnp.zeros_like(carry_scr)

    nhs = 2 * FF_CHUNK // LANES
    gate = FF_CHUNK // LANES
    piece = tm // FF_PIECES

    def up(jj, slot, pc):
        xr = xn_scr[pc * piece:(pc + 1) * piece, :]
        rows = slice(SUBLANES + pc * piece, SUBLANES + (pc + 1) * piece)
        for half in range(2):
            h = jnp.dot(xr, wup_ref[half * N_FF + jj], preferred_element_type=F32)
            for k in range(gate):
                hs_scr[slot][2 * (half * gate + k), rows, :] = h[:, k * LANES:(k + 1) * LANES]

    def down(jj, slot, pc):
        rows = slice(pc * piece, (pc + 1) * piece)
        y_ref[rows, :] += jnp.dot(act_scr[slot][rows, :], wd_ref[jj], preferred_element_type=F32)

    def conv_taps(jj, slot):
        for k in range(nhs):
            hs_scr[slot][2 * k, 0:SUBLANES, :] = carry_scr[jj, k]
        cwj = jnp.concatenate([cw_ref[jj], cw_ref[N_FF + jj]], axis=1)
        cbj = jnp.concatenate([cb_ref[jj], cb_ref[N_FF + jj]], axis=1)
        slabs = lambda row: jnp.stack([row[:, k * LANES:(k + 1) * LANES] for k in range(nhs)])
        return [slabs(cwj[t:t + 1]) for t in range(3)], slabs(cbj)

    def conv(jj, slot, pc, taps, bias):
        every_other = pl.ds(0, nhs, stride=2)
        for r in range(pc * piece // FF_ROWS, (pc + 1) * piece // FF_ROWS):
            shifted = lambda back: hs_scr[slot][every_other, pl.ds(SUBLANES - back + r * FF_ROWS, FF_ROWS), :]
            hc = bias + (shifted(2) * taps[0] + shifted(1) * taps[1] + shifted(0) * taps[2])
            act = jax.nn.gelu(hc[:gate]) * hc[gate:]
            for k in range(gate):
                act_scr[slot][r * FF_ROWS:(r + 1) * FF_ROWS, k * LANES:(k + 1) * LANES] = act[k].astype(BF16)

    def conv_tail(jj, slot):
        for k in range(nhs):
            last = hs_scr[slot][2 * k, tm:tm + SUBLANES, :]
            carry_scr[jj, k] = last
            if k < gate:
                cg_ref[0, jj, :, k * LANES:(k + 1) * LANES] = last
            else:
                cu_ref[0, jj, :, (k - gate) * LANES:(k - gate + 1) * LANES] = last

    def stage(j, slot, do_down, do_up):
        taps, bias = conv_taps(j, slot)
        for pc in range(FF_PIECES):
            conv(j, slot, pc, taps, bias)
            if do_down:
                down(j - 1, 1 - slot, pc)
            if do_up:
                up(j + 1, 1 - slot, pc)
        conv_tail(j, slot)

    for pc in range(FF_PIECES):
        up(0, 0, pc)
    stage(0, 0, False, True)

    def steady(n, _):
        stage(2 * n + 1, 1, True, True)
        stage(2 * n + 2, 0, True, True)
        return 0
    n_pairs = (N_FF - 2) // 2
    lax.fori_loop(0, n_pairs, steady, 0)
    for j in range(2 * n_pairs + 1, N_FF):
        stage(j, j % 2, True, j + 1 < N_FF)
    for pc in range(FF_PIECES):
        down(N_FF - 1, (N_FF - 1) % 2, pc)

    if final_norm:
        y_ref[...] = _rms(y_ref[...], fg_ref[...])


def _outproj_sample_kernel(x_ref, oa_ref, ob_ref, oc_ref, wo_ref, g2_ref, wug_ref, wuu_ref, cwg_ref, cwu_ref,
                           cbg_ref, cbu_ref, wd_ref, fg_ref, bufg_ref, bufu_ref,
                           y_ref, cg_ref, cu_ref, xn_scr, hs_scr, *, nseq, seq_len, final_norm):
    j = pl.program_id(1)
    tm = x_ref.shape[0]

    @pl.when(j == 0)
    def _():
        _residual_and_norm(x_ref, oa_ref[...], ob_ref[...], oc_ref[...], wo_ref, g2_ref, y_ref, xn_scr)

    xn = xn_scr[...]
    hg = jnp.dot(xn, wug_ref[...], preferred_element_type=F32)
    hu = jnp.dot(xn, wuu_ref[...], preferred_element_type=F32)
    lo, hi_ = SUBLANES, SUBLANES + seq_len
    hs_scr[:, lo:hi_, 0:FF_CHUNK] = hg.reshape(nseq, seq_len, FF_CHUNK)
    hs_scr[:, lo:hi_, FF_CHUNK:] = hu.reshape(nseq, seq_len, FF_CHUNK)
    hs_scr[:, lo - 2:lo, 0:FF_CHUNK] = bufg_ref[...]
    hs_scr[:, lo - 2:lo, FF_CHUNK:] = bufu_ref[...]
    cg_ref[...] = hs_scr[:, hi_ - SUBLANES:hi_, 0:FF_CHUNK]
    cu_ref[...] = hs_scr[:, hi_ - SUBLANES:hi_, FF_CHUNK:]

    cw = jnp.concatenate([cwg_ref[...], cwu_ref[...]], axis=1)
    cb = jnp.concatenate([cbg_ref[...], cbu_ref[...]], axis=1)
    hc = cb + (hs_scr[:, lo - 2:hi_ - 2, :] * cw[0:1] + hs_scr[:, lo - 1:hi_ - 1, :] * cw[1:2]
               + hs_scr[:, lo:hi_, :] * cw[2:3])
    hc = hc.reshape(tm, 2 * FF_CHUNK)
    act = jax.nn.gelu(hc[:, :FF_CHUNK]) * hc[:, FF_CHUNK:]
    y_ref[...] += _dot(act, wd_ref[...])

    if final_norm:
        @pl.when(j == pl.num_programs(1) - 1)
        def _():
            y_ref[...] = _rms(y_ref[...], fg_ref[...])


def ffn_layouts(w_up, conv_w, conv_b, w_down):
    chunks = lambda a: jnp.transpose(a.reshape(a.shape[0], 2 * N_FF, FF_CHUNK), (1, 0, 2))
    return (chunks(w_up), chunks(conv_w), chunks(conv_b.reshape(1, -1)), w_down.reshape(N_FF, FF_CHUNK, D_MODEL))


def out_projection_prompt(x, oa, ob, o_list, st_list, dils, expand, ffw, *, tm, tiles_per_seq, final_norm):
    n = x.shape[0]
    nt = n // tm
    row = lambda w: pl.BlockSpec((tm, w), lambda i: (i, 0))
    perm = lambda w: [pl.BlockSpec((tm // d, d * w), lambda i: (i, 0)) for d in dils]
    resident = lambda a: pl.BlockSpec(a.shape, lambda i: (0,) * a.ndim, pipeline_mode=pl.Buffered(1))
    conv_spec = pl.BlockSpec((1, N_FF, SUBLANES, FF_CHUNK), lambda i: (i, 0, 0, 0))
    conv_sd = jax.ShapeDtypeStruct((nt, N_FF, SUBLANES, FF_CHUNK), F32)
    n_unp = sum(C_WIDTH // LANES + 1 for d in dils if d > 1) + S5_BLOCKS
    seg_rows = pl.BlockSpec((tm // SEG, SEG * B_WIDTH), lambda i: (i, 0))
    nhs = 2 * FF_CHUNK // LANES
    return pl.pallas_call(
        functools.partial(_outproj_prompt_kernel, tiles_per_seq=tiles_per_seq, dils=tuple(dils),
                          final_norm=final_norm),
        grid=(nt,),
        in_specs=[row(D_MODEL), row(A_WIDTH), seg_rows] + perm(C_WIDTH) + perm(LANES)
                 + [resident(expand)] + [resident(a) for a in ffw],
        out_specs=(row(D_MODEL), conv_spec, conv_spec),
        out_shape=(jax.ShapeDtypeStruct((n, D_MODEL), F32), conv_sd, conv_sd),
        scratch_shapes=[pltpu.VMEM((tm, D_MODEL), BF16),
                        pltpu.VMEM((N_FF, nhs, SUBLANES, LANES), F32),
                        pltpu.VMEM((n_unp, tm, LANES), F32),
                        pltpu.VMEM((2 * nhs, SUBLANES + tm, LANES), F32),
                        pltpu.VMEM((2 * nhs, SUBLANES + tm, LANES), F32),
                        pltpu.VMEM((tm, FF_CHUNK), BF16),
                        pltpu.VMEM((tm, FF_CHUNK), BF16)],
        compiler_params=_params("arbitrary"),
        name="out_projection_prompt",
    )(x, oa, ob, *o_list, *st_list, expand, *ffw)


def out_projection_sample(x, oa, ob, oc, ffw, bufs, *, nseq, seq_len, final_norm):
    n = x.shape[0]
    row = lambda w: pl.BlockSpec((n, w), lambda i, j: (0, 0))
    full = lambda a: pl.BlockSpec(a.shape, lambda i, j: (0,) * a.ndim)
    w_out, g2, w_up, conv_w, conv_b, w_down, fg = ffw
    chunk = lambda a, off: pl.BlockSpec((None,) + a.shape[1:], lambda i, j: (off + j, 0, 0))
    wspecs = [full(w_out), full(g2), chunk(w_up, 0), chunk(w_up, N_FF), chunk(conv_w, 0), chunk(conv_w, N_FF),
              chunk(conv_b, 0), chunk(conv_b, N_FF), chunk(w_down, 0), full(fg)]
    wargs = [w_out, g2, w_up, w_up, conv_w, conv_w, conv_b, conv_b, w_down, fg]
    conv_spec = pl.BlockSpec((nseq, SUBLANES, FF_CHUNK), lambda i, j: (0, 0, j))
    return pl.pallas_call(
        functools.partial(_outproj_sample_kernel, nseq=nseq, seq_len=seq_len, final_norm=final_norm),
        grid=(1, N_FF),
        in_specs=[row(D_MODEL), row(A_WIDTH), row(B_WIDTH), row(C_WIDTH)] + wspecs
                 + [pl.BlockSpec((nseq, 2, FF_CHUNK), lambda i, j: (0, 0, j)),
                    pl.BlockSpec((nseq, 2, FF_CHUNK), lambda i, j: (0, 0, N_FF + j))],
        out_specs=(row(D_MODEL), conv_spec, conv_spec),
        out_shape=(jax.ShapeDtypeStruct((n, D_MODEL), F32),
                   jax.ShapeDtypeStruct((nseq, SUBLANES, D_FF), F32),
                   jax.ShapeDtypeStruct((nseq, SUBLANES, D_FF), F32)),
        scratch_shapes=[pltpu.VMEM((n, D_MODEL), BF16),
                        pltpu.VMEM((nseq, SUBLANES + seq_len, 2 * FF_CHUNK), F32)],
        compiler_params=_params("arbitrary", "arbitrary"),
        name="out_projection_sample",
    )(x, oa, ob, oc, *wargs, bufs, bufs)


PROMPT_TM = 512
S5_ROWS = 256
ATTN_TQ = 512


def kernel(x_prompt, x_sample, cache_c_k, cache_c_v, state_ssm_re, state_ssm_im, state_ffn_conv, norm1_g, w_in, w_s, b_s, ssm_lam_re, ssm_lam_im, ssm_log_dt, ssm_b_re, ssm_b_im, ssm_c_re, ssm_c_im, ssm_d, w_glu, b_glu, w_out, norm2_g, w_up, conv_w, conv_b, w_down, final_g):
    bp, sp, _ = x_prompt.shape
    nseq, steps, _ = x_sample.shape
    ns = nseq * steps
    cw_p = min(MAX_WINDOW, sp)
    cw_s = cache_c_k.shape[2]

    rope_p = rope_tables(sp, 0)
    rope_s = tuple(jnp.tile(t, (nseq, 1)) for t in rope_tables(steps, PAST_LEN))
    bbr, bbi, pr, pi, ncim = s5_params(ssm_lam_re, ssm_lam_im, ssm_log_dt, ssm_b_re, ssm_b_im, ssm_c_im, SEG)
    lag_k, abr, abi, car, cani = s5_seg_params(pr, pi, bbr, bbi, ssm_c_re, ncim)
    segw = s5_seg_layouts(lag_k, abr, abi, car, cani, pr, pi, ssm_d)

    w_in_b, w_out_b, w_up_b, w_down_b, w_glu_b = (w.astype(BF16) for w in (w_in, w_out, w_up, w_down, w_glu))
    expand = (jnp.arange(2 * LANES)[:, None] % LANES == jnp.arange(C_WIDTH)[None, :] // HEAD_DIM).astype(BF16)
    bs_p = jnp.repeat(jnp.transpose(b_s, (0, 2, 1)), HEAD_DIM, axis=2)
    bs_s = jnp.tile(bs_p[:, :steps], (1, CHUNK // steps, 1))
    ws_s = jnp.tile(w_s[:, :, :steps, :steps], (1, 1, CHUNK // steps, CHUNK // steps))
    row1 = lambda a: a.reshape(1, -1)

    cache_kt = jnp.transpose(cache_c_k, (0, 1, 3, 4, 2))
    cache_vt = jnp.transpose(cache_c_v, (0, 1, 3, 4, 2))
    dils = tuple(d for _, d in DILATIONS)

    hp = x_prompt
    hs = x_sample.reshape(1, ns, D_MODEL)
    outs = [[] for _ in range(11)]
    for l in range(DEPTH):
        last = l == DEPTH - 1
        wb, wc, prl, pil = s5_layouts(bbr[l], bbi[l], ssm_c_re[l], ncim[l], pr[l], pi[l])
        s5w = (wb, wc, prl, pil, row1(ssm_d[l]), w_glu_b[l], row1(b_glu[l]))
        ffw = (w_out_b[l], row1(norm2_g[l]), *ffn_layouts(w_up_b[l], conv_w[l], conv_b[l], w_down_b[l]),
               row1(final_g))

        oa, _, ub, kf, vf, *qkv = in_projection(hp, row1(norm1_g[l]), w_in_b[l], w_s[l], bs_p[l], rope_p,
                                                tm=PROMPT_TM, seq_rows=CHUNK, tail_rows=cw_p, dils=dils,
                                                seg_major=True)
        nd = len(dils)
        ob, hr, hi = s5_seg(ub, *(a[l] for a in segw), w_glu_b[l], row1(b_glu[l]), rows=S5_ROWS)
        o_list, st_list = zip(*[attn_prompt(qkv[n], qkv[nd + n], qkv[2 * nd + n], dil, tq=ATTN_TQ)
                                for n, dil in enumerate(dils)])
        flat = lambda a: a.reshape(a.shape[0] * a.shape[1], a.shape[2])
        y, cg, cu = out_projection_prompt(flat(hp), flat(oa), flat(ob), [flat(o) for o in o_list],
                                          [flat(s) for s in st_list], dils, expand, ffw,
                                          tm=PROMPT_TM, tiles_per_seq=sp // PROMPT_TM, final_norm=last)
        hp = y.reshape(bp, sp, D_MODEL)
        outs[0].append(kf.reshape(bp, cw_p, C_HEADS, HEAD_DIM))
        outs[1].append(vf.reshape(bp, cw_p, C_HEADS, HEAD_DIM))
        outs[4].append(hr.reshape(bp, B_GROUPS, SSM_STATE))
        outs[5].append(hi.reshape(bp, B_GROUPS, SSM_STATE))
        seq_end = lambda a: jnp.transpose(a.reshape(bp, sp // PROMPT_TM, N_FF, SUBLANES, FF_CHUNK)[:, -1, :, -2:],
                                          (0, 2, 1, 3)).reshape(bp, 2, D_FF)
        outs[8].append(jnp.concatenate([seq_end(cg), seq_end(cu)], -1))

        oa, vn, ub, kf, vf, q, k, v = in_projection(hs, row1(norm1_g[l]), w_in_b[l], ws_s[l], bs_s[l], rope_s,
                                                    tm=ns, seq_rows=steps, tail_rows=ns, dils=(1,))
        h0r = jnp.transpose(state_ssm_re[l].reshape(nseq, S5_BLOCKS, S5_HALF), (1, 0, 2))
        h0i = jnp.transpose(state_ssm_im[l].reshape(nseq, S5_BLOCKS, S5_HALF), (1, 0, 2))
        ob, hr, hi = s5_sample(ub[0], h0r, h0i, *s5w, nseq=nseq, steps=steps)
        oc = attn_sample(q[0].astype(F32), k[0].astype(F32), v[0].astype(F32), cache_kt, cache_vt, l, steps=steps)
        y, cg, cu = out_projection_sample(hs[0], oa[0], ob, oc, ffw, state_ffn_conv[l],
                                          nseq=nseq, seq_len=steps, final_norm=last)
        hs = y.reshape(1, ns, D_MODEL)
        outs[2].append(kf.reshape(nseq, steps, C_HEADS, HEAD_DIM))
        outs[3].append(vf.reshape(nseq, steps, C_HEADS, HEAD_DIM))
        outs[6].append(jnp.transpose(hr, (1, 0, 2)).reshape(nseq, B_GROUPS, SSM_STATE))
        outs[7].append(jnp.transpose(hi, (1, 0, 2)).reshape(nseq, B_GROUPS, SSM_STATE))
        outs[9].append(jnp.concatenate([cg[:, -2:], cu[:, -2:]], -1))
        outs[10].append(vn.reshape(nseq, steps, A_WIDTH))

    return (hp, hs.reshape(nseq, steps, D_MODEL)) + tuple(jnp.stack(o) for o in outs)
```

```python
import functools
import math

import jax
import jax.numpy as jnp
from jax import lax
from jax.experimental import pallas as pl
from jax.experimental.pallas import tpu as pltpu

F32 = jnp.float32
BF16 = jnp.bfloat16

D_MODEL = 1024
DEPTH = 4
PAST_LEN = 8192
HEAD_DIM = 64
A_WIDTH = 256
B_WIDTH = 384
C_WIDTH = 384
A_HEADS = 4
C_HEADS = 6
CHUNK = 128
SSM_GROUP = 16
B_GROUPS = 24
SSM_STATE = 64
DILATIONS = ((128, 1), (512, 4), (2048, 16))
MAX_WINDOW = 2048
ROPE_THETA = 500000.0
ROT_DIM = 16
D_FF = 2816
EPS = 1e-6
NEG_INF = -1e30

O_A, O_B, O_Q, O_K, O_V = 0, 2 * A_WIDTH, 896, 1280, 1664

LANES = 128
SUBLANES = 8
S5_BLOCKS = B_WIDTH // LANES
S5_HALF = 512
FF_CHUNK = 256
N_FF = D_FF // FF_CHUNK
VMEM_LIMIT = 56 * 1024 * 1024


def _dot(a, b):
    return jnp.dot(a.astype(BF16), b.astype(BF16), preferred_element_type=F32)


def _rms(x, g):
    return x * lax.rsqrt(jnp.mean(x * x, -1, keepdims=True) + EPS) * g


def _params(*sem):
    return pltpu.CompilerParams(dimension_semantics=sem, vmem_limit_bytes=VMEM_LIMIT)


def _rope_tables_kernel(c_ref, a_ref, b_ref, *, pos0):
    n = c_ref.shape[0]
    pos = (lax.broadcasted_iota(jnp.int32, (n, LANES), 0) + (pos0 + pl.program_id(0) * n)).astype(F32)
    d = lax.broadcasted_iota(jnp.int32, (n, LANES), 1) & (HEAD_DIM - 1)
    k = (d & (ROT_DIM // 2 - 1)).astype(F32)
    inv = jnp.exp(k * (-2.0 / ROT_DIM * math.log(ROPE_THETA)))
    ang = pos * inv
    cos, sin = jnp.cos(ang), jnp.sin(ang)
    c_ref[...] = jnp.where(d < ROT_DIM, cos, 1.0)
    a_ref[...] = jnp.where(d < ROT_DIM // 2, -sin, 0.0)
    b_ref[...] = jnp.where((d >= ROT_DIM // 2) & (d < ROT_DIM), sin, 0.0)


def rope_tables(n, pos0):
    sds = jax.ShapeDtypeStruct((n, LANES), F32)
    tr = min(n, 1024)
    spec = pl.BlockSpec((tr, LANES), lambda i: (i, 0))
    return pl.pallas_call(functools.partial(_rope_tables_kernel, pos0=pos0), grid=(n // tr,),
                          out_specs=(spec, spec, spec), out_shape=(sds, sds, sds),
                          compiler_params=_params("arbitrary"), name="rope_tables")()


def _s5_params_kernel(lr_ref, li_ref, ldt_ref, br_ref, bi_ref, cim_ref,
                      bbr_ref, bbi_ref, pr_ref, pi_ref, ncim_ref, *, n_pow):
    lr, li = lr_ref[...], li_ref[...]
    dt = jnp.exp(ldt_ref[...])
    mag = jnp.exp(lr * dt)
    ar, ai = mag * jnp.cos(li * dt), mag * jnp.sin(li * dt)
    nr, ni = ar - 1.0, ai
    den = lr * lr + li * li
    fr, fi = (nr * lr + ni * li) / den, (ni * lr - nr * li) / den
    br, bi = br_ref[...], bi_ref[...]
    bbr_ref[...] = fr[None] * br - fi[None] * bi
    bbi_ref[...] = fr[None] * bi + fi[None] * br
    k = (lax.broadcasted_iota(jnp.int32, (n_pow,) + lr.shape, 0) + 1).astype(F32)
    magk = jnp.exp(k * (lr * dt)[None])
    angk = k * (li * dt)[None]
    pr_ref[...] = magk * jnp.cos(angk)
    pi_ref[...] = magk * jnp.sin(angk)
    ncim_ref[...] = -cim_ref[...]


def s5_params(lam_re, lam_im, log_dt, b_re, b_im, c_im, n_pow):
    depth, g, p = lam_re.shape
    c = b_re.shape[-1]
    ldt = jnp.broadcast_to(log_dt[:, :, None], (depth, g, p))
    brt = jnp.transpose(b_re, (0, 3, 1, 2))
    bit = jnp.transpose(b_im, (0, 3, 1, 2))
    gp = pl.BlockSpec((None, g, p), lambda l: (l, 0, 0))
    cgp = pl.BlockSpec((None, c, g, p), lambda l: (l, 0, 0, 0))
    gcp = pl.BlockSpec((None, g, c, p), lambda l: (l, 0, 0, 0))
    kgp = pl.BlockSpec((None, n_pow, g, p), lambda l: (l, 0, 0, 0))
    return pl.pallas_call(
        functools.partial(_s5_params_kernel, n_pow=n_pow),
        grid=(depth,),
        in_specs=[gp, gp, gp, cgp, cgp, gcp],
        out_specs=(cgp, cgp, kgp, kgp, gcp),
        out_shape=(jax.ShapeDtypeStruct((depth, c, g, p), F32), jax.ShapeDtypeStruct((depth, c, g, p), F32),
                   jax.ShapeDtypeStruct((depth, n_pow, g, p), F32), jax.ShapeDtypeStruct((depth, n_pow, g, p), F32),
                   jax.ShapeDtypeStruct((depth, g, c, p), F32)),
        compiler_params=_params("arbitrary"),
        name="s5_params",
    )(lam_re, lam_im, ldt, brt, bit, c_im)


def _s5_lane_vec(t):
    return t.reshape(t.shape[:-2] + (S5_BLOCKS, 1, S5_HALF))


def s5_layouts(bbr, bbi, c_re, ncim, pr, pi):
    eye = jnp.eye(SUBLANES, dtype=F32)
    bb = jnp.stack([bbr, bbi], 0).reshape(2, SSM_GROUP, S5_BLOCKS, 8, SSM_STATE)
    wb = jnp.einsum('ecjgp,gh->jgcehp', bb, eye).reshape(S5_BLOCKS, LANES, 2 * S5_HALF)
    cc = jnp.stack([c_re, ncim], 0).reshape(2, S5_BLOCKS, 8, SSM_GROUP, SSM_STATE)
    wc = jnp.einsum('ejgcp,gh->jeghpc', cc, eye)
    wc = jnp.transpose(wc, (0, 1, 2, 4, 3, 5)).reshape(S5_BLOCKS, 2 * S5_HALF, LANES)
    n_pow = pr.shape[0]
    prl = jnp.transpose(pr.reshape(n_pow, S5_BLOCKS, S5_HALF), (1, 0, 2))
    pil = jnp.transpose(pi.reshape(n_pow, S5_BLOCKS, S5_HALF), (1, 0, 2))
    return wb.astype(BF16), wc.astype(BF16), prl, pil


SEG = 8
SEG_W = SEG * LANES


def _s5_seg_params_kernel(p0r_ref, p0i_ref, p1r_ref, p1i_ref, bbr_ref, bbi_ref, cre_ref, ncim_ref, cg_ref, ncg_ref,
                          k_ref, abr_ref, abi_ref, car_ref, cani_ref):
    abr = p0r_ref[...] * bbr_ref[...] - p0i_ref[...] * bbi_ref[...]
    abi = p0r_ref[...] * bbi_ref[...] + p0i_ref[...] * bbr_ref[...]
    abr_ref[...] = abr
    abi_ref[...] = abi
    car_ref[...] = cre_ref[...] * p1r_ref[...] + ncim_ref[...] * p1i_ref[...]
    cani_ref[...] = ncim_ref[...] * p1r_ref[...] - cre_ref[...] * p1i_ref[...]
    nt = (((1,), (1,)), ((), ()))
    for g in range(B_GROUPS):
        k_ref[g] = (lax.dot_general(cg_ref[g], abr[g], nt, precision=lax.Precision.HIGHEST,
                                    preferred_element_type=F32)
                    + lax.dot_general(ncg_ref[g], abi[g], nt, precision=lax.Precision.HIGHEST,
                                      preferred_element_type=F32))


def s5_seg_params(pr, pi, bbr, bbi, c_re, ncim):
    depth = pr.shape[0]
    g, c, p = B_GROUPS, SSM_GROUP, SSM_STATE
    over_c = lambda a: jnp.broadcast_to(jnp.transpose(a, (0, 2, 1, 3))[:, :, :, None, :],
                                        (depth, g, SEG, c, p)).reshape(depth, g, SEG * c, p)
    over_k = lambda a: jnp.broadcast_to(a[:, :, None], (depth, g, SEG, c, p)).reshape(depth, g, SEG * c, p)
    p0r = jnp.concatenate([jnp.ones_like(pr[:, :1]), pr[:, :SEG - 1]], 1)
    p0i = jnp.concatenate([jnp.zeros_like(pi[:, :1]), pi[:, :SEG - 1]], 1)
    bg = lambda a: jnp.transpose(a, (0, 2, 1, 3))
    big = pl.BlockSpec((None, g, SEG * c, p), lambda l: (l, 0, 0, 0))
    small = pl.BlockSpec((None, g, c, p), lambda l: (l, 0, 0, 0))
    big_sd = jax.ShapeDtypeStruct((depth, g, SEG * c, p), F32)
    return pl.pallas_call(
        _s5_seg_params_kernel,
        grid=(depth,),
        in_specs=[big] * 8 + [small, small],
        out_specs=(pl.BlockSpec((None, g, c, SEG * c), lambda l: (l, 0, 0, 0)), big, big, big, big),
        out_shape=(jax.ShapeDtypeStruct((depth, g, c, SEG * c), F32), big_sd, big_sd, big_sd, big_sd),
        compiler_params=_params("arbitrary"),
        name="s5_seg_params",
    )(over_c(p0r), over_c(p0i), over_c(pr[:, :SEG]), over_c(pi[:, :SEG]), over_k(bg(bbr)), over_k(bg(bbi)),
      over_k(c_re), over_k(ncim), c_re, ncim)


def s5_seg_layouts(kk, abr, abi, car, cani, pr, pi, d_skip):
    depth = kk.shape[0]
    c, q = SSM_GROUP, SSM_STATE

    def state_rows(re, im):
        a = jnp.stack([re, im], 1).reshape(depth, 2, S5_BLOCKS, 8, SEG, c, q)
        a = jnp.transpose(a, (0, 2, 4, 5, 1, 3, 6)).reshape(depth, S5_BLOCKS, SEG, c, 2 * S5_HALF)
        same = jnp.arange(8)[:, None] == ((jnp.arange(2 * S5_HALF) // q) % 8)[None, :]
        a = jnp.where(same[None, None, None, :, None, :], a[:, :, :, None, :, :], 0.0)
        return a.reshape(depth, S5_BLOCKS, SEG, LANES, 2 * S5_HALF)

    eb = state_rows(abr, abi)[:, :, ::-1].reshape(depth, S5_BLOCKS, SEG_W, 2 * S5_HALF)
    gm = state_rows(car, cani).reshape(depth, S5_BLOCKS, SEG_W, 2 * S5_HALF)
    kt = jnp.transpose(kk.reshape(depth, S5_BLOCKS, 8, c, SEG, c), (0, 1, 4, 5, 2, 3))
    kt = kt.reshape(depth, S5_BLOCKS, SEG, c, LANES)
    same = jnp.arange(8)[:, None] == (jnp.arange(LANES) // c)[None, :]
    kb = jnp.where(same[None, None, None, :, None, :], kt[:, :, :, None, :, :], 0.0)
    kb = kb.reshape(depth, S5_BLOCKS, SEG, LANES, LANES)
    zero = jnp.zeros_like(kb[:, :, 0])
    t = jnp.concatenate([jnp.concatenate([kb[:, :, r - rp] if r >= rp else zero for r in range(SEG)], axis=3)
                         for rp in range(SEG)], axis=2)
    a8r = pr[:, SEG - 1].reshape(depth, S5_BLOCKS, 1, S5_HALF)
    a8i = pi[:, SEG - 1].reshape(depth, S5_BLOCKS, 1, S5_HALF)
    d_t = jnp.tile(d_skip.reshape(depth, S5_BLOCKS, 1, LANES), (1, 1, SEG, 1)).reshape(depth, 1, S5_BLOCKS * SEG_W)
    return t.astype(BF16), eb.astype(BF16), gm.astype(BF16), a8r, a8i, d_t


def _rope_apply(x, c, a, b):
    return x * c + pltpu.roll(x, LANES - ROT_DIM // 2, 1) * a + pltpu.roll(x, ROT_DIM // 2, 1) * b


def _inproj_kernel(x_ref, g_ref, w_ref, ws_ref, bs_ref, rc_ref, ra_ref, rb_ref,
                   oa_ref, vn_ref, ub_ref, kf_ref, vf_ref, *rest, seq_rows, dils, seg_major):
    qkv_refs, proj_scr, perm_scr, perm2_scr = rest[:-3], rest[-3], rest[-2], rest[-1]
    tm = x_ref.shape[0]
    xn = _rms(x_ref[...], g_ref[...]).astype(BF16)
    proj_scr[...] = jnp.dot(xn, w_ref[...], preferred_element_type=F32)

    h = jax.nn.gelu(proj_scr[:, O_A:O_B])
    u, v = h[:, :A_WIDTH], h[:, A_WIDTH:]
    mu = jnp.mean(v, -1, keepdims=True)
    var = jnp.mean(jnp.square(v - mu), -1, keepdims=True)
    vn = (v - mu) * lax.rsqrt(var + EPS)
    vn_ref[...] = vn
    ri = lax.broadcasted_iota(jnp.int32, (CHUNK, CHUNK), 0)
    ci = lax.broadcasted_iota(jnp.int32, (CHUNK, CHUNK), 1)
    keep = (ci <= ri) & ((ri // seq_rows) == (ci // seq_rows))
    lane = lax.broadcasted_iota(jnp.int32, (CHUNK, LANES), 1)
    wm = [jnp.where(keep, ws_ref[hh], 0.0).astype(BF16) for hh in range(A_HEADS)]
    vnb = vn.astype(BF16)
    for c in range(tm // CHUNK):
        rows = slice(c * CHUNK, (c + 1) * CHUNK)
        for p in range(A_WIDTH // LANES):
            cols = slice(p * LANES, (p + 1) * LANES)
            vp = vnb[rows, cols]
            m0 = jnp.dot(wm[2 * p], vp, preferred_element_type=F32)
            m1 = jnp.dot(wm[2 * p + 1], vp, preferred_element_type=F32)
            mixed = jnp.where(lane < HEAD_DIM, m0, m1) + bs_ref[:, cols]
            oa_ref[rows, cols] = u[rows, cols] * mixed

    if seg_major:
        nqkv = 3 * (C_WIDTH // LANES)
        for p in range(S5_BLOCKS):
            slot = nqkv + p
            perm_scr[slot] = proj_scr[:, O_B + p * LANES:O_B + (p + 1) * LANES]
            for c4 in range(4):
                perm2_scr[slot, c4 * (tm // 4):(c4 + 1) * (tm // 4), :] = perm_scr[slot, pl.ds(c4, tm // 4, stride=4), :]
            f = SEG // 4
            for c4 in range(4):
                for s in range(f):
                    c0 = p * SEG_W + (c4 + 4 * s) * LANES
                    ub_ref[:, c0:c0 + LANES] = perm2_scr[slot, pl.ds(c4 * (tm // 4) + s, tm // SEG, stride=f), :]
    else:
        ub_ref[...] = proj_scr[:, O_B:O_Q]

    rc, ra, rb = rc_ref[...], ra_ref[...], rb_ref[...]
    def emit(val, outs, slot, p):
        outs[0][:, p * LANES:(p + 1) * LANES] = val.astype(BF16)
        if len(dils) == 1:
            return
        perm_scr[slot] = val
        src, dst, prev = perm_scr, perm2_scr, 1
        for n in range(1, len(dils)):
            dil = dils[n]
            f = dil // prev
            for c in range(prev):
                for s in range(f):
                    cls = c + prev * s
                    blk = src[slot, pl.ds(c * (tm // prev) + s, tm // dil, stride=f), :]
                    c0 = cls * C_WIDTH + p * LANES
                    outs[n][:, c0:c0 + LANES] = blk.astype(BF16)
                    if n + 1 < len(dils):
                        dst[slot, cls * (tm // dil):(cls + 1) * (tm // dil), :] = blk
            src, dst, prev = dst, src, dil

    nd = len(dils)
    assert dils[0] == 1
    q_outs, k_outs, v_outs = qkv_refs[0:nd], qkv_refs[nd:2 * nd], qkv_refs[2 * nd:3 * nd]
    for p in range(C_WIDTH // LANES):
        cols = slice(p * LANES, (p + 1) * LANES)
        q = proj_scr[:, O_Q + p * LANES:O_Q + (p + 1) * LANES]
        emit(_rope_apply(q, rc, ra, rb) * (HEAD_DIM ** -0.5), q_outs, 3 * p, p)
        k = _rope_apply(proj_scr[:, O_K + p * LANES:O_K + (p + 1) * LANES], rc, ra, rb)
        emit(k, k_outs, 3 * p + 1, p)
        kf_ref[:, cols] = k
        vv = proj_scr[:, O_V + p * LANES:O_V + (p + 1) * LANES]
        emit(vv, v_outs, 3 * p + 2, p)
        vf_ref[:, cols] = vv


def in_projection(x, g1, w_in, layer, w_s, bs_rows, rope, *, tm, seq_rows, tail_rows, dils, seg_major=False):
    bsz, s, _ = x.shape
    nt = s // tm
    tail_t = tail_rows // tm
    row = lambda w: pl.BlockSpec((None, tm, w), lambda b, t: (b, t, 0))
    tail = pl.BlockSpec((None, tm, C_WIDTH), lambda b, t: (b, jnp.maximum(t - (nt - tail_t), 0), 0))
    full = lambda shp: pl.BlockSpec(shp, lambda b, t: (0,) * len(shp))
    ropespec = pl.BlockSpec((tm, LANES), lambda b, t: (t, 0))
    sd = lambda w, dt: jax.ShapeDtypeStruct((bsz, s, w), dt)
    tail_sd = jax.ShapeDtypeStruct((bsz, tail_rows, C_WIDTH), F32)
    qkv_specs = [pl.BlockSpec((None, tm // d, d * C_WIDTH), lambda b, t: (b, t, 0)) for d in dils] * 3
    qkv_sds = [jax.ShapeDtypeStruct((bsz, s // d, d * C_WIDTH), BF16) for d in dils] * 3
    if seg_major:
        ub_spec = pl.BlockSpec((None, tm // SEG, SEG * B_WIDTH), lambda b, t: (b, t, 0))
        ub_sd = jax.ShapeDtypeStruct((bsz, s // SEG, SEG * B_WIDTH), F32)
    else:
        ub_spec, ub_sd = row(B_WIDTH), sd(B_WIDTH, F32)
    n_perm = 3 * (C_WIDTH // LANES) + S5_BLOCKS
    return pl.pallas_call(
        functools.partial(_inproj_kernel, seq_rows=seq_rows, dils=tuple(dils), seg_major=seg_major),
        grid=(bsz, nt),
        in_specs=[row(D_MODEL), _layer_spec(g1, layer), _layer_spec(w_in, layer), full(w_s.shape), full(bs_rows.shape),
                  ropespec, ropespec, ropespec],
        out_specs=(row(A_WIDTH), row(A_WIDTH), ub_spec, tail, tail, *qkv_specs),
        out_shape=(sd(A_WIDTH, F32), sd(A_WIDTH, F32), ub_sd, tail_sd, tail_sd, *qkv_sds),
        scratch_shapes=[pltpu.VMEM((tm, w_in.shape[-1]), F32),
                        pltpu.VMEM((n_perm, tm, LANES), F32),
                        pltpu.VMEM((n_perm, tm, LANES), F32)],
        compiler_params=_params("arbitrary", "arbitrary"),
        name="in_projection",
    )(x, g1, w_in, w_s, bs_rows, *rope)


def _s5_readout(hb, uperm, gs, wc_ref, d_ref, wg_ref, bg_ref):
    for j in range(S5_BLOCKS):
        cols = slice(j * LANES, (j + 1) * LANES)
        y = jnp.dot(hb[j], wc_ref[j], preferred_element_type=F32) + d_ref[:, cols] * uperm[:, cols]
        gs[:, cols] = jax.nn.gelu(y)
    g = gs[...]
    gs[...] = g * jax.nn.sigmoid(_dot(g, wg_ref[...]) + bg_ref[...])


def _s5_prompt_kernel(u_ref, wb_ref, wc_ref, pr_ref, pi_ref, d_ref, wg_ref, bg_ref,
                      o_ref, hr_ref, hi_ref,
                      upad, uperm, xs, hb, cs, carry, gs, *, seglen):
    pitch = seglen + SUBLANES

    @pl.when(pl.program_id(1) == 0)
    def _():
        carry[...] = jnp.zeros_like(carry)

    for s in range(SUBLANES):
        for p in range(S5_BLOCKS):
            upad[p, s * pitch:s * pitch + seglen, :] = u_ref[s * seglen:(s + 1) * seglen, p * LANES:(p + 1) * LANES]

    def perm(i, _):
        r = pl.multiple_of(i * SUBLANES, SUBLANES)
        for p in range(S5_BLOCKS):
            uperm[pl.ds(r, SUBLANES), p * LANES:(p + 1) * LANES] = upad[p, pl.ds(i, SUBLANES, stride=pitch), :]
        return 0
    lax.fori_loop(0, seglen, perm, 0)

    for j in range(S5_BLOCKS):
        xs[j] = _dot(uperm[:, j * LANES:(j + 1) * LANES], wb_ref[j])

    for j in range(S5_BLOCKS):
        ar = jnp.broadcast_to(pr_ref[j, 0:1, :], (SUBLANES, S5_HALF))
        ai = jnp.broadcast_to(pi_ref[j, 0:1, :], (SUBLANES, S5_HALF))

        def scan(i, hc, j=j, ar=ar, ai=ai):
            hr, hi = hc
            r = pl.multiple_of(i * SUBLANES, SUBLANES)
            nhr = ar * hr - ai * hi + xs[j, pl.ds(r, SUBLANES), 0:S5_HALF]
            nhi = ar * hi + ai * hr + xs[j, pl.ds(r, SUBLANES), S5_HALF:2 * S5_HALF]
            xs[j, pl.ds(r, SUBLANES), 0:S5_HALF] = nhr
            xs[j, pl.ds(r, SUBLANES), S5_HALF:2 * S5_HALF] = nhi
            return nhr, nhi
        z = jnp.zeros((SUBLANES, S5_HALF), F32)
        er, ei = lax.fori_loop(0, seglen, scan, (z, z))

        a64r, a64i = pr_ref[j, seglen - 1:seglen, :], pi_ref[j, seglen - 1:seglen, :]
        cr, ci = carry[j, 0, 0:1, :], carry[j, 1, 0:1, :]
        for s in range(SUBLANES):
            cs[j, 0, s:s + 1, :] = cr
            cs[j, 1, s:s + 1, :] = ci
            cr, ci = (a64r * cr - a64i * ci + er[s:s + 1, :], a64r * ci + a64i * cr + ei[s:s + 1, :])
        carry[j, 0, 0:1, :] = cr
        carry[j, 1, 0:1, :] = ci
        hr_ref[j] = cr
        hi_ref[j] = ci

        cr8, ci8 = cs[j, 0], cs[j, 1]

        def fix(i2, _, j=j, cr8=cr8, ci8=ci8):
            r = pl.multiple_of(i2 * 2 * SUBLANES, 2 * SUBLANES)
            hrs, his = [], []
            for t in range(2):
                i = i2 * 2 + t
                pr, pi = pr_ref[j, pl.ds(i, 1), :], pi_ref[j, pl.ds(i, 1), :]
                rows = pl.ds(r + t * SUBLANES, SUBLANES)
                hrs.append(xs[j, rows, 0:S5_HALF] + pr * cr8 - pi * ci8)
                his.append(xs[j, rows, S5_HALF:2 * S5_HALF] + pr * ci8 + pi * cr8)
            hb[j, pl.ds(r, 2 * SUBLANES), 0:S5_HALF] = jnp.concatenate(hrs, 0).astype(BF16)
            hb[j, pl.ds(r, 2 * SUBLANES), S5_HALF:2 * S5_HALF] = jnp.concatenate(his, 0).astype(BF16)
            return 0
        lax.fori_loop(0, seglen // 2, fix, 0, unroll=2)

    _s5_readout(hb, uperm, gs, wc_ref, d_ref, wg_ref, bg_ref)

    def unperm(i, _):
        r = pl.multiple_of(i * SUBLANES, SUBLANES)
        for p in range(S5_BLOCKS):
            upad[p, pl.ds(i, SUBLANES, stride=pitch), :] = gs[pl.ds(r, SUBLANES), p * LANES:(p + 1) * LANES]
        return 0
    lax.fori_loop(0, seglen, unperm, 0)
    for s in range(SUBLANES):
        for p in range(S5_BLOCKS):
            o_ref[s * seglen:(s + 1) * seglen, p * LANES:(p + 1) * LANES] = upad[p, s * pitch:s * pitch + seglen, :]


def s5_prompt(u, wb, wc, prl, pil, d_skip, w_glu, b_glu, *, seglen):
    bsz, s, _ = u.shape
    t = SUBLANES * seglen
    pitch = seglen + SUBLANES
    full = lambda shp: pl.BlockSpec(shp, lambda b, c: (0,) * len(shp))
    st = pl.BlockSpec((None, S5_BLOCKS, 1, S5_HALF), lambda b, c: (b, 0, 0, 0))
    rows = pl.BlockSpec((None, t, B_WIDTH), lambda b, c: (b, c, 0))
    return pl.pallas_call(
        functools.partial(_s5_prompt_kernel, seglen=seglen),
        grid=(bsz, s // t),
        in_specs=[rows, full(wb.shape), full(wc.shape), full(prl.shape), full(pil.shape),
                  full((1, B_WIDTH)), full(w_glu.shape), full((1, B_WIDTH))],
        out_specs=(rows, st, st),
        out_shape=(jax.ShapeDtypeStruct((bsz, s, B_WIDTH), F32),
                   jax.ShapeDtypeStruct((bsz, S5_BLOCKS, 1, S5_HALF), F32),
                   jax.ShapeDtypeStruct((bsz, S5_BLOCKS, 1, S5_HALF), F32)),
        scratch_shapes=[pltpu.VMEM((S5_BLOCKS, SUBLANES * pitch, LANES), F32),
                        pltpu.VMEM((t, B_WIDTH), F32),
                        pltpu.VMEM((S5_BLOCKS, t, 2 * S5_HALF), F32),
                        pltpu.VMEM((S5_BLOCKS, t, 2 * S5_HALF), BF16),
                        pltpu.VMEM((S5_BLOCKS, 2, SUBLANES, S5_HALF), F32),
                        pltpu.VMEM((S5_BLOCKS, 2, SUBLANES, S5_HALF), F32),
                        pltpu.VMEM((t, B_WIDTH), F32)],
        compiler_params=_params("arbitrary", "arbitrary"),
        name="s5_prompt",
    )(u, wb, wc, prl, pil, d_skip, w_glu, b_glu)


def _s5_seg_kernel(u_ref, t_ref, eb_ref, gm_ref, a8r_ref, a8i_ref, d_ref, wg_ref, bg_ref,
                   o_ref, hr_ref, hi_ref, e_scr, cs_scr, gs_scr, carry):
    nseg = u_ref.shape[0]

    @pl.when(pl.program_id(1) == 0)
    def _():
        carry[...] = jnp.zeros_like(carry)

    for p in range(S5_BLOCKS):
        ubf = u_ref[:, p * SEG_W:(p + 1) * SEG_W].astype(BF16)
        e_scr[p] = jnp.dot(ubf, eb_ref[p], preferred_element_type=F32)

    sub = lax.broadcasted_iota(jnp.int32, (SUBLANES, S5_HALF), 0)
    a8 = [(a8r_ref[p], a8i_ref[p]) for p in range(S5_BLOCKS)]

    def block(k, c):
        r0 = pl.multiple_of(k * SUBLANES, SUBLANES)
        out = []
        for p in range(S5_BLOCKS):
            cr, ci = c[2 * p], c[2 * p + 1]
            er = e_scr[p, pl.ds(r0, SUBLANES), 0:S5_HALF]
            ei = e_scr[p, pl.ds(r0, SUBLANES), S5_HALF:2 * S5_HALF]
            br = jnp.zeros((SUBLANES, S5_HALF), F32)
            bi = jnp.zeros((SUBLANES, S5_HALF), F32)
            for j in range(SUBLANES):
                br = jnp.where(sub == j, cr, br)
                bi = jnp.where(sub == j, ci, bi)
                cr, ci = (a8[p][0] * cr - a8[p][1] * ci + er[j:j + 1, :],
                          a8[p][0] * ci + a8[p][1] * cr + ei[j:j + 1, :])
            cs_scr[p, pl.ds(r0, SUBLANES), 0:S5_HALF] = br
            cs_scr[p, pl.ds(r0, SUBLANES), S5_HALF:2 * S5_HALF] = bi
            out += [cr, ci]
        return tuple(out)

    c0 = tuple(carry[p, e, 0:1, :] for p in range(S5_BLOCKS) for e in range(2))
    cf = lax.fori_loop(0, nseg // SUBLANES, block, c0)
    for p in range(S5_BLOCKS):
        carry[p, 0, 0:1, :] = cf[2 * p]
        carry[p, 1, 0:1, :] = cf[2 * p + 1]
        hr_ref[p] = cf[2 * p]
        hi_ref[p] = cf[2 * p + 1]

    for p in range(S5_BLOCKS):
        cols = slice(p * SEG_W, (p + 1) * SEG_W)
        u = u_ref[:, cols]
        y = (jnp.dot(u.astype(BF16), t_ref[p], preferred_element_type=F32)
             + lax.dot_general(cs_scr[p].astype(BF16), gm_ref[p], (((1,), (1,)), ((), ())),
                               preferred_element_type=F32) + d_ref[:, cols] * u)
        gs_scr[:, cols] = jax.nn.gelu(y)
    for r in range(SEG):
        tok = [slice(p * SEG_W + r * LANES, p * SEG_W + (r + 1) * LANES) for p in range(S5_BLOCKS)]
        g = jnp.concatenate([gs_scr[:, c] for c in tok], axis=1)
        out = g * jax.nn.sigmoid(_dot(g, wg_ref[...]) + bg_ref[...])
        for p in range(S5_BLOCKS):
            o_ref[:, tok[p]] = out[:, p * LANES:(p + 1) * LANES]


def s5_seg(u, t, eb, gm, a8r, a8i, d_t, w_glu, b_glu, layer, *, rows):
    bsz, nseg, width = u.shape
    resident = lambda a: _layer_spec(a, layer, pipeline_mode=pl.Buffered(1))
    blk = pl.BlockSpec((None, rows, width), lambda b, c: (b, c, 0))
    st = pl.BlockSpec((None, S5_BLOCKS, 1, S5_HALF), lambda b, c: (b, 0, 0, 0))
    st_sd = jax.ShapeDtypeStruct((bsz, S5_BLOCKS, 1, S5_HALF), F32)
    return pl.pallas_call(
        _s5_seg_kernel,
        grid=(bsz, nseg // rows),
        in_specs=[blk] + [resident(a) for a in (t, eb, gm, a8r, a8i, d_t, w_glu, b_glu)],
        out_specs=(blk, st, st),
        out_shape=(jax.ShapeDtypeStruct(u.shape, F32), st_sd, st_sd),
        scratch_shapes=[pltpu.VMEM((S5_BLOCKS, rows, 2 * S5_HALF), F32),
                        pltpu.VMEM((S5_BLOCKS, rows, 2 * S5_HALF), F32),
                        pltpu.VMEM((rows, width), F32),
                        pltpu.VMEM((S5_BLOCKS, 2, SUBLANES, S5_HALF), F32)],
        compiler_params=_params("arbitrary", "arbitrary"),
        name="s5_seg",
    )(u, t, eb, gm, a8r, a8i, d_t, w_glu, b_glu)


def _s5_sample_kernel(u_ref, h0r_ref, h0i_ref, wb_ref, wc_ref, pr_ref, pi_ref, d_ref, wg_ref, bg_ref,
                      o_ref, hr_ref, hi_ref,
                      upad, uperm, xs, hb, gs, *, nseq, steps):
    nblk = nseq // SUBLANES
    for p in range(S5_BLOCKS):
        upad[p] = u_ref[:, p * LANES:(p + 1) * LANES]
    for st in range(steps):
        for bb in range(nblk):
            r = (st * nblk + bb) * SUBLANES
            for p in range(S5_BLOCKS):
                uperm[r:r + SUBLANES, p * LANES:(p + 1) * LANES] = (
                    upad[p, pl.ds(bb * SUBLANES * steps + st, SUBLANES, stride=steps), :])
    for j in range(S5_BLOCKS):
        xs[j] = _dot(uperm[:, j * LANES:(j + 1) * LANES], wb_ref[j])
    for j in range(S5_BLOCKS):
        ar = jnp.broadcast_to(pr_ref[j, 0:1, :], (SUBLANES, S5_HALF))
        ai = jnp.broadcast_to(pi_ref[j, 0:1, :], (SUBLANES, S5_HALF))
        for bb in range(nblk):
            seqs = slice(bb * SUBLANES, (bb + 1) * SUBLANES)
            hr, hi = h0r_ref[j, seqs, :], h0i_ref[j, seqs, :]
            for st in range(steps):
                r = (st * nblk + bb) * SUBLANES
                hr, hi = (ar * hr - ai * hi + xs[j, r:r + SUBLANES, 0:S5_HALF],
                          ar * hi + ai * hr + xs[j, r:r + SUBLANES, S5_HALF:2 * S5_HALF])
                xs[j, r:r + SUBLANES, 0:S5_HALF] = hr
                xs[j, r:r + SUBLANES, S5_HALF:2 * S5_HALF] = hi
            hr_ref[j, seqs, :] = hr
            hi_ref[j, seqs, :] = hi
        hb[j] = xs[j].astype(BF16)
    _s5_readout(hb, uperm, gs, wc_ref, d_ref, wg_ref, bg_ref)
    for st in range(steps):
        for bb in range(nblk):
            r = (st * nblk + bb) * SUBLANES
            for p in range(S5_BLOCKS):
                upad[p, pl.ds(bb * SUBLANES * steps + st, SUBLANES, stride=steps), :] = (
                    gs[r:r + SUBLANES, p * LANES:(p + 1) * LANES])
    for p in range(S5_BLOCKS):
        o_ref[:, p * LANES:(p + 1) * LANES] = upad[p]


def s5_sample(u, h0r, h0i, wb, wc, prl, pil, d_skip, w_glu, b_glu, *, nseq, steps):
    n = nseq * steps
    st_sd = jax.ShapeDtypeStruct((S5_BLOCKS, nseq, S5_HALF), F32)
    return pl.pallas_call(
        functools.partial(_s5_sample_kernel, nseq=nseq, steps=steps),
        out_shape=(jax.ShapeDtypeStruct((n, B_WIDTH), F32), st_sd, st_sd),
        scratch_shapes=[pltpu.VMEM((S5_BLOCKS, n, LANES), F32),
                        pltpu.VMEM((n, B_WIDTH), F32),
                        pltpu.VMEM((S5_BLOCKS, n, 2 * S5_HALF), F32),
                        pltpu.VMEM((S5_BLOCKS, n, 2 * S5_HALF), BF16),
                        pltpu.VMEM((n, B_WIDTH), F32)],
        compiler_params=pltpu.CompilerParams(vmem_limit_bytes=VMEM_LIMIT),
        name="s5_sample",
    )(u, h0r, h0i, wb, wc, prl, pil, d_skip, w_glu, b_glu)


def _attn_prompt_kernel(q_ref, kp_ref, kc_ref, vp_ref, vc_ref, o_ref, st_ref, kx, vx):
    tq = q_ref.shape[0]
    t = pl.program_id(2)
    kx[0:CHUNK, :] = kp_ref[...]
    kx[CHUNK:CHUNK + tq, :] = kc_ref[...]
    vx[0:CHUNK, :] = vp_ref[...]
    vx[CHUNK:CHUNK + tq, :] = vc_ref[...]
    r = lax.broadcasted_iota(jnp.int32, (CHUNK, 2 * CHUNK), 0)
    c = lax.broadcasted_iota(jnp.int32, (CHUNK, 2 * CHUNK), 1)
    band = (c >= r) & (c <= r + CHUNK)
    lane = lax.broadcasted_iota(jnp.int32, (CHUNK, LANES), 1)
    lo_half = lane < HEAD_DIM
    for u in range(tq // CHUNK):
        rows = slice(u * CHUNK, (u + 1) * CHUNK)
        keys = slice(u * CHUNK, (u + 2) * CHUNK)
        if u == 0:
            valid = band & (c >= jnp.where(t == 0, CHUNK, 0))
        else:
            valid = band
        bias = jnp.where(valid, 0.0, NEG_INF)
        st = jnp.zeros((CHUNK, LANES), F32)
        for p in range(C_WIDTH // LANES):
            cols = slice(p * LANES, (p + 1) * LANES)
            qp = q_ref[rows, cols]
            kpair = kx[keys, cols]
            vpair = vx[keys, cols]
            outs = []
            for hh in range(2):
                qm = jnp.where(lo_half if hh == 0 else ~lo_half, qp, jnp.zeros_like(qp))
                sc = lax.dot_general(qm, kpair, (((1,), (1,)), ((), ())), preferred_element_type=F32) + bias
                m = jnp.max(sc, -1, keepdims=True)
                pe = jnp.exp(sc - m)
                l = jnp.sum(pe, -1, keepdims=True)
                outs.append(jnp.dot(pe.astype(BF16), vpair, preferred_element_type=F32) / l)
                st = jnp.where(lane == 2 * p + hh, m + jnp.log(l), st)
            o_ref[rows, cols] = jnp.where(lo_half, outs[0], outs[1])
        st_ref[rows, :] = st


def attn_prompt(q, k, v, dil, *, tq):
    bsz, ln, _ = q.shape
    tq = min(tq, ln)
    cur = pl.BlockSpec((None, tq, C_WIDTH), lambda b, r, t: (b, t, r))
    prev = pl.BlockSpec((None, CHUNK, C_WIDTH), lambda b, r, t: (b, jnp.maximum(t * (tq // CHUNK) - 1, 0), r))
    return pl.pallas_call(
        _attn_prompt_kernel,
        grid=(bsz, dil, ln // tq),
        in_specs=[cur, prev, cur, prev, cur],
        out_specs=(cur, pl.BlockSpec((None, tq, LANES), lambda b, r, t: (b, t, r))),
        out_shape=(jax.ShapeDtypeStruct((bsz, ln, dil * C_WIDTH), F32),
                   jax.ShapeDtypeStruct((bsz, ln, dil * LANES), F32)),
        scratch_shapes=[pltpu.VMEM((CHUNK + tq, C_WIDTH), BF16), pltpu.VMEM((CHUNK + tq, C_WIDTH), BF16)],
        compiler_params=_params("arbitrary", "arbitrary", "arbitrary"),
        name=f"attn_prompt_d{dil}",
    )(q, k, k, v, v)


def _attn_sample_kernel(q_ref, kn_ref, vn_ref, ck_ref, cv_ref, o_ref, *, steps):
    cw = ck_ref.shape[-1]

    def mult(delta):
        cnt = jnp.zeros(delta.shape, F32)
        for window, dil in DILATIONS:
            cnt = cnt + jnp.where((delta >= 0) & (delta <= window) & ((delta & (dil - 1)) == 0), 1.0, 0.0)
        return cnt

    cnt_c = mult(cw + lax.broadcasted_iota(jnp.int32, (steps, cw), 0)
                 - lax.broadcasted_iota(jnp.int32, (steps, cw), 1))
    cnt_n = mult(lax.broadcasted_iota(jnp.int32, (steps, CHUNK), 0)
                 - lax.broadcasted_iota(jnp.int32, (steps, CHUNK), 1))
    live_c, live_n = cnt_c > 0, cnt_n > 0
    zpad = jnp.zeros((CHUNK - steps, C_WIDTH), F32)
    kn = jnp.concatenate([kn_ref[...], zpad], 0).astype(BF16)
    vn = jnp.concatenate([vn_ref[...], zpad], 0).astype(BF16)
    q = q_ref[...].astype(BF16)
    nt_dims = (((1,), (1,)), ((), ()))
    for hh in range(C_HEADS):
        cols = slice(hh * HEAD_DIM, (hh + 1) * HEAD_DIM)
        qh = q[:, cols]
        sc = jnp.dot(qh, ck_ref[hh].astype(BF16), preferred_element_type=F32)
        sn = lax.dot_general(qh, kn[:, cols], nt_dims, preferred_element_type=F32)
        m = jnp.maximum(jnp.max(jnp.where(live_c, sc, NEG_INF), -1, keepdims=True),
                        jnp.max(jnp.where(live_n, sn, NEG_INF), -1, keepdims=True))
        ec = cnt_c * jnp.exp(jnp.where(live_c, sc - m, NEG_INF))
        en = cnt_n * jnp.exp(jnp.where(live_n, sn - m, NEG_INF))
        l = jnp.sum(ec, -1, keepdims=True) + jnp.sum(en, -1, keepdims=True)
        acc = (lax.dot_general(ec.astype(BF16), cv_ref[hh].astype(BF16), nt_dims, preferred_element_type=F32)
               + jnp.dot(en.astype(BF16), vn[:, cols], preferred_element_type=F32))
        o_ref[:, cols] = acc / l


def attn_sample(q, k_new, v_new, cache_kt, cache_vt, layer, *, steps):
    _, nseq, _, _, cw = cache_kt.shape
    rows = pl.BlockSpec((steps, C_WIDTH), lambda b: (b, 0))
    cache = pl.BlockSpec((None, None, C_HEADS, HEAD_DIM, cw), lambda b: (layer, b, 0, 0, 0))
    return pl.pallas_call(
        functools.partial(_attn_sample_kernel, steps=steps),
        grid=(nseq,),
        in_specs=[rows, rows, rows, cache, cache],
        out_specs=rows,
        out_shape=jax.ShapeDtypeStruct((nseq * steps, C_WIDTH), F32),
        compiler_params=_params("arbitrary"),
        name="attn_sample",
    )(q, k_new, v_new, cache_kt, cache_vt)


FF_ROWS = 128
FF_PIECES = 4


def _residual_and_norm(x_ref, oa, ob, oc, wo_ref, g2_ref, y_ref, xn_scr):
    mix = (_dot(oa, wo_ref[0:A_WIDTH, :]) + _dot(ob, wo_ref[A_WIDTH:A_WIDTH + B_WIDTH, :])
           + _dot(oc, wo_ref[A_WIDTH + B_WIDTH:, :]))
    x1 = x_ref[...] + mix
    y_ref[...] = x1
    xn_scr[...] = _rms(x1, g2_ref[...]).astype(BF16)


def _outproj_prompt_kernel(x_ref, oa_ref, ob_ref, *rest, tiles_per_seq, dils, final_norm):
    nb = len(dils)
    o_refs, st_refs = rest[:nb], rest[nb:2 * nb]
    (e_ref, wo_ref, g2_ref, wup_ref, cw_ref, cb_ref, wd_ref, fg_ref,
     y_ref, cg_ref, cu_ref, xn_scr, carry_scr, unp_scr, hs0, hs1, act0, act1) = rest[2 * nb:]
    hs_scr, act_scr = (hs0, hs1), (act0, act1)
    i = pl.program_id(0)
    tm = x_ref.shape[0]
    nslab = C_WIDTH // LANES

    outs, sts = [], []
    slot = 0
    for dil, o_ref, s_ref in zip(dils, o_refs, st_refs):
        if dil == 1:
            outs.append(o_ref[...])
            sts.append(s_ref[...])
            continue
        for r in range(dil):
            rows = pl.ds(r, tm // dil, stride=dil)
            for p in range(nslab):
                c0 = r * C_WIDTH + p * LANES
                unp_scr[slot + p, rows, :] = o_ref[:, c0:c0 + LANES]
            unp_scr[slot + nslab, rows, :] = s_ref[:, r * LANES:(r + 1) * LANES]
        outs.append(jnp.concatenate([unp_scr[slot + p] for p in range(nslab)], axis=1))
        sts.append(unp_scr[slot + nslab])
        slot += nslab + 1
    mx = functools.reduce(jnp.maximum, sts)
    ws = [jnp.exp(s - mx) for s in sts]
    tot = functools.reduce(lambda a, b: a + b, ws)
    oc = jnp.zeros((tm, C_WIDTH), F32)
    for w, o in zip(ws, outs):
        alpha = w / tot
        hi = alpha.astype(BF16)
        lo = (alpha - hi.astype(F32)).astype(BF16)
        wide = jnp.dot(jnp.concatenate([hi, lo], axis=1), e_ref[...], preferred_element_type=F32)
        oc = oc + wide * o
    for p in range(S5_BLOCKS):
        for r in range(SEG):
            c0 = p * SEG_W + r * LANES
            unp_scr[slot + p, pl.ds(r, tm // SEG, stride=SEG), :] = ob_ref[:, c0:c0 + LANES]
    ob = jnp.concatenate([unp_scr[slot + p] for p in range(S5_BLOCKS)], axis=1)
    _residual_and_norm(x_ref, oa_ref[...], ob, oc, wo_ref, g2_ref, y_ref, xn_scr)

    @pl.when((i % tiles_per_seq) == 0)
    def _():
        carry_scr[...] = jnp.zeros_like(carry_scr)

    nhs = 2 * FF_CHUNK // LANES
    gate = FF_CHUNK // LANES
    piece = tm // FF_PIECES

    def up(jj, slot, pc):
        xr = xn_scr[pc * piece:(pc + 1) * piece, :]
        rows = slice(SUBLANES + pc * piece, SUBLANES + (pc + 1) * piece)
        for half in range(2):
            h = jnp.dot(xr, wup_ref[half * N_FF + jj], preferred_element_type=F32)
            for k in range(gate):
                hs_scr[slot][2 * (half * gate + k), rows, :] = h[:, k * LANES:(k + 1) * LANES]

    def down(jj, slot, pc):
        rows = slice(pc * piece, (pc + 1) * piece)
        y_ref[rows, :] += jnp.dot(act_scr[slot][rows, :], wd_ref[jj], preferred_element_type=F32)

    def conv_taps(jj, slot):
        for k in range(nhs):
            hs_scr[slot][2 * k, 0:SUBLANES, :] = carry_scr[jj, k]
        cwj = jnp.concatenate([cw_ref[jj], cw_ref[N_FF + jj]], axis=1)
        cbj = jnp.concatenate([cb_ref[jj], cb_ref[N_FF + jj]], axis=1)
        slabs = lambda row: jnp.stack([row[:, k * LANES:(k + 1) * LANES] for k in range(nhs)])
        return [slabs(cwj[t:t + 1]) for t in range(3)], slabs(cbj)

    def conv(jj, slot, pc, taps, bias):
        every_other = pl.ds(0, nhs, stride=2)
        for r in range(pc * piece // FF_ROWS, (pc + 1) * piece // FF_ROWS):
            shifted = lambda back: hs_scr[slot][every_other, pl.ds(SUBLANES - back + r * FF_ROWS, FF_ROWS), :]
            hc = bias + (shifted(2) * taps[0] + shifted(1) * taps[1] + shifted(0) * taps[2])
            act = jax.nn.gelu(hc[:gate]) * hc[gate:]
            for k in range(gate):
                act_scr[slot][r * FF_ROWS:(r + 1) * FF_ROWS, k * LANES:(k + 1) * LANES] = act[k].astype(BF16)

    def conv_tail(jj, slot):
        for k in range(nhs):
            last = hs_scr[slot][2 * k, tm:tm + SUBLANES, :]
            carry_scr[jj, k] = last
            if k < gate:
                cg_ref[0, jj, :, k * LANES:(k + 1) * LANES] = last
            else:
                cu_ref[0, jj, :, (k - gate) * LANES:(k - gate + 1) * LANES] = last

    def stage(j, slot, do_down, do_up):
        taps, bias = conv_taps(j, slot)
        for pc in range(FF_PIECES):
            conv(j, slot, pc, taps, bias)
            if do_down:
                down(j - 1, 1 - slot, pc)
            if do_up:
                up(j + 1, 1 - slot, pc)
        conv_tail(j, slot)

    for pc in range(FF_PIECES):
        up(0, 0, pc)
    stage(0, 0, False, True)

    def steady(n, _):
        stage(2 * n + 1, 1, True, True)
        stage(2 * n + 2, 0, True, True)
        return 0
    n_pairs = (N_FF - 2) // 2
    lax.fori_loop(0, n_pairs, steady, 0)
    for j in range(2 * n_pairs + 1, N_FF):
        stage(j, j % 2, True, j + 1 < N_FF)
    for pc in range(FF_PIECES):
        down(N_FF - 1, (N_FF - 1) % 2, pc)

    if final_norm:
        y_ref[...] = _rms(y_ref[...], fg_ref[...])


def _outproj_sample_kernel(x_ref, oa_ref, ob_ref, oc_ref, wo_ref, g2_ref, wug_ref, wuu_ref, cwg_ref, cwu_ref,
                           cbg_ref, cbu_ref, wd_ref, fg_ref, bufg_ref, bufu_ref,
                           y_ref, cg_ref, cu_ref, xn_scr, hs_scr, *, nseq, seq_len, final_norm):
    j = pl.program_id(1)
    tm = x_ref.shape[0]

    @pl.when(j == 0)
    def _():
        _residual_and_norm(x_ref, oa_ref[...], ob_ref[...], oc_ref[...], wo_ref, g2_ref, y_ref, xn_scr)

    xn = xn_scr[...]
    hg = jnp.dot(xn, wug_ref[...], preferred_element_type=F32)
    hu = jnp.dot(xn, wuu_ref[...], preferred_element_type=F32)
    lo, hi_ = SUBLANES, SUBLANES + seq_len
    hs_scr[:, lo:hi_, 0:FF_CHUNK] = hg.reshape(nseq, seq_len, FF_CHUNK)
    hs_scr[:, lo:hi_, FF_CHUNK:] = hu.reshape(nseq, seq_len, FF_CHUNK)
    hs_scr[:, lo - 2:lo, 0:FF_CHUNK] = bufg_ref[...]
    hs_scr[:, lo - 2:lo, FF_CHUNK:] = bufu_ref[...]
    cg_ref[...] = hs_scr[:, hi_ - SUBLANES:hi_, 0:FF_CHUNK]
    cu_ref[...] = hs_scr[:, hi_ - SUBLANES:hi_, FF_CHUNK:]

    cw = jnp.concatenate([cwg_ref[...], cwu_ref[...]], axis=1)
    cb = jnp.concatenate([cbg_ref[...], cbu_ref[...]], axis=1)
    hc = cb + (hs_scr[:, lo - 2:hi_ - 2, :] * cw[0:1] + hs_scr[:, lo - 1:hi_ - 1, :] * cw[1:2]
               + hs_scr[:, lo:hi_, :] * cw[2:3])
    hc = hc.reshape(tm, 2 * FF_CHUNK)
    act = jax.nn.gelu(hc[:, :FF_CHUNK]) * hc[:, FF_CHUNK:]
    y_ref[...] += _dot(act, wd_ref[...])

    if final_norm:
        @pl.when(j == pl.num_programs(1) - 1)
        def _():
            y_ref[...] = _rms(y_ref[...], fg_ref[...])


def ffn_layouts(w_up, conv_w, conv_b, w_down):
    depth = w_up.shape[0]
    chunks = lambda a: jnp.transpose(a.reshape(depth, a.shape[1], 2 * N_FF, FF_CHUNK), (0, 2, 1, 3))
    return (chunks(w_up), chunks(conv_w), chunks(conv_b.reshape(depth, 1, -1)),
            w_down.reshape(depth, N_FF, FF_CHUNK, D_MODEL))


def _layer_spec(a, layer, **kw):
    return pl.BlockSpec((None,) + a.shape[1:], lambda *_: (layer,) + (0,) * (a.ndim - 1), **kw)


def out_projection_prompt(x, oa, ob, o_list, st_list, dils, expand, ffw, layer, *, tm, tiles_per_seq, final_norm):
    n = x.shape[0]
    nt = n // tm
    row = lambda w: pl.BlockSpec((tm, w), lambda i: (i, 0))
    perm = lambda w: [pl.BlockSpec((tm // d, d * w), lambda i: (i, 0)) for d in dils]
    resident = lambda a: pl.BlockSpec(a.shape, lambda i: (0,) * a.ndim, pipeline_mode=pl.Buffered(1))
    conv_spec = pl.BlockSpec((1, N_FF, SUBLANES, FF_CHUNK), lambda i: (i, 0, 0, 0))
    conv_sd = jax.ShapeDtypeStruct((nt, N_FF, SUBLANES, FF_CHUNK), F32)
    n_unp = sum(C_WIDTH // LANES + 1 for d in dils if d > 1) + S5_BLOCKS
    seg_rows = pl.BlockSpec((tm // SEG, SEG * B_WIDTH), lambda i: (i, 0))
    nhs = 2 * FF_CHUNK // LANES
    return pl.pallas_call(
        functools.partial(_outproj_prompt_kernel, tiles_per_seq=tiles_per_seq, dils=tuple(dils),
                          final_norm=final_norm),
        grid=(nt,),
        in_specs=[row(D_MODEL), row(A_WIDTH), seg_rows] + perm(C_WIDTH) + perm(LANES)
                 + [resident(expand)] + [_layer_spec(a, layer, pipeline_mode=pl.Buffered(1)) for a in ffw],
        out_specs=(row(D_MODEL), conv_spec, conv_spec),
        out_shape=(jax.ShapeDtypeStruct((n, D_MODEL), F32), conv_sd, conv_sd),
        scratch_shapes=[pltpu.VMEM((tm, D_MODEL), BF16),
                        pltpu.VMEM((N_FF, nhs, SUBLANES, LANES), F32),
                        pltpu.VMEM((n_unp, tm, LANES), F32),
                        pltpu.VMEM((2 * nhs, SUBLANES + tm, LANES), F32),
                        pltpu.VMEM((2 * nhs, SUBLANES + tm, LANES), F32),
                        pltpu.VMEM((tm, FF_CHUNK), BF16),
                        pltpu.VMEM((tm, FF_CHUNK), BF16)],
        compiler_params=_params("arbitrary"),
        name="out_projection_prompt",
    )(x, oa, ob, *o_list, *st_list, expand, *ffw)


def out_projection_sample(x, oa, ob, oc, ffw, layer, bufs, *, nseq, seq_len, final_norm):
    n = x.shape[0]
    row = lambda w: pl.BlockSpec((n, w), lambda i, j: (0, 0))
    full = lambda a: _layer_spec(a, layer)
    w_out, g2, w_up, conv_w, conv_b, w_down, fg = ffw
    chunk = lambda a, off: pl.BlockSpec((None, None) + a.shape[2:], lambda i, j: (layer, off + j, 0, 0))
    wspecs = [full(w_out), full(g2), chunk(w_up, 0), chunk(w_up, N_FF), chunk(conv_w, 0), chunk(conv_w, N_FF),
              chunk(conv_b, 0), chunk(conv_b, N_FF), chunk(w_down, 0), full(fg)]
    wargs = [w_out, g2, w_up, w_up, conv_w, conv_w, conv_b, conv_b, w_down, fg]
    conv_spec = pl.BlockSpec((nseq, SUBLANES, FF_CHUNK), lambda i, j: (0, 0, j))
    return pl.pallas_call(
        functools.partial(_outproj_sample_kernel, nseq=nseq, seq_len=seq_len, final_norm=final_norm),
        grid=(1, N_FF),
        in_specs=[row(D_MODEL), row(A_WIDTH), row(B_WIDTH), row(C_WIDTH)] + wspecs
                 + [pl.BlockSpec((nseq, 2, FF_CHUNK), lambda i, j: (0, 0, j)),
                    pl.BlockSpec((nseq, 2, FF_CHUNK), lambda i, j: (0, 0, N_FF + j))],
        out_specs=(row(D_MODEL), conv_spec, conv_spec),
        out_shape=(jax.ShapeDtypeStruct((n, D_MODEL), F32),
                   jax.ShapeDtypeStruct((nseq, SUBLANES, D_FF), F32),
                   jax.ShapeDtypeStruct((nseq, SUBLANES, D_FF), F32)),
        scratch_shapes=[pltpu.VMEM((n, D_MODEL), BF16),
                        pltpu.VMEM((nseq, SUBLANES + seq_len, 2 * FF_CHUNK), F32)],
        compiler_params=_params("arbitrary", "arbitrary"),
        name="out_projection_sample",
    )(x, oa, ob, oc, *wargs, bufs, bufs)


PROMPT_TM = 512
S5_ROWS = 256
ATTN_TQ = 512


def kernel(x_prompt, x_sample, cache_c_k, cache_c_v, state_ssm_re, state_ssm_im, state_ffn_conv, norm1_g, w_in, w_s, b_s, ssm_lam_re, ssm_lam_im, ssm_log_dt, ssm_b_re, ssm_b_im, ssm_c_re, ssm_c_im, ssm_d, w_glu, b_glu, w_out, norm2_g, w_up, conv_w, conv_b, w_down, final_g):
    bp, sp, _ = x_prompt.shape
    nseq, steps, _ = x_sample.shape
    ns = nseq * steps
    cw_p = min(MAX_WINDOW, sp)
    cw_s = cache_c_k.shape[2]

    rope_p = rope_tables(sp, 0)
    rope_s = tuple(jnp.tile(t, (nseq, 1)) for t in rope_tables(steps, PAST_LEN))
    bbr, bbi, pr, pi, ncim = s5_params(ssm_lam_re, ssm_lam_im, ssm_log_dt, ssm_b_re, ssm_b_im, ssm_c_im, SEG)
    lag_k, abr, abi, car, cani = s5_seg_params(pr, pi, bbr, bbi, ssm_c_re, ncim)
    segw = s5_seg_layouts(lag_k, abr, abi, car, cani, pr, pi, ssm_d)

    w_in_b, w_out_b, w_up_b, w_down_b, w_glu_b = (w.astype(BF16) for w in (w_in, w_out, w_up, w_down, w_glu))
    expand = (jnp.arange(2 * LANES)[:, None] % LANES == jnp.arange(C_WIDTH)[None, :] // HEAD_DIM).astype(BF16)
    bs_p = jnp.repeat(jnp.transpose(b_s, (0, 2, 1)), HEAD_DIM, axis=2)
    bs_s = jnp.tile(bs_p[:, :steps], (1, CHUNK // steps, 1))
    ws_s = jnp.tile(w_s[:, :, :steps, :steps], (1, 1, CHUNK // steps, CHUNK // steps))
    row1 = lambda a: a.reshape(1, -1)
    rows1 = lambda a: a.reshape(a.shape[0], 1, a.shape[1])
    ffw = (w_out_b, rows1(norm2_g), *ffn_layouts(w_up_b, conv_w, conv_b, w_down_b),
           jnp.broadcast_to(final_g.reshape(1, 1, -1), (DEPTH, 1, D_MODEL)))
    segw = segw + (w_glu_b, rows1(b_glu))
    g1 = rows1(norm1_g)

    cache_kt = jnp.transpose(cache_c_k, (0, 1, 3, 4, 2))
    cache_vt = jnp.transpose(cache_c_v, (0, 1, 3, 4, 2))
    dils = tuple(d for _, d in DILATIONS)

    hp = x_prompt
    hs = x_sample.reshape(1, ns, D_MODEL)
    outs = [[] for _ in range(11)]
    for l in range(DEPTH):
        last = l == DEPTH - 1
        wb, wc, prl, pil = s5_layouts(bbr[l], bbi[l], ssm_c_re[l], ncim[l], pr[l], pi[l])
        s5w = (wb, wc, prl, pil, row1(ssm_d[l]), w_glu_b[l], row1(b_glu[l]))

        oa, _, ub, kf, vf, *qkv = in_projection(hp, g1, w_in_b, l, w_s[l], bs_p[l], rope_p,
                                                tm=PROMPT_TM, seq_rows=CHUNK, tail_rows=cw_p, dils=dils,
                                                seg_major=True)
        nd = len(dils)
        ob, hr, hi = s5_seg(ub, *segw, l, rows=S5_ROWS)
        o_list, st_list = zip(*[attn_prompt(qkv[n], qkv[nd + n], qkv[2 * nd + n], dil, tq=ATTN_TQ)
                                for n, dil in enumerate(dils)])
        flat = lambda a: a.reshape(a.shape[0] * a.shape[1], a.shape[2])
        y, cg, cu = out_projection_prompt(flat(hp), flat(oa), flat(ob), [flat(o) for o in o_list],
                                          [flat(s) for s in st_list], dils, expand, ffw, l,
                                          tm=PROMPT_TM, tiles_per_seq=sp // PROMPT_TM, final_norm=last)
        hp = y.reshape(bp, sp, D_MODEL)
        outs[0].append(kf.reshape(bp, cw_p, C_HEADS, HEAD_DIM))
        outs[1].append(vf.reshape(bp, cw_p, C_HEADS, HEAD_DIM))
        outs[4].append(hr.reshape(bp, B_GROUPS, SSM_STATE))
        outs[5].append(hi.reshape(bp, B_GROUPS, SSM_STATE))
        seq_end = lambda a: jnp.transpose(a.reshape(bp, sp // PROMPT_TM, N_FF, SUBLANES, FF_CHUNK)[:, -1, :, -2:],
                                          (0, 2, 1, 3)).reshape(bp, 2, D_FF)
        outs[8].append(jnp.concatenate([seq_end(cg), seq_end(cu)], -1))

        oa, vn, ub, kf, vf, q, k, v = in_projection(hs, g1, w_in_b, l, ws_s[l], bs_s[l], rope_s,
                                                    tm=ns, seq_rows=steps, tail_rows=ns, dils=(1,))
        h0r = jnp.transpose(state_ssm_re[l].reshape(nseq, S5_BLOCKS, S5_HALF), (1, 0, 2))
        h0i = jnp.transpose(state_ssm_im[l].reshape(nseq, S5_BLOCKS, S5_HALF), (1, 0, 2))
        ob, hr, hi = s5_sample(ub[0], h0r, h0i, *s5w, nseq=nseq, steps=steps)
        oc = attn_sample(q[0].astype(F32), k[0].astype(F32), v[0].astype(F32), cache_kt, cache_vt, l, steps=steps)
        y, cg, cu = out_projection_sample(hs[0], oa[0], ob, oc, ffw, l, state_ffn_conv[l],
                                          nseq=nseq, seq_len=steps, final_norm=last)
        hs = y.reshape(1, ns, D_MODEL)
        outs[2].append(kf.reshape(nseq, steps, C_HEADS, HEAD_DIM))
        outs[3].append(vf.reshape(nseq, steps, C_HEADS, HEAD_DIM))
        outs[6].append(jnp.transpose(hr, (1, 0, 2)).reshape(nseq, B_GROUPS, SSM_STATE))
        outs[7].append(jnp.transpose(hi, (1, 0, 2)).reshape(nseq, B_GROUPS, SSM_STATE))
        outs[9].append(jnp.concatenate([cg[:, -2:], cu[:, -2:]], -1))
        outs[10].append(vn.reshape(nseq, steps, A_WIDTH))

    return (hp, hs.reshape(nseq, steps, D_MODEL)) + tuple(jnp.stack(o) for o in outs)
```

```python
import functools
import math

import jax
import jax.numpy as jnp
from jax import lax
from jax.experimental import pallas as pl
from jax.experimental.pallas import tpu as pltpu

F32 = jnp.float32
BF16 = jnp.bfloat16

D_MODEL = 1024
DEPTH = 4
PAST_LEN = 8192
HEAD_DIM = 64
A_WIDTH = 256
B_WIDTH = 384
C_WIDTH = 384
A_HEADS = 4
C_HEADS = 6
CHUNK = 128
SSM_GROUP = 16
B_GROUPS = 24
SSM_STATE = 64
DILATIONS = ((128, 1), (512, 4), (2048, 16))
MAX_WINDOW = 2048
ROPE_THETA = 500000.0
ROT_DIM = 16
D_FF = 2816
EPS = 1e-6
NEG_INF = -1e30

O_A, O_B, O_Q, O_K, O_V = 0, 2 * A_WIDTH, 896, 1280, 1664

LANES = 128
SUBLANES = 8
S5_BLOCKS = B_WIDTH // LANES
S5_HALF = 512
FF_CHUNK = 256
N_FF = D_FF // FF_CHUNK
VMEM_LIMIT = 56 * 1024 * 1024


def _dot(a, b):
    return jnp.dot(a.astype(BF16), b.astype(BF16), preferred_element_type=F32)


def _rms(x, g):
    return x * lax.rsqrt(jnp.mean(x * x, -1, keepdims=True) + EPS) * g


def _params(*sem):
    return pltpu.CompilerParams(dimension_semantics=sem, vmem_limit_bytes=VMEM_LIMIT)


def _rope_tables_kernel(c_ref, a_ref, b_ref, *, pos0):
    n = c_ref.shape[0]
    pos = (lax.broadcasted_iota(jnp.int32, (n, LANES), 0) + (pos0 + pl.program_id(0) * n)).astype(F32)
    d = lax.broadcasted_iota(jnp.int32, (n, LANES), 1) & (HEAD_DIM - 1)
    k = (d & (ROT_DIM // 2 - 1)).astype(F32)
    inv = jnp.exp(k * (-2.0 / ROT_DIM * math.log(ROPE_THETA)))
    ang = pos * inv
    cos, sin = jnp.cos(ang), jnp.sin(ang)
    c_ref[...] = jnp.where(d < ROT_DIM, cos, 1.0)
    a_ref[...] = jnp.where(d < ROT_DIM // 2, -sin, 0.0)
    b_ref[...] = jnp.where((d >= ROT_DIM // 2) & (d < ROT_DIM), sin, 0.0)


def rope_tables(n, pos0):
    sds = jax.ShapeDtypeStruct((n, LANES), F32)
    tr = min(n, 1024)
    spec = pl.BlockSpec((tr, LANES), lambda i: (i, 0))
    return pl.pallas_call(functools.partial(_rope_tables_kernel, pos0=pos0), grid=(n // tr,),
                          out_specs=(spec, spec, spec), out_shape=(sds, sds, sds),
                          compiler_params=_params("arbitrary"), name="rope_tables")()


def _s5_params_kernel(lr_ref, li_ref, ldt_ref, br_ref, bi_ref, cim_ref,
                      bbr_ref, bbi_ref, pr_ref, pi_ref, ncim_ref, *, n_pow):
    lr, li = lr_ref[...], li_ref[...]
    dt = jnp.exp(ldt_ref[...])
    mag = jnp.exp(lr * dt)
    ar, ai = mag * jnp.cos(li * dt), mag * jnp.sin(li * dt)
    nr, ni = ar - 1.0, ai
    den = lr * lr + li * li
    fr, fi = (nr * lr + ni * li) / den, (ni * lr - nr * li) / den
    br, bi = br_ref[...], bi_ref[...]
    bbr_ref[...] = fr[None] * br - fi[None] * bi
    bbi_ref[...] = fr[None] * bi + fi[None] * br
    k = (lax.broadcasted_iota(jnp.int32, (n_pow,) + lr.shape, 0) + 1).astype(F32)
    magk = jnp.exp(k * (lr * dt)[None])
    angk = k * (li * dt)[None]
    pr_ref[...] = magk * jnp.cos(angk)
    pi_ref[...] = magk * jnp.sin(angk)
    ncim_ref[...] = -cim_ref[...]


def s5_params(lam_re, lam_im, log_dt, b_re, b_im, c_im, n_pow):
    depth, g, p = lam_re.shape
    c = b_re.shape[-1]
    ldt = jnp.broadcast_to(log_dt[:, :, None], (depth, g, p))
    brt = jnp.transpose(b_re, (0, 3, 1, 2))
    bit = jnp.transpose(b_im, (0, 3, 1, 2))
    gp = pl.BlockSpec((None, g, p), lambda l: (l, 0, 0))
    cgp = pl.BlockSpec((None, c, g, p), lambda l: (l, 0, 0, 0))
    gcp = pl.BlockSpec((None, g, c, p), lambda l: (l, 0, 0, 0))
    kgp = pl.BlockSpec((None, n_pow, g, p), lambda l: (l, 0, 0, 0))
    return pl.pallas_call(
        functools.partial(_s5_params_kernel, n_pow=n_pow),
        grid=(depth,),
        in_specs=[gp, gp, gp, cgp, cgp, gcp],
        out_specs=(cgp, cgp, kgp, kgp, gcp),
        out_shape=(jax.ShapeDtypeStruct((depth, c, g, p), F32), jax.ShapeDtypeStruct((depth, c, g, p), F32),
                   jax.ShapeDtypeStruct((depth, n_pow, g, p), F32), jax.ShapeDtypeStruct((depth, n_pow, g, p), F32),
                   jax.ShapeDtypeStruct((depth, g, c, p), F32)),
        compiler_params=_params("arbitrary"),
        name="s5_params",
    )(lam_re, lam_im, ldt, brt, bit, c_im)


def s5_layouts(bbr, bbi, c_re, ncim, pr, pi):
    eye = jnp.eye(SUBLANES, dtype=F32)
    bb = jnp.stack([bbr, bbi], 0).reshape(2, SSM_GROUP, S5_BLOCKS, 8, SSM_STATE)
    wb = jnp.einsum('ecjgp,gh->jgcehp', bb, eye).reshape(S5_BLOCKS, LANES, 2 * S5_HALF)
    cc = jnp.stack([c_re, ncim], 0).reshape(2, S5_BLOCKS, 8, SSM_GROUP, SSM_STATE)
    wc = jnp.einsum('ejgcp,gh->jeghpc', cc, eye)
    wc = jnp.transpose(wc, (0, 1, 2, 4, 3, 5)).reshape(S5_BLOCKS, 2 * S5_HALF, LANES)
    n_pow = pr.shape[0]
    prl = jnp.transpose(pr.reshape(n_pow, S5_BLOCKS, S5_HALF), (1, 0, 2))
    pil = jnp.transpose(pi.reshape(n_pow, S5_BLOCKS, S5_HALF), (1, 0, 2))
    return wb.astype(BF16), wc.astype(BF16), prl, pil


SEG = 8
SEG_W = SEG * LANES


def _s5_seg_params_kernel(p0r_ref, p0i_ref, p1r_ref, p1i_ref, bbr_ref, bbi_ref, cre_ref, ncim_ref, cg_ref, ncg_ref,
                          k_ref, abr_ref, abi_ref, car_ref, cani_ref):
    abr = p0r_ref[...] * bbr_ref[...] - p0i_ref[...] * bbi_ref[...]
    abi = p0r_ref[...] * bbi_ref[...] + p0i_ref[...] * bbr_ref[...]
    abr_ref[...] = abr
    abi_ref[...] = abi
    car_ref[...] = cre_ref[...] * p1r_ref[...] + ncim_ref[...] * p1i_ref[...]
    cani_ref[...] = ncim_ref[...] * p1r_ref[...] - cre_ref[...] * p1i_ref[...]
    nt = (((1,), (1,)), ((), ()))
    for g in range(B_GROUPS):
        k_ref[g] = (lax.dot_general(cg_ref[g], abr[g], nt, precision=lax.Precision.HIGHEST,
                                    preferred_element_type=F32)
                    + lax.dot_general(ncg_ref[g], abi[g], nt, precision=lax.Precision.HIGHEST,
                                      preferred_element_type=F32))


def s5_seg_params(pr, pi, bbr, bbi, c_re, ncim):
    depth = pr.shape[0]
    g, c, p = B_GROUPS, SSM_GROUP, SSM_STATE
    over_c = lambda a: jnp.broadcast_to(jnp.transpose(a, (0, 2, 1, 3))[:, :, :, None, :],
                                        (depth, g, SEG, c, p)).reshape(depth, g, SEG * c, p)
    over_k = lambda a: jnp.broadcast_to(a[:, :, None], (depth, g, SEG, c, p)).reshape(depth, g, SEG * c, p)
    p0r = jnp.concatenate([jnp.ones_like(pr[:, :1]), pr[:, :SEG - 1]], 1)
    p0i = jnp.concatenate([jnp.zeros_like(pi[:, :1]), pi[:, :SEG - 1]], 1)
    bg = lambda a: jnp.transpose(a, (0, 2, 1, 3))
    big = pl.BlockSpec((None, g, SEG * c, p), lambda l: (l, 0, 0, 0))
    small = pl.BlockSpec((None, g, c, p), lambda l: (l, 0, 0, 0))
    big_sd = jax.ShapeDtypeStruct((depth, g, SEG * c, p), F32)
    return pl.pallas_call(
        _s5_seg_params_kernel,
        grid=(depth,),
        in_specs=[big] * 8 + [small, small],
        out_specs=(pl.BlockSpec((None, g, c, SEG * c), lambda l: (l, 0, 0, 0)), big, big, big, big),
        out_shape=(jax.ShapeDtypeStruct((depth, g, c, SEG * c), F32), big_sd, big_sd, big_sd, big_sd),
        compiler_params=_params("arbitrary"),
        name="s5_seg_params",
    )(over_c(p0r), over_c(p0i), over_c(pr[:, :SEG]), over_c(pi[:, :SEG]), over_k(bg(bbr)), over_k(bg(bbi)),
      over_k(c_re), over_k(ncim), c_re, ncim)


def s5_seg_layouts(kk, abr, abi, car, cani, pr, pi, d_skip):
    depth = kk.shape[0]
    c, q = SSM_GROUP, SSM_STATE

    def state_rows(re, im):
        a = jnp.stack([re, im], 1).reshape(depth, 2, S5_BLOCKS, 8, SEG, c, q)
        a = jnp.transpose(a, (0, 2, 4, 5, 1, 3, 6)).reshape(depth, S5_BLOCKS, SEG, c, 2 * S5_HALF)
        same = jnp.arange(8)[:, None] == ((jnp.arange(2 * S5_HALF) // q) % 8)[None, :]
        a = jnp.where(same[None, None, None, :, None, :], a[:, :, :, None, :, :], 0.0)
        return a.reshape(depth, S5_BLOCKS, SEG, LANES, 2 * S5_HALF)

    eb = state_rows(abr, abi)[:, :, ::-1].reshape(depth, S5_BLOCKS, SEG_W, 2 * S5_HALF)
    gm = state_rows(car, cani).reshape(depth, S5_BLOCKS, SEG_W, 2 * S5_HALF)
    kt = jnp.transpose(kk.reshape(depth, S5_BLOCKS, 8, c, SEG, c), (0, 1, 4, 5, 2, 3))
    kt = kt.reshape(depth, S5_BLOCKS, SEG, c, LANES)
    same = jnp.arange(8)[:, None] == (jnp.arange(LANES) // c)[None, :]
    kb = jnp.where(same[None, None, None, :, None, :], kt[:, :, :, None, :, :], 0.0)
    kb = kb.reshape(depth, S5_BLOCKS, SEG, LANES, LANES)
    zero = jnp.zeros_like(kb[:, :, 0])
    t = jnp.concatenate([jnp.concatenate([kb[:, :, r - rp] if r >= rp else zero for r in range(SEG)], axis=3)
                         for rp in range(SEG)], axis=2)
    a8r = pr[:, SEG - 1].reshape(depth, S5_BLOCKS, 1, S5_HALF)
    a8i = pi[:, SEG - 1].reshape(depth, S5_BLOCKS, 1, S5_HALF)
    d_t = jnp.tile(d_skip.reshape(depth, S5_BLOCKS, 1, LANES), (1, 1, SEG, 1)).reshape(depth, 1, S5_BLOCKS * SEG_W)
    return t.astype(BF16), eb.astype(BF16), gm.astype(BF16), a8r, a8i, d_t


def _rope_apply(x, c, a, b):
    return x * c + pltpu.roll(x, LANES - ROT_DIM // 2, 1) * a + pltpu.roll(x, ROT_DIM // 2, 1) * b


def _inproj_kernel(x_ref, g_ref, w_ref, ws_ref, bs_ref, rc_ref, ra_ref, rb_ref,
                   oa_ref, vn_ref, ub_ref, kf_ref, vf_ref, *rest, seq_rows, dils, seg_major):
    qkv_refs, proj_scr, perm_scr, perm2_scr = rest[:-3], rest[-3], rest[-2], rest[-1]
    tm = x_ref.shape[0]
    xn = _rms(x_ref[...], g_ref[...]).astype(BF16)
    proj_scr[...] = jnp.dot(xn, w_ref[...], preferred_element_type=F32)

    h = jax.nn.gelu(proj_scr[:, O_A:O_B])
    u, v = h[:, :A_WIDTH], h[:, A_WIDTH:]
    mu = jnp.mean(v, -1, keepdims=True)
    var = jnp.mean(jnp.square(v - mu), -1, keepdims=True)
    vn = (v - mu) * lax.rsqrt(var + EPS)
    vn_ref[...] = vn
    ri = lax.broadcasted_iota(jnp.int32, (CHUNK, CHUNK), 0)
    ci = lax.broadcasted_iota(jnp.int32, (CHUNK, CHUNK), 1)
    keep = (ci <= ri) & ((ri // seq_rows) == (ci // seq_rows))
    lane = lax.broadcasted_iota(jnp.int32, (CHUNK, LANES), 1)
    wm = [jnp.where(keep, ws_ref[hh], 0.0).astype(BF16) for hh in range(A_HEADS)]
    vnb = vn.astype(BF16)
    for c in range(tm // CHUNK):
        rows = slice(c * CHUNK, (c + 1) * CHUNK)
        for p in range(A_WIDTH // LANES):
            cols = slice(p * LANES, (p + 1) * LANES)
            vp = vnb[rows, cols]
            m0 = jnp.dot(wm[2 * p], vp, preferred_element_type=F32)
            m1 = jnp.dot(wm[2 * p + 1], vp, preferred_element_type=F32)
            mixed = jnp.where(lane < HEAD_DIM, m0, m1) + bs_ref[:, cols]
            oa_ref[rows, cols] = u[rows, cols] * mixed

    if seg_major:
        nqkv = 3 * (C_WIDTH // LANES)
        for p in range(S5_BLOCKS):
            slot = nqkv + p
            perm_scr[slot] = proj_scr[:, O_B + p * LANES:O_B + (p + 1) * LANES]
            for c4 in range(4):
                perm2_scr[slot, c4 * (tm // 4):(c4 + 1) * (tm // 4), :] = perm_scr[slot, pl.ds(c4, tm // 4, stride=4), :]
            f = SEG // 4
            for c4 in range(4):
                for s in range(f):
                    c0 = p * SEG_W + (c4 + 4 * s) * LANES
                    ub_ref[:, c0:c0 + LANES] = perm2_scr[slot, pl.ds(c4 * (tm // 4) + s, tm // SEG, stride=f), :]
    else:
        ub_ref[...] = proj_scr[:, O_B:O_Q]

    rc, ra, rb = rc_ref[...], ra_ref[...], rb_ref[...]
    def emit(val, outs, slot, p):
        outs[0][:, p * LANES:(p + 1) * LANES] = val.astype(BF16)
        if len(dils) == 1:
            return
        perm_scr[slot] = val
        src, dst, prev = perm_scr, perm2_scr, 1
        for n in range(1, len(dils)):
            dil = dils[n]
            f = dil // prev
            for c in range(prev):
                for s in range(f):
                    cls = c + prev * s
                    blk = src[slot, pl.ds(c * (tm // prev) + s, tm // dil, stride=f), :]
                    c0 = cls * C_WIDTH + p * LANES
                    outs[n][:, c0:c0 + LANES] = blk.astype(BF16)
                    if n + 1 < len(dils):
                        dst[slot, cls * (tm // dil):(cls + 1) * (tm // dil), :] = blk
            src, dst, prev = dst, src, dil

    nd = len(dils)
    assert dils[0] == 1
    q_outs, k_outs, v_outs = qkv_refs[0:nd], qkv_refs[nd:2 * nd], qkv_refs[2 * nd:3 * nd]
    for p in range(C_WIDTH // LANES):
        cols = slice(p * LANES, (p + 1) * LANES)
        q = proj_scr[:, O_Q + p * LANES:O_Q + (p + 1) * LANES]
        emit(_rope_apply(q, rc, ra, rb) * (HEAD_DIM ** -0.5), q_outs, 3 * p, p)
        k = _rope_apply(proj_scr[:, O_K + p * LANES:O_K + (p + 1) * LANES], rc, ra, rb)
        emit(k, k_outs, 3 * p + 1, p)
        kf_ref[:, cols] = k
        vv = proj_scr[:, O_V + p * LANES:O_V + (p + 1) * LANES]
        emit(vv, v_outs, 3 * p + 2, p)
        vf_ref[:, cols] = vv


def in_projection(x, g1, w_in, layer, w_s, bs_rows, rope, *, tm, seq_rows, tail_rows, dils, seg_major=False):
    bsz, s, _ = x.shape
    nt = s // tm
    tail_t = tail_rows // tm
    row = lambda w: pl.BlockSpec((None, tm, w), lambda b, t: (b, t, 0))
    tail = pl.BlockSpec((None, tm, C_WIDTH), lambda b, t: (b, jnp.maximum(t - (nt - tail_t), 0), 0))
    full = lambda shp: pl.BlockSpec(shp, lambda b, t: (0,) * len(shp))
    ropespec = pl.BlockSpec((tm, LANES), lambda b, t: (t, 0))
    sd = lambda w, dt: jax.ShapeDtypeStruct((bsz, s, w), dt)
    tail_sd = jax.ShapeDtypeStruct((bsz, tail_rows, C_WIDTH), F32)
    qkv_specs = [pl.BlockSpec((None, tm // d, d * C_WIDTH), lambda b, t: (b, t, 0)) for d in dils] * 3
    qkv_sds = [jax.ShapeDtypeStruct((bsz, s // d, d * C_WIDTH), BF16) for d in dils] * 3
    if seg_major:
        ub_spec = pl.BlockSpec((None, tm // SEG, SEG * B_WIDTH), lambda b, t: (b, t, 0))
        ub_sd = jax.ShapeDtypeStruct((bsz, s // SEG, SEG * B_WIDTH), F32)
    else:
        ub_spec, ub_sd = row(B_WIDTH), sd(B_WIDTH, F32)
    n_perm = 3 * (C_WIDTH // LANES) + S5_BLOCKS
    return pl.pallas_call(
        functools.partial(_inproj_kernel, seq_rows=seq_rows, dils=tuple(dils), seg_major=seg_major),
        grid=(bsz, nt),
        in_specs=[row(D_MODEL), _layer_spec(g1, layer), _layer_spec(w_in, layer), full(w_s.shape), full(bs_rows.shape),
                  ropespec, ropespec, ropespec],
        out_specs=(row(A_WIDTH), row(A_WIDTH), ub_spec, tail, tail, *qkv_specs),
        out_shape=(sd(A_WIDTH, F32), sd(A_WIDTH, F32), ub_sd, tail_sd, tail_sd, *qkv_sds),
        scratch_shapes=[pltpu.VMEM((tm, w_in.shape[-1]), F32),
                        pltpu.VMEM((n_perm, tm, LANES), F32),
                        pltpu.VMEM((n_perm, tm, LANES), F32)],
        compiler_params=_params("arbitrary", "arbitrary"),
        name="in_projection",
    )(x, g1, w_in, w_s, bs_rows, *rope)


def _s5_readout(hb, uperm, gs, wc_ref, d_ref, wg_ref, bg_ref):
    for j in range(S5_BLOCKS):
        cols = slice(j * LANES, (j + 1) * LANES)
        y = jnp.dot(hb[j], wc_ref[j], preferred_element_type=F32) + d_ref[:, cols] * uperm[:, cols]
        gs[:, cols] = jax.nn.gelu(y)
    g = gs[...]
    gs[...] = g * jax.nn.sigmoid(_dot(g, wg_ref[...]) + bg_ref[...])


def _s5_seg_kernel(u_ref, t_ref, eb_ref, gm_ref, a8r_ref, a8i_ref, d_ref, wg_ref, bg_ref,
                   o_ref, hr_ref, hi_ref, e_scr, cs_scr, gs_scr, carry):
    nseg = u_ref.shape[0]

    @pl.when(pl.program_id(1) == 0)
    def _():
        carry[...] = jnp.zeros_like(carry)

    for p in range(S5_BLOCKS):
        ubf = u_ref[:, p * SEG_W:(p + 1) * SEG_W].astype(BF16)
        e_scr[p] = jnp.dot(ubf, eb_ref[p], preferred_element_type=F32)

    sub = lax.broadcasted_iota(jnp.int32, (SUBLANES, S5_HALF), 0)
    a8 = [(a8r_ref[p], a8i_ref[p]) for p in range(S5_BLOCKS)]

    def block(k, c):
        r0 = pl.multiple_of(k * SUBLANES, SUBLANES)
        out = []
        for p in range(S5_BLOCKS):
            cr, ci = c[2 * p], c[2 * p + 1]
            er = e_scr[p, pl.ds(r0, SUBLANES), 0:S5_HALF]
            ei = e_scr[p, pl.ds(r0, SUBLANES), S5_HALF:2 * S5_HALF]
            br = jnp.zeros((SUBLANES, S5_HALF), F32)
            bi = jnp.zeros((SUBLANES, S5_HALF), F32)
            for j in range(SUBLANES):
                br = jnp.where(sub == j, cr, br)
                bi = jnp.where(sub == j, ci, bi)
                cr, ci = (a8[p][0] * cr - a8[p][1] * ci + er[j:j + 1, :],
                          a8[p][0] * ci + a8[p][1] * cr + ei[j:j + 1, :])
            cs_scr[p, pl.ds(r0, SUBLANES), 0:S5_HALF] = br
            cs_scr[p, pl.ds(r0, SUBLANES), S5_HALF:2 * S5_HALF] = bi
            out += [cr, ci]
        return tuple(out)

    c0 = tuple(carry[p, e, 0:1, :] for p in range(S5_BLOCKS) for e in range(2))
    cf = lax.fori_loop(0, nseg // SUBLANES, block, c0)
    for p in range(S5_BLOCKS):
        carry[p, 0, 0:1, :] = cf[2 * p]
        carry[p, 1, 0:1, :] = cf[2 * p + 1]
        hr_ref[p] = cf[2 * p]
        hi_ref[p] = cf[2 * p + 1]

    for p in range(S5_BLOCKS):
        cols = slice(p * SEG_W, (p + 1) * SEG_W)
        u = u_ref[:, cols]
        y = (jnp.dot(u.astype(BF16), t_ref[p], preferred_element_type=F32)
             + lax.dot_general(cs_scr[p].astype(BF16), gm_ref[p], (((1,), (1,)), ((), ())),
                               preferred_element_type=F32) + d_ref[:, cols] * u)
        gs_scr[:, cols] = jax.nn.gelu(y)
    for r in range(SEG):
        tok = [slice(p * SEG_W + r * LANES, p * SEG_W + (r + 1) * LANES) for p in range(S5_BLOCKS)]
        g = jnp.concatenate([gs_scr[:, c] for c in tok], axis=1)
        out = g * jax.nn.sigmoid(_dot(g, wg_ref[...]) + bg_ref[...])
        for p in range(S5_BLOCKS):
            o_ref[:, tok[p]] = out[:, p * LANES:(p + 1) * LANES]


def s5_seg(u, t, eb, gm, a8r, a8i, d_t, w_glu, b_glu, layer, *, rows):
    bsz, nseg, width = u.shape
    resident = lambda a: _layer_spec(a, layer, pipeline_mode=pl.Buffered(1))
    blk = pl.BlockSpec((None, rows, width), lambda b, c: (b, c, 0))
    st = pl.BlockSpec((None, S5_BLOCKS, 1, S5_HALF), lambda b, c: (b, 0, 0, 0))
    st_sd = jax.ShapeDtypeStruct((bsz, S5_BLOCKS, 1, S5_HALF), F32)
    return pl.pallas_call(
        _s5_seg_kernel,
        grid=(bsz, nseg // rows),
        in_specs=[blk] + [resident(a) for a in (t, eb, gm, a8r, a8i, d_t, w_glu, b_glu)],
        out_specs=(blk, st, st),
        out_shape=(jax.ShapeDtypeStruct(u.shape, F32), st_sd, st_sd),
        scratch_shapes=[pltpu.VMEM((S5_BLOCKS, rows, 2 * S5_HALF), F32),
                        pltpu.VMEM((S5_BLOCKS, rows, 2 * S5_HALF), F32),
                        pltpu.VMEM((rows, width), F32),
                        pltpu.VMEM((S5_BLOCKS, 2, SUBLANES, S5_HALF), F32)],
        compiler_params=_params("arbitrary", "arbitrary"),
        name="s5_seg",
    )(u, t, eb, gm, a8r, a8i, d_t, w_glu, b_glu)


def _s5_sample_kernel(u_ref, h0r_ref, h0i_ref, wb_ref, wc_ref, pr_ref, pi_ref, d_ref, wg_ref, bg_ref,
                      o_ref, hr_ref, hi_ref,
                      upad, uperm, xs, hb, gs, *, nseq, steps):
    nblk = nseq // SUBLANES
    for p in range(S5_BLOCKS):
        upad[p] = u_ref[:, p * LANES:(p + 1) * LANES]
    for st in range(steps):
        for bb in range(nblk):
            r = (st * nblk + bb) * SUBLANES
            for p in range(S5_BLOCKS):
                uperm[r:r + SUBLANES, p * LANES:(p + 1) * LANES] = (
                    upad[p, pl.ds(bb * SUBLANES * steps + st, SUBLANES, stride=steps), :])
    for j in range(S5_BLOCKS):
        xs[j] = _dot(uperm[:, j * LANES:(j + 1) * LANES], wb_ref[j])
    for j in range(S5_BLOCKS):
        ar = jnp.broadcast_to(pr_ref[j, 0:1, :], (SUBLANES, S5_HALF))
        ai = jnp.broadcast_to(pi_ref[j, 0:1, :], (SUBLANES, S5_HALF))
        for bb in range(nblk):
            seqs = slice(bb * SUBLANES, (bb + 1) * SUBLANES)
            hr, hi = h0r_ref[j, seqs, :], h0i_ref[j, seqs, :]
            for st in range(steps):
                r = (st * nblk + bb) * SUBLANES
                hr, hi = (ar * hr - ai * hi + xs[j, r:r + SUBLANES, 0:S5_HALF],
                          ar * hi + ai * hr + xs[j, r:r + SUBLANES, S5_HALF:2 * S5_HALF])
                xs[j, r:r + SUBLANES, 0:S5_HALF] = hr
                xs[j, r:r + SUBLANES, S5_HALF:2 * S5_HALF] = hi
            hr_ref[j, seqs, :] = hr
            hi_ref[j, seqs, :] = hi
        hb[j] = xs[j].astype(BF16)
    _s5_readout(hb, uperm, gs, wc_ref, d_ref, wg_ref, bg_ref)
    for st in range(steps):
        for bb in range(nblk):
            r = (st * nblk + bb) * SUBLANES
            for p in range(S5_BLOCKS):
                upad[p, pl.ds(bb * SUBLANES * steps + st, SUBLANES, stride=steps), :] = (
                    gs[r:r + SUBLANES, p * LANES:(p + 1) * LANES])
    for p in range(S5_BLOCKS):
        o_ref[:, p * LANES:(p + 1) * LANES] = upad[p]


def s5_sample(u, h0r, h0i, wb, wc, prl, pil, d_skip, w_glu, b_glu, *, nseq, steps):
    n = nseq * steps
    st_sd = jax.ShapeDtypeStruct((S5_BLOCKS, nseq, S5_HALF), F32)
    return pl.pallas_call(
        functools.partial(_s5_sample_kernel, nseq=nseq, steps=steps),
        out_shape=(jax.ShapeDtypeStruct((n, B_WIDTH), F32), st_sd, st_sd),
        scratch_shapes=[pltpu.VMEM((S5_BLOCKS, n, LANES), F32),
                        pltpu.VMEM((n, B_WIDTH), F32),
                        pltpu.VMEM((S5_BLOCKS, n, 2 * S5_HALF), F32),
                        pltpu.VMEM((S5_BLOCKS, n, 2 * S5_HALF), BF16),
                        pltpu.VMEM((n, B_WIDTH), F32)],
        compiler_params=pltpu.CompilerParams(vmem_limit_bytes=VMEM_LIMIT),
        name="s5_sample",
    )(u, h0r, h0i, wb, wc, prl, pil, d_skip, w_glu, b_glu)


def _attn_prompt_kernel(q_ref, kp_ref, kc_ref, vp_ref, vc_ref, o_ref, st_ref, kx, vx):
    tq = q_ref.shape[0]
    t = pl.program_id(2)
    kx[0:CHUNK, :] = kp_ref[...]
    kx[CHUNK:CHUNK + tq, :] = kc_ref[...]
    vx[0:CHUNK, :] = vp_ref[...]
    vx[CHUNK:CHUNK + tq, :] = vc_ref[...]
    r = lax.broadcasted_iota(jnp.int32, (CHUNK, 2 * CHUNK), 0)
    c = lax.broadcasted_iota(jnp.int32, (CHUNK, 2 * CHUNK), 1)
    band = (c >= r) & (c <= r + CHUNK)
    lane = lax.broadcasted_iota(jnp.int32, (CHUNK, LANES), 1)
    lo_half = lane < HEAD_DIM
    for u in range(tq // CHUNK):
        rows = slice(u * CHUNK, (u + 1) * CHUNK)
        keys = slice(u * CHUNK, (u + 2) * CHUNK)
        if u == 0:
            valid = band & (c >= jnp.where(t == 0, CHUNK, 0))
        else:
            valid = band
        bias = jnp.where(valid, 0.0, NEG_INF)
        st = jnp.zeros((CHUNK, LANES), F32)
        for p in range(C_WIDTH // LANES):
            cols = slice(p * LANES, (p + 1) * LANES)
            qp = q_ref[rows, cols]
            kpair = kx[keys, cols]
            vpair = vx[keys, cols]
            outs = []
            for hh in range(2):
                qm = jnp.where(lo_half if hh == 0 else ~lo_half, qp, jnp.zeros_like(qp))
                sc = lax.dot_general(qm, kpair, (((1,), (1,)), ((), ())), preferred_element_type=F32) + bias
                m = jnp.max(sc, -1, keepdims=True)
                pe = jnp.exp(sc - m)
                l = jnp.sum(pe, -1, keepdims=True)
                outs.append(jnp.dot(pe.astype(BF16), vpair, preferred_element_type=F32) / l)
                st = jnp.where(lane == 2 * p + hh, m + jnp.log(l), st)
            o_ref[rows, cols] = jnp.where(lo_half, outs[0], outs[1])
        st_ref[rows, :] = st


def attn_prompt(q, k, v, dil, *, tq):
    bsz, ln, _ = q.shape
    tq = min(tq, ln)
    cur = pl.BlockSpec((None, tq, C_WIDTH), lambda b, r, t: (b, t, r))
    prev = pl.BlockSpec((None, CHUNK, C_WIDTH), lambda b, r, t: (b, jnp.maximum(t * (tq // CHUNK) - 1, 0), r))
    return pl.pallas_call(
        _attn_prompt_kernel,
        grid=(bsz, dil, ln // tq),
        in_specs=[cur, prev, cur, prev, cur],
        out_specs=(cur, pl.BlockSpec((None, tq, LANES), lambda b, r, t: (b, t, r))),
        out_shape=(jax.ShapeDtypeStruct((bsz, ln, dil * C_WIDTH), F32),
                   jax.ShapeDtypeStruct((bsz, ln, dil * LANES), F32)),
        scratch_shapes=[pltpu.VMEM((CHUNK + tq, C_WIDTH), BF16), pltpu.VMEM((CHUNK + tq, C_WIDTH), BF16)],
        compiler_params=_params("arbitrary", "arbitrary", "arbitrary"),
        name=f"attn_prompt_d{dil}",
    )(q, k, k, v, v)


def _attn_sample_kernel(q_ref, kn_ref, vn_ref, ck_ref, cv_ref, o_ref, *, steps):
    cw = ck_ref.shape[-1]

    def mult(delta):
        cnt = jnp.zeros(delta.shape, F32)
        for window, dil in DILATIONS:
            cnt = cnt + jnp.where((delta >= 0) & (delta <= window) & ((delta & (dil - 1)) == 0), 1.0, 0.0)
        return cnt

    cnt_c = mult(cw + lax.broadcasted_iota(jnp.int32, (steps, cw), 0)
                 - lax.broadcasted_iota(jnp.int32, (steps, cw), 1))
    cnt_n = mult(lax.broadcasted_iota(jnp.int32, (steps, CHUNK), 0)
                 - lax.broadcasted_iota(jnp.int32, (steps, CHUNK), 1))
    live_c, live_n = cnt_c > 0, cnt_n > 0
    zpad = jnp.zeros((CHUNK - steps, C_WIDTH), F32)
    kn = jnp.concatenate([kn_ref[...], zpad], 0).astype(BF16)
    vn = jnp.concatenate([vn_ref[...], zpad], 0).astype(BF16)
    q = q_ref[...].astype(BF16)
    nt_dims = (((1,), (1,)), ((), ()))
    for hh in range(C_HEADS):
        cols = slice(hh * HEAD_DIM, (hh + 1) * HEAD_DIM)
        qh = q[:, cols]
        sc = jnp.dot(qh, ck_ref[hh].astype(BF16), preferred_element_type=F32)
        sn = lax.dot_general(qh, kn[:, cols], nt_dims, preferred_element_type=F32)
        m = jnp.maximum(jnp.max(jnp.where(live_c, sc, NEG_INF), -1, keepdims=True),
                        jnp.max(jnp.where(live_n, sn, NEG_INF), -1, keepdims=True))
        ec = cnt_c * jnp.exp(jnp.where(live_c, sc - m, NEG_INF))
        en = cnt_n * jnp.exp(jnp.where(live_n, sn - m, NEG_INF))
        l = jnp.sum(ec, -1, keepdims=True) + jnp.sum(en, -1, keepdims=True)
        acc = (lax.dot_general(ec.astype(BF16), cv_ref[hh].astype(BF16), nt_dims, preferred_element_type=F32)
               + jnp.dot(en.astype(BF16), vn[:, cols], preferred_element_type=F32))
        o_ref[:, cols] = acc / l


def attn_sample(q, k_new, v_new, cache_kt, cache_vt, layer, *, steps):
    _, nseq, _, _, cw = cache_kt.shape
    rows = pl.BlockSpec((steps, C_WIDTH), lambda b: (b, 0))
    cache = pl.BlockSpec((None, None, C_HEADS, HEAD_DIM, cw), lambda b: (layer, b, 0, 0, 0))
    return pl.pallas_call(
        functools.partial(_attn_sample_kernel, steps=steps),
        grid=(nseq,),
        in_specs=[rows, rows, rows, cache, cache],
        out_specs=rows,
        out_shape=jax.ShapeDtypeStruct((nseq * steps, C_WIDTH), F32),
        compiler_params=_params("arbitrary"),
        name="attn_sample",
    )(q, k_new, v_new, cache_kt, cache_vt)


FF_ROWS = 128
FF_PIECES = 4


def _residual_and_norm(x_ref, oa, ob, oc, wo_ref, g2_ref, y_ref, xn_scr):
    mix = (_dot(oa, wo_ref[0:A_WIDTH, :]) + _dot(ob, wo_ref[A_WIDTH:A_WIDTH + B_WIDTH, :])
           + _dot(oc, wo_ref[A_WIDTH + B_WIDTH:, :]))
    x1 = x_ref[...] + mix
    y_ref[...] = x1
    xn_scr[...] = _rms(x1, g2_ref[...]).astype(BF16)


def _outproj_prompt_kernel(x_ref, oa_ref, ob_ref, *rest, tiles_per_seq, dils, final_norm):
    nb = len(dils)
    o_refs, st_refs = rest[:nb], rest[nb:2 * nb]
    (e_ref, wo_ref, g2_ref, wup_ref, cw_ref, cb_ref, wd_ref, fg_ref,
     y_ref, cg_ref, cu_ref, xn_scr, carry_scr, unp_scr, hs0, hs1, act0, act1) = rest[2 * nb:]
    hs_scr, act_scr = (hs0, hs1), (act0, act1)
    i = pl.program_id(0)
    tm = x_ref.shape[0]
    nslab = C_WIDTH // LANES

    outs, sts = [], []
    slot = 0
    for dil, o_ref, s_ref in zip(dils, o_refs, st_refs):
        if dil == 1:
            outs.append(o_ref[...])
            sts.append(s_ref[...])
            continue
        for r in range(dil):
            rows = pl.ds(r, tm // dil, stride=dil)
            for p in range(nslab):
                c0 = r * C_WIDTH + p * LANES
                unp_scr[slot + p, rows, :] = o_ref[:, c0:c0 + LANES]
            unp_scr[slot + nslab, rows, :] = s_ref[:, r * LANES:(r + 1) * LANES]
        outs.append(jnp.concatenate([unp_scr[slot + p] for p in range(nslab)], axis=1))
        sts.append(unp_scr[slot + nslab])
        slot += nslab + 1
    mx = functools.reduce(jnp.maximum, sts)
    ws = [jnp.exp(s - mx) for s in sts]
    tot = functools.reduce(lambda a, b: a + b, ws)
    oc = jnp.zeros((tm, C_WIDTH), F32)
    for w, o in zip(ws, outs):
        alpha = w / tot
        hi = alpha.astype(BF16)
        lo = (alpha - hi.astype(F32)).astype(BF16)
        wide = jnp.dot(jnp.concatenate([hi, lo], axis=1), e_ref[...], preferred_element_type=F32)
        oc = oc + wide * o
    for p in range(S5_BLOCKS):
        for r in range(SEG):
            c0 = p * SEG_W + r * LANES
            unp_scr[slot + p, pl.ds(r, tm // SEG, stride=SEG), :] = ob_ref[:, c0:c0 + LANES]
    ob = jnp.concatenate([unp_scr[slot + p] for p in range(S5_BLOCKS)], axis=1)
    _residual_and_norm(x_ref, oa_ref[...], ob, oc, wo_ref, g2_ref, y_ref, xn_scr)

    @pl.when((i % tiles_per_seq) == 0)
    def _():
        carry_scr[...] = jnp.zeros_like(carry_scr)

    nhs = 2 * FF_CHUNK // LANES
    gate = FF_CHUNK // LANES
    piece = tm // FF_PIECES

    def up(jj, slot, pc):
        xr = xn_scr[pc * piece:(pc + 1) * piece, :]
        rows = slice(SUBLANES + pc * piece, SUBLANES + (pc + 1) * piece)
        for half in range(2):
            h = jnp.dot(xr, wup_ref[half * N_FF + jj], preferred_element_type=F32)
            for k in range(gate):
                hs_scr[slot][2 * (half * gate + k), rows, :] = h[:, k * LANES:(k + 1) * LANES]

    def down(jj, slot, pc):
        rows = slice(pc * piece, (pc + 1) * piece)
        y_ref[rows, :] += jnp.dot(act_scr[slot][rows, :], wd_ref[jj], preferred_element_type=F32)

    def conv_taps(jj, slot):
        for k in range(nhs):
            hs_scr[slot][2 * k, 0:SUBLANES, :] = carry_scr[jj, k]
        cwj = jnp.concatenate([cw_ref[jj], cw_ref[N_FF + jj]], axis=1)
        cbj = jnp.concatenate([cb_ref[jj], cb_ref[N_FF + jj]], axis=1)
        slabs = lambda row: jnp.stack([row[:, k * LANES:(k + 1) * LANES] for k in range(nhs)])
        return [slabs(cwj[t:t + 1]) for t in range(3)], slabs(cbj)

    def conv(jj, slot, pc, taps, bias):
        every_other = pl.ds(0, nhs, stride=2)
        for r in range(pc * piece // FF_ROWS, (pc + 1) * piece // FF_ROWS):
            shifted = lambda back: hs_scr[slot][every_other, pl.ds(SUBLANES - back + r * FF_ROWS, FF_ROWS), :]
            hc = bias + (shifted(2) * taps[0] + shifted(1) * taps[1] + shifted(0) * taps[2])
            act = jax.nn.gelu(hc[:gate]) * hc[gate:]
            for k in range(gate):
                act_scr[slot][r * FF_ROWS:(r + 1) * FF_ROWS, k * LANES:(k + 1) * LANES] = act[k].astype(BF16)

    def conv_tail(jj, slot):
        for k in range(nhs):
            last = hs_scr[slot][2 * k, tm:tm + SUBLANES, :]
            carry_scr[jj, k] = last
            if k < gate:
                cg_ref[0, jj, :, k * LANES:(k + 1) * LANES] = last
            else:
                cu_ref[0, jj, :, (k - gate) * LANES:(k - gate + 1) * LANES] = last

    def stage(j, slot, do_down, do_up):
        taps, bias = conv_taps(j, slot)
        for pc in range(FF_PIECES):
            conv(j, slot, pc, taps, bias)
            if do_down:
                down(j - 1, 1 - slot, pc)
            if do_up:
                up(j + 1, 1 - slot, pc)
        conv_tail(j, slot)

    for pc in range(FF_PIECES):
        up(0, 0, pc)
    stage(0, 0, False, True)

    def steady(n, _):
        stage(2 * n + 1, 1, True, True)
        stage(2 * n + 2, 0, True, True)
        return 0
    n_pairs = (N_FF - 2) // 2
    lax.fori_loop(0, n_pairs, steady, 0)
    for j in range(2 * n_pairs + 1, N_FF):
        stage(j, j % 2, True, j + 1 < N_FF)
    for pc in range(FF_PIECES):
        down(N_FF - 1, (N_FF - 1) % 2, pc)

    if final_norm:
        y_ref[...] = _rms(y_ref[...], fg_ref[...])


def _outproj_sample_kernel(x_ref, oa_ref, ob_ref, oc_ref, wo_ref, g2_ref, wug_ref, wuu_ref, cwg_ref, cwu_ref,
                           cbg_ref, cbu_ref, wd_ref, fg_ref, bufg_ref, bufu_ref,
                           y_ref, cg_ref, cu_ref, xn_scr, hs_scr, *, nseq, seq_len, final_norm):
    j = pl.program_id(1)
    tm = x_ref.shape[0]

    @pl.when(j == 0)
    def _():
        _residual_and_norm(x_ref, oa_ref[...], ob_ref[...], oc_ref[...], wo_ref, g2_ref, y_ref, xn_scr)

    xn = xn_scr[...]
    hg = jnp.dot(xn, wug_ref[...], preferred_element_type=F32)
    hu = jnp.dot(xn, wuu_ref[...], preferred_element_type=F32)
    lo, hi_ = SUBLANES, SUBLANES + seq_len
    hs_scr[:, lo:hi_, 0:FF_CHUNK] = hg.reshape(nseq, seq_len, FF_CHUNK)
    hs_scr[:, lo:hi_, FF_CHUNK:] = hu.reshape(nseq, seq_len, FF_CHUNK)
    hs_scr[:, lo - 2:lo, 0:FF_CHUNK] = bufg_ref[...]
    hs_scr[:, lo - 2:lo, FF_CHUNK:] = bufu_ref[...]
    cg_ref[...] = hs_scr[:, hi_ - SUBLANES:hi_, 0:FF_CHUNK]
    cu_ref[...] = hs_scr[:, hi_ - SUBLANES:hi_, FF_CHUNK:]

    cw = jnp.concatenate([cwg_ref[...], cwu_ref[...]], axis=1)
    cb = jnp.concatenate([cbg_ref[...], cbu_ref[...]], axis=1)
    hc = cb + (hs_scr[:, lo - 2:hi_ - 2, :] * cw[0:1] + hs_scr[:, lo - 1:hi_ - 1, :] * cw[1:2]
               + hs_scr[:, lo:hi_, :] * cw[2:3])
    hc = hc.reshape(tm, 2 * FF_CHUNK)
    act = jax.nn.gelu(hc[:, :FF_CHUNK]) * hc[:, FF_CHUNK:]
    y_ref[...] += _dot(act, wd_ref[...])

    if final_norm:
        @pl.when(j == pl.num_programs(1) - 1)
        def _():
            y_ref[...] = _rms(y_ref[...], fg_ref[...])


def ffn_layouts(w_up, conv_w, conv_b, w_down):
    depth = w_up.shape[0]
    chunks = lambda a: jnp.transpose(a.reshape(depth, a.shape[1], 2 * N_FF, FF_CHUNK), (0, 2, 1, 3))
    return (chunks(w_up), chunks(conv_w), chunks(conv_b.reshape(depth, 1, -1)),
            w_down.reshape(depth, N_FF, FF_CHUNK, D_MODEL))


def _layer_spec(a, layer, **kw):
    return pl.BlockSpec((None,) + a.shape[1:], lambda *_: (layer,) + (0,) * (a.ndim - 1), **kw)


def out_projection_prompt(x, oa, ob, o_list, st_list, dils, expand, ffw, layer, *, tm, tiles_per_seq, final_norm):
    n = x.shape[0]
    nt = n // tm
    row = lambda w: pl.BlockSpec((tm, w), lambda i: (i, 0))
    perm = lambda w: [pl.BlockSpec((tm // d, d * w), lambda i: (i, 0)) for d in dils]
    resident = lambda a: pl.BlockSpec(a.shape, lambda i: (0,) * a.ndim, pipeline_mode=pl.Buffered(1))
    conv_spec = pl.BlockSpec((1, N_FF, SUBLANES, FF_CHUNK), lambda i: (i, 0, 0, 0))
    conv_sd = jax.ShapeDtypeStruct((nt, N_FF, SUBLANES, FF_CHUNK), F32)
    n_unp = sum(C_WIDTH // LANES + 1 for d in dils if d > 1) + S5_BLOCKS
    seg_rows = pl.BlockSpec((tm // SEG, SEG * B_WIDTH), lambda i: (i, 0))
    nhs = 2 * FF_CHUNK // LANES
    return pl.pallas_call(
        functools.partial(_outproj_prompt_kernel, tiles_per_seq=tiles_per_seq, dils=tuple(dils),
                          final_norm=final_norm),
        grid=(nt,),
        in_specs=[row(D_MODEL), row(A_WIDTH), seg_rows] + perm(C_WIDTH) + perm(LANES)
                 + [resident(expand)] + [_layer_spec(a, layer, pipeline_mode=pl.Buffered(1)) for a in ffw],
        out_specs=(row(D_MODEL), conv_spec, conv_spec),
        out_shape=(jax.ShapeDtypeStruct((n, D_MODEL), F32), conv_sd, conv_sd),
        scratch_shapes=[pltpu.VMEM((tm, D_MODEL), BF16),
                        pltpu.VMEM((N_FF, nhs, SUBLANES, LANES), F32),
                        pltpu.VMEM((n_unp, tm, LANES), F32),
                        pltpu.VMEM((2 * nhs, SUBLANES + tm, LANES), F32),
                        pltpu.VMEM((2 * nhs, SUBLANES + tm, LANES), F32),
                        pltpu.VMEM((tm, FF_CHUNK), BF16),
                        pltpu.VMEM((tm, FF_CHUNK), BF16)],
        compiler_params=_params("arbitrary"),
        name="out_projection_prompt",
    )(x, oa, ob, *o_list, *st_list, expand, *ffw)


def out_projection_sample(x, oa, ob, oc, ffw, layer, bufs, *, nseq, seq_len, final_norm):
    n = x.shape[0]
    row = lambda w: pl.BlockSpec((n, w), lambda i, j: (0, 0))
    full = lambda a: _layer_spec(a, layer)
    w_out, g2, w_up, conv_w, conv_b, w_down, fg = ffw
    chunk = lambda a, off: pl.BlockSpec((None, None) + a.shape[2:], lambda i, j: (layer, off + j, 0, 0))
    wspecs = [full(w_out), full(g2), chunk(w_up, 0), chunk(w_up, N_FF), chunk(conv_w, 0), chunk(conv_w, N_FF),
              chunk(conv_b, 0), chunk(conv_b, N_FF), chunk(w_down, 0), full(fg)]
    wargs = [w_out, g2, w_up, w_up, conv_w, conv_w, conv_b, conv_b, w_down, fg]
    conv_spec = pl.BlockSpec((nseq, SUBLANES, FF_CHUNK), lambda i, j: (0, 0, j))
    return pl.pallas_call(
        functools.partial(_outproj_sample_kernel, nseq=nseq, seq_len=seq_len, final_norm=final_norm),
        grid=(1, N_FF),
        in_specs=[row(D_MODEL), row(A_WIDTH), row(B_WIDTH), row(C_WIDTH)] + wspecs
                 + [pl.BlockSpec((nseq, 2, FF_CHUNK), lambda i, j: (0, 0, j)),
                    pl.BlockSpec((nseq, 2, FF_CHUNK), lambda i, j: (0, 0, N_FF + j))],
        out_specs=(row(D_MODEL), conv_spec, conv_spec),
        out_shape=(jax.ShapeDtypeStruct((n, D_MODEL), F32),
                   jax.ShapeDtypeStruct((nseq, SUBLANES, D_FF), F32),
                   jax.ShapeDtypeStruct((nseq, SUBLANES, D_FF), F32)),
        scratch_shapes=[pltpu.VMEM((n, D_MODEL), BF16),
                        pltpu.VMEM((nseq, SUBLANES + seq_len, 2 * FF_CHUNK), F32)],
        compiler_params=_params("arbitrary", "arbitrary"),
        name="out_projection_sample",
    )(x, oa, ob, oc, *wargs, bufs, bufs)


PROMPT_TM = 512
S5_ROWS = 256
ATTN_TQ = 512


def kernel(x_prompt, x_sample, cache_c_k, cache_c_v, state_ssm_re, state_ssm_im, state_ffn_conv, norm1_g, w_in, w_s, b_s, ssm_lam_re, ssm_lam_im, ssm_log_dt, ssm_b_re, ssm_b_im, ssm_c_re, ssm_c_im, ssm_d, w_glu, b_glu, w_out, norm2_g, w_up, conv_w, conv_b, w_down, final_g):
    bp, sp, _ = x_prompt.shape
    nseq, steps, _ = x_sample.shape
    ns = nseq * steps
    cw_p = min(MAX_WINDOW, sp)

    rope_p = rope_tables(sp, 0)
    rope_s = tuple(jnp.tile(t, (nseq, 1)) for t in rope_tables(steps, PAST_LEN))
    bbr, bbi, pr, pi, ncim = s5_params(ssm_lam_re, ssm_lam_im, ssm_log_dt, ssm_b_re, ssm_b_im, ssm_c_im, SEG)
    lag_k, abr, abi, car, cani = s5_seg_params(pr, pi, bbr, bbi, ssm_c_re, ncim)
    segw = s5_seg_layouts(lag_k, abr, abi, car, cani, pr, pi, ssm_d)

    w_in_b, w_out_b, w_up_b, w_down_b, w_glu_b = (w.astype(BF16) for w in (w_in, w_out, w_up, w_down, w_glu))
    expand = (jnp.arange(2 * LANES)[:, None] % LANES == jnp.arange(C_WIDTH)[None, :] // HEAD_DIM).astype(BF16)
    bs_p = jnp.repeat(jnp.transpose(b_s, (0, 2, 1)), HEAD_DIM, axis=2)
    bs_s = jnp.tile(bs_p[:, :steps], (1, CHUNK // steps, 1))
    ws_s = jnp.tile(w_s[:, :, :steps, :steps], (1, 1, CHUNK // steps, CHUNK // steps))
    row1 = lambda a: a.reshape(1, -1)
    rows1 = lambda a: a.reshape(a.shape[0], 1, a.shape[1])
    ffw = (w_out_b, rows1(norm2_g), *ffn_layouts(w_up_b, conv_w, conv_b, w_down_b),
           jnp.broadcast_to(final_g.reshape(1, 1, -1), (DEPTH, 1, D_MODEL)))
    segw = segw + (w_glu_b, rows1(b_glu))
    g1 = rows1(norm1_g)

    cache_kt = jnp.transpose(cache_c_k, (0, 1, 3, 4, 2))
    cache_vt = jnp.transpose(cache_c_v, (0, 1, 3, 4, 2))
    dils = tuple(d for _, d in DILATIONS)

    hp = x_prompt
    hs = x_sample.reshape(1, ns, D_MODEL)
    outs = [[] for _ in range(11)]
    for l in range(DEPTH):
        last = l == DEPTH - 1
        wb, wc, prl, pil = s5_layouts(bbr[l], bbi[l], ssm_c_re[l], ncim[l], pr[l], pi[l])
        s5w = (wb, wc, prl, pil, row1(ssm_d[l]), w_glu_b[l], row1(b_glu[l]))

        oa, _, ub, kf, vf, *qkv = in_projection(hp, g1, w_in_b, l, w_s[l], bs_p[l], rope_p,
                                                tm=PROMPT_TM, seq_rows=CHUNK, tail_rows=cw_p, dils=dils,
                                                seg_major=True)
        nd = len(dils)
        ob, hr, hi = s5_seg(ub, *segw, l, rows=S5_ROWS)
        o_list, st_list = zip(*[attn_prompt(qkv[n], qkv[nd + n], qkv[2 * nd + n], dil, tq=ATTN_TQ)
                                for n, dil in enumerate(dils)])
        flat = lambda a: a.reshape(a.shape[0] * a.shape[1], a.shape[2])
        y, cg, cu = out_projection_prompt(flat(hp), flat(oa), flat(ob), [flat(o) for o in o_list],
                                          [flat(s) for s in st_list], dils, expand, ffw, l,
                                          tm=PROMPT_TM, tiles_per_seq=sp // PROMPT_TM, final_norm=last)
        hp = y.reshape(bp, sp, D_MODEL)
        outs[0].append(kf.reshape(bp, cw_p, C_HEADS, HEAD_DIM))
        outs[1].append(vf.reshape(bp, cw_p, C_HEADS, HEAD_DIM))
        outs[4].append(hr.reshape(bp, B_GROUPS, SSM_STATE))
        outs[5].append(hi.reshape(bp, B_GROUPS, SSM_STATE))
        seq_end = lambda a: jnp.transpose(a.reshape(bp, sp // PROMPT_TM, N_FF, SUBLANES, FF_CHUNK)[:, -1, :, -2:],
                                          (0, 2, 1, 3)).reshape(bp, 2, D_FF)
        outs[8].append(jnp.concatenate([seq_end(cg), seq_end(cu)], -1))

        oa, vn, ub, kf, vf, q, k, v = in_projection(hs, g1, w_in_b, l, ws_s[l], bs_s[l], rope_s,
                                                    tm=ns, seq_rows=steps, tail_rows=ns, dils=(1,))
        h0r = jnp.transpose(state_ssm_re[l].reshape(nseq, S5_BLOCKS, S5_HALF), (1, 0, 2))
        h0i = jnp.transpose(state_ssm_im[l].reshape(nseq, S5_BLOCKS, S5_HALF), (1, 0, 2))
        ob, hr, hi = s5_sample(ub[0], h0r, h0i, *s5w, nseq=nseq, steps=steps)
        oc = attn_sample(q[0].astype(F32), k[0].astype(F32), v[0].astype(F32), cache_kt, cache_vt, l, steps=steps)
        y, cg, cu = out_projection_sample(hs[0], oa[0], ob, oc, ffw, l, state_ffn_conv[l],
                                          nseq=nseq, seq_len=steps, final_norm=last)
        hs = y.reshape(1, ns, D_MODEL)
        outs[2].append(kf.reshape(nseq, steps, C_HEADS, HEAD_DIM))
        outs[3].append(vf.reshape(nseq, steps, C_HEADS, HEAD_DIM))
        outs[6].append(jnp.transpose(hr, (1, 0, 2)).reshape(nseq, B_GROUPS, SSM_STATE))
        outs[7].append(jnp.transpose(hi, (1, 0, 2)).reshape(nseq, B_GROUPS, SSM_STATE))
        outs[9].append(jnp.concatenate([cg[:, -2:], cu[:, -2:]], -1))
        outs[10].append(vn.reshape(nseq, steps, A_WIDTH))

    return (hp, hs.reshape(nseq, steps, D_MODEL)) + tuple(jnp.stack(o) for o in outs)
```
